```python
import jax, jax.numpy as jnp
from jax import lax
import numpy as np

D_MODEL = 1024
BATCH = 4
SEQ = 4096
DEPTH = 2

GRID_W = 64
CTX_LEN = 256
HEAD_DIM = 64
ATTN_SCALE = HEAD_DIM ** -0.5
ROPE_THETA = 10000.0
A_HEADS = 4
A_KV = 2
WINDOW = 128
BAND_BLOCK = 128
B_HEADS = 4
B_KV = 2
Q_BLOCK = 128
C_HEADS = 4
C_KDIM = 128
C_VDIM = 128
C_CHUNK = 64
A_WIDTH = A_HEADS * HEAD_DIM
B_WIDTH = B_HEADS * HEAD_DIM
C_WIDTH = C_HEADS * C_VDIM
MIX_WIDTH = A_WIDTH + B_WIDTH + C_WIDTH
IN_SIZES = (A_HEADS * HEAD_DIM, A_KV * HEAD_DIM, A_KV * HEAD_DIM,
            B_HEADS * HEAD_DIM, B_KV * HEAD_DIM, B_KV * HEAD_DIM,
            C_HEADS * C_KDIM, C_HEADS * C_KDIM, C_HEADS * C_KDIM, C_HEADS * C_VDIM, C_HEADS * C_VDIM)
IN_WIDTH = sum(IN_SIZES)
SPLIT_IDX = tuple(int(s) for s in np.cumsum(IN_SIZES)[:-1])
N_EXPERTS = 32
TOP_K = 4
D_FF = 1024
SWIGLU_ALPHA = 1.702
SWIGLU_LIMIT = 7.0
MOE_BLOCK = 256
EPS = 1e-6
NEG_INF = -1e30
TINY = 1e-30

kernel_name = 'hybrid_flow_trunk_block'


def rms_norm(x, w):
    xf = x.astype(jnp.float32)
    y = xf * lax.rsqrt(jnp.mean(xf * xf, axis=-1, keepdims=True) + EPS)
    return (y * w.astype(jnp.float32)).astype(x.dtype)


def heads(a, n):
    return a.reshape(a.shape[0], a.shape[1], n, -1)


def group(a, kv):
    return a.reshape(a.shape[0], a.shape[1], kv, -1, a.shape[-1])


def flip_seq(a):
    return jnp.flip(a, axis=1)


def axial_rope_tables(n):
    rows = n // GRID_W
    row = jnp.repeat(jnp.arange(rows), GRID_W).astype(jnp.float32)
    col = jnp.tile(jnp.arange(GRID_W), rows).astype(jnp.float32)
    half = HEAD_DIM // 2
    inv = 1.0 / (ROPE_THETA ** (jnp.arange(0, half, 2, dtype=jnp.float32) / half))
    ang_r = row[:, None] * inv
    ang_c = col[:, None] * inv
    ang = jnp.concatenate([ang_r, ang_r, ang_c, ang_c], axis=-1)
    return jnp.cos(ang), jnp.sin(ang)


def apply_rope(x, cos, sin):
    x1, x2, x3, x4 = jnp.split(x, 4, axis=-1)
    rot = jnp.concatenate([-x2, x1, -x4, x3], axis=-1)
    return x * cos[:, None].astype(x.dtype) + rot * sin[:, None].astype(x.dtype)


def dense_attend(q, k, v, sink=None):
    s = jnp.einsum('bqkgd,bskd->bkgqs', q, k).astype(jnp.float32) * ATTN_SCALE
    if sink is not None:
        kv, g = q.shape[2], q.shape[3]
        sc = jnp.broadcast_to(sink.reshape(kv, g)[None, :, :, None, None].astype(jnp.float32), s.shape[:-1] + (1,))
        s = jnp.concatenate([s, sc], axis=-1)
    p = jax.nn.softmax(s, axis=-1)
    if sink is not None:
        p = p[..., :-1]
    return jnp.einsum('bkgqs,bskd->bqkgd', p.astype(v.dtype), v)


def window_attention(q, k, v, kc, vc, sink):
    B, N, H, Dh = q.shape
    KV = k.shape[2]
    G = H // KV
    W = BAND_BLOCK
    nb = N // W
    qb = q.reshape(B, nb, W, KV, G, Dh)

    def band(a):
        pad = jnp.zeros((B, W, KV, Dh), a.dtype)
        ab = jnp.concatenate([pad, a, pad], axis=1).reshape(B, nb + 2, W, KV, Dh)
        return jnp.concatenate([ab[:, :-2], ab[:, 1:-1], ab[:, 2:]], axis=2)

    kb, vb = band(k), band(v)
    s_loc = jnp.einsum('bnqkgd,bnskd->bnkgqs', qb, kb).astype(jnp.float32) * ATTN_SCALE
    qi = jnp.arange(W)[:, None]
    kj = jnp.arange(3 * W)[None, :]
    kpos = jnp.arange(nb)[:, None, None] * W - W + kj[None]
    mask = (jnp.abs(qi + W - kj) <= WINDOW)[None] & (kpos >= 0) & (kpos < N)
    s_loc = jnp.where(mask[None, :, None, None], s_loc, NEG_INF)
    s_ctx = jnp.einsum('bnqkgd,bmkd->bnkgqm', qb, kc).astype(jnp.float32) * ATTN_SCALE
    s_sink = jnp.broadcast_to(sink.reshape(KV, G)[None, None, :, :, None, None].astype(jnp.float32), s_loc.shape[:-1] + (1,))
    p = jax.nn.softmax(jnp.concatenate([s_loc, s_ctx, s_sink], axis=-1), axis=-1).astype(v.dtype)
    L = 3 * W
    o = (jnp.einsum('bnkgqs,bnskd->bnqkgd', p[..., :L], vb)
         + jnp.einsum('bnkgqm,bmkd->bnqkgd', p[..., L:-1], vc))
    return o.reshape(B, N, H * Dh)


def blocked_attention(q, k, v):
    B, N, H, Dh = q.shape
    KV = k.shape[2]
    nb = N // Q_BLOCK
    qb = q.reshape(B, nb, Q_BLOCK, KV, H // KV, Dh).transpose(1, 0, 2, 3, 4, 5)
    o = lax.map(lambda qq: dense_attend(qq, k, v), qb)
    return o.transpose(1, 0, 2, 3, 4, 5).reshape(B, N, H * Dh)


def hgrn_forget(z, lb):
    z = z.astype(jnp.float32)
    lb = lb.astype(jnp.float32)
    f = lb + (1.0 - lb) * jax.nn.sigmoid(z)
    logf = jnp.log(jnp.maximum(f, TINY))
    k = (1.0 - lb) * jax.nn.sigmoid(-z)
    return logf, k


def chunk_scan(q, k, v, logf, s0):
    B, L, H, K = q.shape
    V = v.shape[-1]
    nc = L // C_CHUNK

    def chunks(a):
        return a.astype(jnp.float32).reshape(B, nc, C_CHUNK, H, a.shape[-1]).transpose(1, 0, 3, 2, 4)

    tri = jnp.tril(jnp.ones((C_CHUNK, C_CHUNK), bool))[:, :, None]

    def step(S, inp):
        qc, kc, vc, gc = inp
        b = jnp.cumsum(gc, axis=2)
        inter = jnp.einsum('bhtk,bhkv->bhtv', qc * jnp.exp(b), S)
        diff = jnp.where(tri, b[:, :, :, None, :] - b[:, :, None, :, :], 0.0)
        decay = jnp.where(tri, jnp.exp(diff), 0.0)
        attn = jnp.einsum('bhtk,bhtsk,bhsk->bhts', qc, decay, kc)
        out = inter + jnp.einsum('bhts,bhsv->bhtv', attn, vc)
        b_end = b[:, :, -1:, :]
        S = jnp.exp(b_end[:, :, 0, :, None]) * S + jnp.einsum('bhsk,bhsv->bhkv', kc * jnp.exp(b_end - b), vc)
        return S, out

    S, o = lax.scan(step, s0, (chunks(q), chunks(k), chunks(v), chunks(logf)))
    return o.transpose(1, 0, 3, 2, 4).reshape(B, L, H, V), S


def final_state(k, v, logf):
    b = jnp.cumsum(logf, axis=1)
    w = k * jnp.exp(b[:, -1:] - b)
    return jnp.einsum('blhk,blhv->bhkv', w, v.astype(jnp.float32))


def hgrn_readout(o, g_pre, w, dtype):
    y = rms_norm(o, w) * jax.nn.silu(heads(g_pre, C_HEADS).astype(jnp.float32))
    return y.reshape(o.shape[0], o.shape[1], -1).astype(dtype)


def mix_layer(h, hc, w_in, w_out, sink, q_norm_w, k_norm_w, lb, gate_norm_w, cos, sin, ctx_out):
    B, N, _ = h.shape
    M = hc.shape[1]
    wa_q, wa_k, wa_v, wb_q, wb_k, wb_v, wc_q, wc_ff, wc_fb, wc_i, wc_g = jnp.split(w_in, SPLIT_IDX, axis=1)
    aq, ak, av, bq, bk, bv, cq, cff, cfb, ci, cg = jnp.split(h @ w_in, SPLIT_IDX, axis=-1)
    a_kc = heads(hc @ wa_k, A_KV)
    a_vc = heads(hc @ wa_v, A_KV)
    ya = window_attention(apply_rope(heads(aq, A_HEADS), cos, sin), apply_rope(heads(ak, A_KV), cos, sin),
                          heads(av, A_KV), a_kc, a_vc, sink)
    b_kc = rms_norm(heads(hc @ wb_k, B_KV), k_norm_w)
    b_vc = heads(hc @ wb_v, B_KV)
    b_q = apply_rope(rms_norm(heads(bq, B_HEADS), q_norm_w), cos, sin)
    b_k = apply_rope(rms_norm(heads(bk, B_KV), k_norm_w), cos, sin)
    yb = blocked_attention(b_q, jnp.concatenate([b_k, b_kc], axis=1),
                           jnp.concatenate([heads(bv, B_KV), b_vc], axis=1))
    lb_f = lb[0].reshape(C_HEADS, C_KDIM)
    lb_b = lb[1].reshape(C_HEADS, C_KDIM)
    c_vc = heads(hc @ wc_i, C_HEADS)
    g_cf, k_cf = hgrn_forget(heads(hc @ wc_ff, C_HEADS), lb_f)
    g_cb, k_cb = hgrn_forget(heads(hc @ wc_fb, C_HEADS), lb_b)
    if ctx_out:
        s0 = jnp.zeros((B, C_HEADS, C_KDIM, C_VDIM), jnp.float32)
        c_qc = jax.nn.silu(heads(hc @ wc_q, C_HEADS))
        oc_f, s_f = chunk_scan(c_qc, k_cf, c_vc, g_cf, s0)
        oc_b, s_b = chunk_scan(flip_seq(c_qc), flip_seq(k_cb), flip_seq(c_vc), flip_seq(g_cb), s0)
    else:
        s_f = final_state(k_cf, c_vc, g_cf)
        s_b = final_state(flip_seq(k_cb), flip_seq(c_vc), flip_seq(g_cb))
    c_q = jax.nn.silu(heads(cq, C_HEADS))
    c_v = heads(ci, C_HEADS)
    g_f, k_f = hgrn_forget(heads(cff, C_HEADS), lb_f)
    g_b, k_b = hgrn_forget(heads(cfb, C_HEADS), lb_b)
    o_f, _ = chunk_scan(c_q, k_f, c_v, g_f, s_f)
    o_b, _ = chunk_scan(flip_seq(c_q), flip_seq(k_b), flip_seq(c_v), flip_seq(g_b), s_b)
    yc = hgrn_readout(o_f + flip_seq(o_b), cg, gate_norm_w, h.dtype)
    y = jnp.concatenate([ya, yb, yc], axis=-1) @ w_out
    if not ctx_out:
        return y, None
    yac = dense_attend(group(heads(hc @ wa_q, A_HEADS), A_KV), a_kc, a_vc, sink).reshape(B, M, -1)
    ybc = dense_attend(group(rms_norm(heads(hc @ wb_q, B_HEADS), q_norm_w), B_KV), b_kc, b_vc).reshape(B, M, -1)
    ycc = hgrn_readout(oc_f + flip_seq(oc_b), hc @ wc_g, gate_norm_w, hc.dtype)
    return y, jnp.concatenate([yac, ybc, ycc], axis=-1) @ w_out


def clamped_swiglu(u):
    u_glu = jnp.minimum(u[..., ::2], SWIGLU_LIMIT)
    u_lin = jnp.clip(u[..., 1::2], -SWIGLU_LIMIT, SWIGLU_LIMIT)
    return u_glu * jax.nn.sigmoid(SWIGLU_ALPHA * u_glu) * (u_lin + 1.0)


def moe(h, router_w, router_b, w1, b1, w2, b2):
    T, D = h.shape
    logits = (h @ router_w + router_b).astype(jnp.float32)
    top_v, top_i = lax.top_k(logits, TOP_K)
    gates = jax.nn.softmax(top_v, axis=-1)
    TK = T * TOP_K
    flat_e = top_i.reshape(-1)
    flat_tok = jnp.repeat(jnp.arange(T), TOP_K)
    order = jnp.argsort(flat_e)
    se = flat_e[order]
    st = flat_tok[order]
    sg = gates.reshape(-1)[order]
    counts = jnp.bincount(flat_e, length=N_EXPERTS)
    padded = (counts + MOE_BLOCK - 1) // MOE_BLOCK * MOE_BLOCK
    start = jnp.cumsum(counts) - counts
    pend = jnp.cumsum(padded)
    pstart = pend - padded
    dest = pstart[se] + jnp.arange(TK) - start[se]
    n_blocks = -(-(TK + N_EXPERTS * (MOE_BLOCK - 1)) // MOE_BLOCK)
    P = n_blocks * MOE_BLOCK
    buf_tok = jnp.zeros((P,), jnp.int32).at[dest].set(st.astype(jnp.int32))
    buf_gate = jnp.zeros((P,), h.dtype).at[dest].set(sg.astype(h.dtype))
    blk_e = jnp.minimum(jnp.searchsorted(pend, jnp.arange(n_blocks) * MOE_BLOCK, side='right'), N_EXPERTS - 1)

    def step(acc, inp):
        tok, g, e = inp
        u = h[tok] @ w1[e] + b1[e]
        y = clamped_swiglu(u) @ w2[e] + b2[e]
        return acc.at[tok].add(y * g[:, None]), None

    out, _ = lax.scan(step, jnp.zeros_like(h),
                      (buf_tok.reshape(n_blocks, MOE_BLOCK), buf_gate.reshape(n_blocks, MOE_BLOCK), blk_e))
    return out


def setup_inputs(seed: int = 0) -> dict:
    key = jax.random.key(seed)
    ks = jax.random.split(key, 22)
    d = D_MODEL

    def normal(k, shape, scale):
        return scale * jax.random.normal(k, shape, jnp.float32)

    return {
        'x': normal(ks[0], (BATCH, SEQ, d), 1.0),
        'c': normal(ks[1], (BATCH, d), 1.0),
        'ctx': normal(ks[2], (BATCH, CTX_LEN, d), 1.0),
        'c_ctx': normal(ks[3], (d,), 1.0),
        'w_mod': normal(ks[4], (DEPTH, d, 6 * d), 0.5 * d ** -0.5),
        'b_mod': normal(ks[5], (DEPTH, 6 * d), 0.01),
        'norm1_w': 1.0 + normal(ks[6], (DEPTH, d), 0.05),
        'norm2_w': 1.0 + normal(ks[7], (DEPTH, d), 0.05),
        'w_in': normal(ks[8], (DEPTH, d, IN_WIDTH), d ** -0.5),
        'w_out': normal(ks[9], (DEPTH, MIX_WIDTH, d), MIX_WIDTH ** -0.5),
        'attn_sink': normal(ks[10], (DEPTH, A_HEADS), 1.0),
        'q_norm_w': 1.0 + normal(ks[11], (DEPTH, HEAD_DIM), 0.05),
        'k_norm_w': 1.0 + normal(ks[12], (DEPTH, HEAD_DIM), 0.05),
        'hgrn_lb': normal(ks[13], (DEPTH, 2, C_HEADS * C_KDIM), 0.5),
        'gate_norm_w': 1.0 + normal(ks[14], (DEPTH, C_VDIM), 0.05),
        'router_w': normal(ks[15], (DEPTH, d, N_EXPERTS), d ** -0.5),
        'router_b': normal(ks[16], (DEPTH, N_EXPERTS), 0.01),
        'w1': normal(ks[17], (DEPTH, N_EXPERTS, d, 2 * D_FF), d ** -0.5),
        'b1': normal(ks[18], (DEPTH, N_EXPERTS, 2 * D_FF), 0.01),
        'w2': normal(ks[19], (DEPTH, N_EXPERTS, D_FF, d), D_FF ** -0.5),
        'b2': normal(ks[20], (DEPTH, N_EXPERTS, d), 0.01),
        'final_norm_w': 1.0 + normal(ks[21], (d,), 0.05),
    }


def reference(x, c, ctx, c_ctx, w_mod, b_mod, norm1_w, norm2_w, w_in, w_out, attn_sink, q_norm_w,
              k_norm_w, hgrn_lb, gate_norm_w, router_w, router_b, w1, b1, w2, b2, final_norm_w):
    B, N, D = x.shape
    cos, sin = axial_rope_tables(N)
    lbs = jax.nn.softmax(hgrn_lb.astype(jnp.float32), axis=0)
    lbs = jnp.cumsum(lbs, axis=0) - lbs[0]
    xc = ctx
    for l in range(DEPTH):
        ctx_out = l < DEPTH - 1
        mod = jax.nn.silu(c) @ w_mod[l] + b_mod[l]
        mod_c = jax.nn.silu(c_ctx) @ w_mod[l] + b_mod[l]
        sh1, sc1, g1, sh2, sc2, g2 = jnp.split(mod[:, None, :], 6, axis=-1)
        csh1, csc1, cg1, csh2, csc2, cg2 = jnp.split(mod_c, 6, axis=-1)
        h = rms_norm(x, norm1_w[l]) * (1.0 + sc1) + sh1
        hc = rms_norm(xc, norm1_w[l]) * (1.0 + csc1) + csh1
        y, yc = mix_layer(h, hc, w_in[l], w_out[l], attn_sink[l], q_norm_w[l], k_norm_w[l], lbs[l],
                          gate_norm_w[l], cos, sin, ctx_out)
        x = x + g1 * y
        h2 = rms_norm(x, norm2_w[l]) * (1.0 + sc2) + sh2
        if ctx_out:
            xc = xc + cg1 * yc
            hc2 = rms_norm(xc, norm2_w[l]) * (1.0 + csc2) + csh2
            tokens = jnp.concatenate([h2.reshape(-1, D), hc2.reshape(-1, D)], axis=0)
            f = moe(tokens, router_w[l], router_b[l], w1[l], b1[l], w2[l], b2[l])
            x = x + g2 * f[:B * N].reshape(B, N, D)
            xc = xc + cg2 * f[B * N:].reshape(B, -1, D)
        else:
            f = moe(h2.reshape(-1, D), router_w[l], router_b[l], w1[l], b1[l], w2[l], b2[l])
            x = x + g2 * f.reshape(B, N, D)
    return rms_norm(x, final_norm_w)
```

```python
import functools

import numpy as np
import jax
import jax.numpy as jnp
from jax import lax
from jax.experimental import pallas as pl
from jax.experimental.pallas import tpu as pltpu

F32 = jnp.float32
BF16 = jnp.bfloat16

D_MODEL = 1024
BATCH = 4
SEQ = 4096
DEPTH = 2
GRID_W = 64
CTX_LEN = 256
HEAD_DIM = 64
ATTN_SCALE = HEAD_DIM ** -0.5
ROPE_THETA = 10000.0
A_HEADS = 4
B_HEADS = 4
WINDOW = 128
C_HEADS = 4
C_KDIM = 128
C_VDIM = 128
IN_WIDTH = 3584
MIX_WIDTH = 1024
N_EXPERTS = 32
TOP_K = 4
D_FF = 1024
SWIGLU_ALPHA = 1.702
SWIGLU_LIMIT = 7.0
MOE_BLOCK = 256
EPS = 1e-6
NEG_INF = -1e30
TINY = 1e-30

N_LAT = BATCH * SEQ
N_CTX = BATCH * CTX_LEN
T_ALL = N_LAT + N_CTX
ROW_BLK = 256
LANES = 128
SCAN_CHUNK = 128
SCAN_LEVELS = 7
VMEM_LIMIT = 56 * 1024 * 1024


def _cparams(sem):
    return pltpu.CompilerParams(dimension_semantics=sem, vmem_limit_bytes=VMEM_LIMIT)


def _split_bf16(a):
    hi = a.astype(BF16)
    lo = (a - hi.astype(F32)).astype(BF16)
    return hi, lo


def _dot(a, b):
    return jnp.dot(a, b, preferred_element_type=F32)


def _dot_nt(a, b):
    return lax.dot_general(a, b, (((1,), (1,)), ((), ())), preferred_element_type=F32)


def _silu(a):
    return a * jax.nn.sigmoid(a)


def _mod_kernel(c_ref, w_ref, b_ref, o_ref):
    a = _silu(c_ref[...]).astype(BF16)
    o_ref[0] = _dot(a, w_ref[0].astype(BF16)) + b_ref[0]


def _modulation(c_all, w_mod, b_mod):
    tn = 1536
    return pl.pallas_call(
        _mod_kernel,
        grid=(DEPTH, 6 * D_MODEL // tn),
        in_specs=[pl.BlockSpec((8, D_MODEL), lambda l, j: (0, 0)),
                  pl.BlockSpec((1, D_MODEL, tn), lambda l, j: (l, 0, j)),
                  pl.BlockSpec((1, 1, tn), lambda l, j: (l, 0, j))],
        out_specs=pl.BlockSpec((1, 8, tn), lambda l, j: (l, 0, j)),
        out_shape=jax.ShapeDtypeStruct((DEPTH, 8, 6 * D_MODEL), F32),
        compiler_params=_cparams(("arbitrary", "arbitrary")),
        name="modulation",
    )(c_all, w_mod, b_mod.reshape(DEPTH, 1, 6 * D_MODEL))


def _in_kernel(x_ref, mod_ref, n1_ref, w_ref, cos_ref, sa_ref, sb_ref, qn_ref, kn_ref, lb_ref, gm_ref,
               qa_ref, ka_ref, va_ref, qb_ref, kb_ref, vb_ref, hq_ref, hk_ref, hg_ref, hv_ref, hgate_ref):
    i = pl.program_id(0)
    midx = jnp.minimum(i // (SEQ // ROW_BLK), BATCH)
    mod = mod_ref[pl.ds(midx, 1), :]
    sh1 = mod[:, 0:D_MODEL]
    sc1 = mod[:, D_MODEL:2 * D_MODEL]
    x = x_ref[...]
    h = x * lax.rsqrt(jnp.mean(x * x, axis=-1, keepdims=True) + EPS) * n1_ref[...]
    hb = (h * (1.0 + sc1) + sh1).astype(BF16)

    cos = cos_ref[...]
    sa = sa_ref[...]
    sb = sb_ref[...]
    gm = gm_ref[...]

    def rope(t):
        return t * cos + pltpu.roll(t, LANES - 16, 1) * sa + pltpu.roll(t, 16, 1) * sb

    def headnorm(t, w):
        hi, lo = _split_bf16(t * t)
        ss = _dot(hi, gm) + _dot(lo, gm)
        return t * lax.rsqrt(ss * (1.0 / HEAD_DIM) + EPS) * w

    pa = _dot(hb, w_ref[:, 0:1024])
    for c in range(2):
        qa_ref[:, c * LANES:(c + 1) * LANES] = (rope(pa[:, c * LANES:(c + 1) * LANES]) * ATTN_SCALE).astype(BF16)
    ka_ref[...] = rope(pa[:, 256:384]).astype(BF16)
    va_ref[...] = pa[:, 384:512].astype(BF16)
    qn = qn_ref[...]
    kn = kn_ref[...]
    for c in range(2):
        t = headnorm(pa[:, 512 + c * LANES:512 + (c + 1) * LANES], qn)
        qb_ref[:, c * LANES:(c + 1) * LANES] = (rope(t) * ATTN_SCALE).astype(BF16)
    kb_ref[...] = rope(headnorm(pa[:, 768:896], kn)).astype(BF16)
    vb_ref[...] = pa[:, 896:1024].astype(BF16)

    hq_ref[...] = _silu(_dot(hb, w_ref[:, 1024:1536]))
    for d in range(2):
        z = _dot(hb, w_ref[:, 1536 + 512 * d:2048 + 512 * d])
        lb = lb_ref[d:d + 1, :]
        e = jnp.exp(-jnp.abs(z))
        r = 1.0 / (1.0 + e)
        er = e * r
        pos = z >= 0.0
        sig = jnp.where(pos, r, er)
        sig_neg = jnp.where(pos, er, r)
        f = lb + (1.0 - lb) * sig
        hg_ref[d] = jnp.log(jnp.maximum(f, TINY))
        hk_ref[d] = (1.0 - lb) * sig_neg
    hv_ref[...] = _dot(hb, w_ref[:, 2560:3072])
    hgate_ref[...] = _silu(_dot(hb, w_ref[:, 3072:3584]))


def _in_proj(x_all, mods_l, n1, w_in_b, cos_t, sa_t, sb_t, qn, kn, lb, gm):
    nblk = T_ALL // ROW_BLK
    row = lambda w: pl.BlockSpec((ROW_BLK, w), lambda i: (i, 0))
    full = lambda a: pl.BlockSpec(a.shape, lambda i: (0,) * a.ndim)
    tab = pl.BlockSpec((ROW_BLK, LANES), lambda i: (jnp.where(i < N_LAT // ROW_BLK, i % (SEQ // ROW_BLK), SEQ // ROW_BLK), 0))
    two = pl.BlockSpec((2, ROW_BLK, 512), lambda i: (0, i, 0))
    sd = lambda w, dt: jax.ShapeDtypeStruct((T_ALL, w), dt)
    return pl.pallas_call(
        _in_kernel,
        grid=(nblk,),
        in_specs=[row(D_MODEL), full(mods_l), full(n1), full(w_in_b), tab, tab, tab, full(qn), full(kn), full(lb), full(gm)],
        out_specs=[row(256), row(128), row(128), row(256), row(128), row(128), row(512), two, two, row(512), row(512)],
        out_shape=[sd(256, BF16), sd(128, BF16), sd(128, BF16), sd(256, BF16), sd(128, BF16), sd(128, BF16),
                   sd(512, F32), jax.ShapeDtypeStruct((2, T_ALL, 512), F32), jax.ShapeDtypeStruct((2, T_ALL, 512), F32),
                   sd(512, F32), sd(512, F32)],
        compiler_params=_cparams(("arbitrary",)),
        name="in_proj",
    )(x_all, mods_l, n1, w_in_b, cos_t, sa_t, sb_t, qn, kn, lb, gm)


def _attn_core(q_ref, parts, sink_ref, o_ref, tq):
    lane = lax.broadcasted_iota(jnp.int32, (tq, LANES), 1)
    outs = []
    for h in range(4):
        kvh = h // 2
        q128 = q_ref[:, kvh * LANES:(kvh + 1) * LANES].astype(F32)
        if (h % 2) != kvh:
            q128 = pltpu.roll(q128, HEAD_DIM, 1)
        qm = jnp.where((lane >= kvh * HEAD_DIM) & (lane < (kvh + 1) * HEAD_DIM), q128, 0.0).astype(BF16)
        scores = []
        for (k, _, mask) in parts:
            s = _dot_nt(qm, k)
            if mask is not None:
                s = jnp.where(mask, s, NEG_INF)
            scores.append(s)
        m = scores[0].max(axis=-1, keepdims=True)
        for s in scores[1:]:
            m = jnp.maximum(m, s.max(axis=-1, keepdims=True))
        if sink_ref is not None:
            sk = sink_ref[:, h:h + 1]
            m = jnp.maximum(m, sk)
            den = jnp.exp(sk - m)
        else:
            den = jnp.zeros_like(m)
        acc = jnp.zeros((tq, LANES), F32)
        for s, (_, v, _) in zip(scores, parts):
            p = jnp.exp(s - m)
            den = den + p.sum(axis=-1, keepdims=True)
            acc = acc + _dot(p.astype(BF16), v)
        outs.append(acc / den)
    for c in range(2):
        a, b = outs[2 * c], outs[2 * c + 1]
        if c == 0:
            b = pltpu.roll(b, HEAD_DIM, 1)
        else:
            a = pltpu.roll(a, HEAD_DIM, 1)
        o_ref[:, c * LANES:(c + 1) * LANES] = jnp.where(lane < HEAD_DIM, a, b).astype(BF16)


def _attn_a_kernel(q_ref, kp_ref, kc_ref, kn_ref, vp_ref, vc_ref, vn_ref, kx_ref, vx_ref, sink_ref, o_ref):
    n = pl.program_id(1)
    qi = lax.broadcasted_iota(jnp.int32, (WINDOW, WINDOW), 0)
    kj = lax.broadcasted_iota(jnp.int32, (WINDOW, WINDOW), 1)
    mask_prev = (kj >= qi) & (n > 0)
    mask_next = (kj <= qi) & (n < SEQ // WINDOW - 1)
    parts = [(kp_ref[...], vp_ref[...], mask_prev), (kc_ref[...], vc_ref[...], None),
             (kn_ref[...], vn_ref[...], mask_next), (kx_ref[...], vx_ref[...], None)]
    _attn_core(q_ref, parts, sink_ref, o_ref, WINDOW)


def _attn_a_latent(qa, ka, va, sink):
    nb = SEQ // WINDOW
    q_spec = pl.BlockSpec((WINDOW, 256), lambda b, n: (b * nb + n, 0))
    prev = pl.BlockSpec((WINDOW, LANES), lambda b, n: (b * nb + jnp.maximum(n - 1, 0), 0))
    cur = pl.BlockSpec((WINDOW, LANES), lambda b, n: (b * nb + n, 0))
    nxt = pl.BlockSpec((WINDOW, LANES), lambda b, n: (b * nb + jnp.minimum(n + 1, nb - 1), 0))
    ctx = pl.BlockSpec((CTX_LEN, LANES), lambda b, n: (N_LAT // CTX_LEN + b, 0))
    return pl.pallas_call(
        _attn_a_kernel,
        grid=(BATCH, nb),
        in_specs=[q_spec, prev, cur, nxt, prev, cur, nxt, ctx, ctx, pl.BlockSpec((1, LANES), lambda b, n: (0, 0))],
        out_specs=pl.BlockSpec((WINDOW, 256), lambda b, n: (b * nb + n, 0)),
        out_shape=jax.ShapeDtypeStruct((N_LAT, 256), BF16),
        compiler_params=_cparams(("arbitrary", "arbitrary")),
        name="attn_window",
    )(qa, ka, ka, ka, va, va, va, ka, va, sink)


B_TQ = 256


def _attn_b_kernel(q_ref, kl_ref, vl_ref, kx_ref, vx_ref, o_ref):
    parts = [(kl_ref[...], vl_ref[...], None), (kx_ref[...], vx_ref[...], None)]
    _attn_core(q_ref, parts, None, o_ref, B_TQ)


def _attn_b_latent(qb, kb, vb):
    nq = SEQ // B_TQ
    lat = pl.BlockSpec((SEQ, LANES), lambda b, j: (b, 0))
    ctx = pl.BlockSpec((CTX_LEN, LANES), lambda b, j: (N_LAT // CTX_LEN + b, 0))
    return pl.pallas_call(
        _attn_b_kernel,
        grid=(BATCH, nq),
        in_specs=[pl.BlockSpec((B_TQ, 256), lambda b, j: (b * nq + j, 0)), lat, lat, ctx, ctx],
        out_specs=pl.BlockSpec((B_TQ, 256), lambda b, j: (b * nq + j, 0)),
        out_shape=jax.ShapeDtypeStruct((N_LAT, 256), BF16),
        compiler_params=_cparams(("arbitrary", "arbitrary")),
        name="attn_full",
    )(qb, kb, vb, kb, vb)


def _attn_ctx_sink_kernel(q_ref, kx_ref, vx_ref, sink_ref, o_ref):
    _attn_core(q_ref, [(kx_ref[...], vx_ref[...], None)], sink_ref, o_ref, CTX_LEN)


def _attn_ctx_kernel(q_ref, kx_ref, vx_ref, o_ref):
    _attn_core(q_ref, [(kx_ref[...], vx_ref[...], None)], None, o_ref, CTX_LEN)


def _attn_ctx(q, k, v, sink):
    blk = lambda w: pl.BlockSpec((CTX_LEN, w), lambda b: (N_LAT // CTX_LEN + b, 0))
    in_specs = [blk(256), blk(LANES), blk(LANES)]
    args = [q, k, v]
    if sink is not None:
        in_specs.append(pl.BlockSpec((1, LANES), lambda b: (0, 0)))
        args.append(sink)
    return pl.pallas_call(
        _attn_ctx_sink_kernel if sink is not None else _attn_ctx_kernel,
        grid=(BATCH,),
        in_specs=in_specs,
        out_specs=pl.BlockSpec((CTX_LEN, 256), lambda b: (b, 0)),
        out_shape=jax.ShapeDtypeStruct((N_CTX, 256), BF16),
        compiler_params=_cparams(("arbitrary",)),
        name="attn_ctx_sink" if sink is not None else "attn_ctx",
    )(*args)


def _scan_constants():
    c = SCAN_CHUNK
    t = np.arange(c)[:, None]
    u = np.arange(c)[None, :]
    mats = [(u <= t), (u > t)]
    qs, ks, masks = [], [], []
    for lvl in range(SCAN_LEVELS):
        m = 2 ** lvl
        second = ((t // m) % 2 == 1)
        end_first = (t // (2 * m)) * 2 * m + m - 1
        qs.append(second & (u > end_first) & (u <= t))
        ks.append((~second) & (u > t) & (u <= end_first))
        s = u
        masks.append(second & ((s // m) % 2 == 0) & ((s // (2 * m)) == (t // (2 * m))))
    fwd = np.concatenate(mats + qs + ks, axis=0).astype(np.float32)
    fmask = np.stack(masks).astype(np.float32)
    bwd = fwd.reshape(16, c, c)[:, ::-1, ::-1].reshape(16 * c, c)
    bmask = fmask[:, ::-1, ::-1]
    return np.stack([fwd, bwd]), np.stack([fmask, bmask])


def _scan_kernel(q_ref, k_ref, g_ref, v_ref, mall_ref, mask_ref, o_ref, st_ref):
    j = pl.program_id(2)

    @pl.when(j == 0)
    def _():
        st_ref[...] = jnp.zeros_like(st_ref)

    c = SCAN_CHUNK
    lf = g_ref[0]
    hi, lo = _split_bf16(lf)
    mall = mall_ref[0]
    ex = jnp.exp(_dot(mall, hi) + _dot(mall, lo))
    tot = jnp.exp(jnp.sum(lf, axis=0, keepdims=True))
    ones = jnp.ones((c, c), BF16)
    for h in range(C_HEADS):
        sl = slice(h * C_KDIM, (h + 1) * C_KDIM)
        q = q_ref[:, sl]
        k = k_ref[0, :, sl]
        v = v_ref[:, sl]
        vb = v.astype(BF16)
        exh = lambda idx: ex[idx * c:(idx + 1) * c, sl]
        st = st_ref[h]
        inter = _dot_nt((q * exh(0)).astype(BF16), st.astype(BF16))
        att = jnp.zeros((c, c), F32)
        for lvl in range(SCAN_LEVELS):
            pr = _dot_nt((q * exh(2 + lvl)).astype(BF16), (k * exh(9 + lvl)).astype(BF16))
            att = att + mask_ref[0, lvl] * pr
        diag = _dot((q * k).astype(BF16), ones)
        o_ref[0, :, sl] = inter + _dot(att.astype(BF16), vb) + diag * v
        khat = (k * exh(1)).astype(BF16)
        st_ref[h] = st * tot[:, sl] + _dot(v.T.astype(BF16), khat)


def _hgrn_scan(hq, hk, hg, hv, mall, masks):
    c = SCAN_CHUNK
    n_ctx_chunks = CTX_LEN // c
    n_lat_chunks = SEQ // c
    steps = n_ctx_chunks + n_lat_chunks

    def rb(b, d, j):
        ctx_blk = N_LAT // c + n_ctx_chunks * b + jnp.where(d == 0, j, n_ctx_chunks - 1 - j)
        jl = j - n_ctx_chunks
        lat_blk = b * n_lat_chunks + jnp.where(d == 0, jl, n_lat_chunks - 1 - jl)
        return jnp.where(j < n_ctx_chunks, ctx_blk, lat_blk)

    row = pl.BlockSpec((c, 512), lambda b, d, j: (rb(b, d, j), 0))
    drow = pl.BlockSpec((1, c, 512), lambda b, d, j: (d, rb(b, d, j), 0))
    return pl.pallas_call(
        _scan_kernel,
        grid=(BATCH, 2, steps),
        in_specs=[row, drow, drow, row,
                  pl.BlockSpec((1, 16 * c, c), lambda b, d, j: (d, 0, 0)),
                  pl.BlockSpec((1, SCAN_LEVELS, c, c), lambda b, d, j: (d, 0, 0, 0))],
        out_specs=drow,
        out_shape=jax.ShapeDtypeStruct((2, T_ALL, 512), F32),
        scratch_shapes=[pltpu.VMEM((C_HEADS, C_VDIM, C_KDIM), F32)],
        compiler_params=_cparams(("arbitrary", "arbitrary", "arbitrary")),
        name="hgrn_scan",
    )(hq, hk, hg, hv, mall, masks)


def _out_kernel(ya_ref, yb_ref, of_ref, ob_ref, gate_ref, x_ref, mod_ref, gw_ref, w_ref, n2_ref,
                rwh_ref, rwl_ref, rb_ref, xo_ref, h2_ref, ti_ref, tg_ref):
    i = pl.program_id(0)
    midx = jnp.minimum(i // (SEQ // ROW_BLK), BATCH)
    mod = mod_ref[pl.ds(midx, 1), :]
    g1 = mod[:, 2 * D_MODEL:3 * D_MODEL]
    sh2 = mod[:, 3 * D_MODEL:4 * D_MODEL]
    sc2 = mod[:, 4 * D_MODEL:5 * D_MODEL]

    o = of_ref[0] + ob_ref[0]
    y = _dot(ya_ref[...], w_ref[0:256, :]) + _dot(yb_ref[...], w_ref[256:512, :])
    for h in range(C_HEADS):
        sl = slice(h * C_VDIM, (h + 1) * C_VDIM)
        oh = o[:, sl]
        yn = oh * lax.rsqrt(jnp.mean(oh * oh, axis=-1, keepdims=True) + EPS) * gw_ref[...]
        yc = (yn * gate_ref[:, sl]).astype(BF16)
        y = y + _dot(yc, w_ref[512 + h * C_VDIM:512 + (h + 1) * C_VDIM, :])
    x = x_ref[...] + g1 * y
    xo_ref[...] = x
    h2 = x * lax.rsqrt(jnp.mean(x * x, axis=-1, keepdims=True) + EPS) * n2_ref[...]
    h2 = h2 * (1.0 + sc2) + sh2
    h2_ref[...] = h2

    hi, lo = _split_bf16(h2)
    rwh = rwh_ref[...]
    logits = _dot(hi, rwh) + _dot(hi, rwl_ref[...]) + _dot(lo, rwh) + rb_ref[...]
    lane = lax.broadcasted_iota(jnp.int32, (ROW_BLK, LANES), 1).astype(F32)
    ti = jnp.zeros((ROW_BLK, LANES), F32)
    ex = jnp.zeros((ROW_BLK, LANES), F32)
    den = jnp.zeros((ROW_BLK, 1), F32)
    top = None
    for k in range(TOP_K):
        m = logits.max(axis=-1, keepdims=True)
        idx = jnp.min(jnp.where(logits == m, lane, float(LANES)), axis=-1, keepdims=True)
        if top is None:
            top = m
        e = jnp.exp(m - top)
        den = den + e
        ti = jnp.where(lane == float(k), idx, ti)
        ex = jnp.where(lane == float(k), e, ex)
        logits = jnp.where(lane == idx, -3.0e38, logits)
    ti_ref[...] = ti.astype(jnp.int32)
    tg_ref[...] = ex / den


def _out_proj(n_rows, ya, yb, o, gate, x_all, mods_l, gw, w_out_b, n2, rwh, rwl, rb):
    nblk = n_rows // ROW_BLK
    row = lambda w: pl.BlockSpec((ROW_BLK, w), lambda i: (i, 0))
    full = lambda a: pl.BlockSpec(a.shape, lambda i: (0,) * a.ndim)
    sd = lambda w, dt: jax.ShapeDtypeStruct((n_rows, w), dt)
    return pl.pallas_call(
        _out_kernel,
        grid=(nblk,),
        in_specs=[row(256), row(256),
                  pl.BlockSpec((1, ROW_BLK, 512), lambda i: (0, i, 0)),
                  pl.BlockSpec((1, ROW_BLK, 512), lambda i: (1, i, 0)),
                  row(512), row(D_MODEL), full(mods_l), full(gw), full(w_out_b), full(n2),
                  full(rwh), full(rwl), full(rb)],
        out_specs=[row(D_MODEL), row(D_MODEL), row(LANES), row(LANES)],
        out_shape=[sd(D_MODEL, F32), sd(D_MODEL, F32), sd(LANES, jnp.int32), sd(LANES, F32)],
        compiler_params=_cparams(("arbitrary",)),
        name="out_proj_router",
    )(ya, yb, o, o, gate, x_all, mods_l, gw, w_out_b, n2, rwh, rwl, rb)


def _rank_kernel(ti_ref, tri_ref, rank_ref, cnt_ref, carry_ref):
    i = pl.program_id(0)

    @pl.when(i == 0)
    def _():
        carry_ref[...] = jnp.zeros_like(carry_ref)

    ti = ti_ref[...]
    lane = lax.broadcasted_iota(jnp.int32, (ROW_BLK, LANES), 1)
    onehots = [(lane == ti[:, k:k + 1]).astype(F32) for k in range(TOP_K)]
    tot = onehots[0] + onehots[1] + onehots[2] + onehots[3]
    before = _dot(tri_ref[...], tot.astype(BF16)) + carry_ref[0:1, :]
    r = jnp.zeros((ROW_BLK, LANES), F32)
    for k in range(TOP_K):
        rk = jnp.sum(onehots[k] * before, axis=-1, keepdims=True)
        r = jnp.where(lane == k, rk, r)
    rank_ref[...] = r.astype(jnp.int32)
    carry = carry_ref[0:1, :] + jnp.sum(tot, axis=0, keepdims=True)
    carry_ref[0:1, :] = carry
    cnt_ref[...] = jnp.broadcast_to(carry, (8, LANES))


def _expert_ranks(top_i, tri):
    n_rows = top_i.shape[0]
    return pl.pallas_call(
        _rank_kernel,
        grid=(n_rows // ROW_BLK,),
        in_specs=[pl.BlockSpec((ROW_BLK, LANES), lambda i: (i, 0)),
                  pl.BlockSpec((ROW_BLK, ROW_BLK), lambda i: (0, 0))],
        out_specs=[pl.BlockSpec((ROW_BLK, LANES), lambda i: (i, 0)),
                   pl.BlockSpec((8, LANES), lambda i: (0, 0))],
        out_shape=[jax.ShapeDtypeStruct((n_rows, LANES), jnp.int32), jax.ShapeDtypeStruct((8, LANES), F32)],
        scratch_shapes=[pltpu.VMEM((8, LANES), F32)],
        compiler_params=_cparams(("arbitrary",)),
        name="expert_ranks",
    )(top_i, tri)


GATHER_ROWS = 256


def _row_copy(src_ref, out_ref, sem, src_row, dst_row):
    return pltpu.make_async_copy(src_ref.at[pl.ds(src_row, 1), :], out_ref.at[pl.ds(dst_row, 1), :], sem)


def _gather_kernel(idx_ref, src_ref, out_ref, sem):
    def start(r, carry):
        _row_copy(src_ref, out_ref, sem, idx_ref[0, 0, r], r).start()
        return carry

    lax.fori_loop(0, GATHER_ROWS, start, 0, unroll=8)

    def wait(r, carry):
        _row_copy(src_ref, out_ref, sem, 0, r).wait()
        return carry

    lax.fori_loop(0, GATHER_ROWS, wait, 0, unroll=8)


def _gather_rows(src, idx):
    n = idx.shape[0]
    nblk = n // GATHER_ROWS
    width = src.shape[1]
    return pl.pallas_call(
        _gather_kernel,
        grid=(nblk,),
        in_specs=[pl.BlockSpec((1, 1, GATHER_ROWS), lambda i: (i, 0, 0), memory_space=pltpu.SMEM),
                  pl.BlockSpec(memory_space=pl.ANY)],
        out_specs=pl.BlockSpec((GATHER_ROWS, width), lambda i: (i, 0)),
        out_shape=jax.ShapeDtypeStruct((n, width), src.dtype),
        scratch_shapes=[pltpu.SemaphoreType.DMA(())],
        compiler_params=_cparams(("arbitrary",)),
        name="gather_rows",
    )(idx.reshape(nblk, 1, GATHER_ROWS), src)


def _mlp_kernel(blk_e_ref, nused_ref, x_ref, w1_ref, b1_ref, w2_ref, b2_ref, y_ref):
    i = pl.program_id(0)

    @pl.when(i < nused_ref[0])
    def _():
        u = _dot(x_ref[...].astype(BF16), w1_ref[0]) + b1_ref[0]
        u_glu = jnp.minimum(u[:, 0:D_FF], SWIGLU_LIMIT)
        u_lin = jnp.clip(u[:, D_FF:2 * D_FF], -SWIGLU_LIMIT, SWIGLU_LIMIT)
        act = u_glu * jax.nn.sigmoid(SWIGLU_ALPHA * u_glu) * (u_lin + 1.0)
        y_ref[...] = _dot(act.astype(BF16), w2_ref[0]) + b2_ref[0]

    @pl.when(i >= nused_ref[0])
    def _():
        y_ref[...] = jnp.zeros_like(y_ref)


def _expert_mlp(x_sorted, blk_e, nused, w1p, b1p, w2b, b2):
    n_blocks = x_sorted.shape[0] // MOE_BLOCK
    grid_spec = pltpu.PrefetchScalarGridSpec(
        num_scalar_prefetch=2,
        grid=(n_blocks,),
        in_specs=[pl.BlockSpec((MOE_BLOCK, D_MODEL), lambda i, be, nu: (i, 0)),
                  pl.BlockSpec((1, D_MODEL, 2 * D_FF), lambda i, be, nu: (be[i], 0, 0)),
                  pl.BlockSpec((1, 1, 2 * D_FF), lambda i, be, nu: (be[i], 0, 0)),
                  pl.BlockSpec((1, D_FF, D_MODEL), lambda i, be, nu: (be[i], 0, 0)),
                  pl.BlockSpec((1, 1, D_MODEL), lambda i, be, nu: (be[i], 0, 0))],
        out_specs=pl.BlockSpec((MOE_BLOCK, D_MODEL), lambda i, be, nu: (i, 0)),
    )
    return pl.pallas_call(
        _mlp_kernel,
        grid_spec=grid_spec,
        out_shape=jax.ShapeDtypeStruct(x_sorted.shape, F32),
        compiler_params=_cparams(("arbitrary",)),
        name="expert_mlp",
    )(blk_e, nused, x_sorted, w1p, b1p, w2b, b2)


def _combine_kernel(y0_ref, y1_ref, y2_ref, y3_ref, tg_ref, x_ref, mod_ref, nf_ref, o_ref, *, final):
    i = pl.program_id(0)
    midx = jnp.minimum(i // (SEQ // ROW_BLK), BATCH)
    g2 = mod_ref[pl.ds(midx, 1), 5 * D_MODEL:6 * D_MODEL]
    tg = tg_ref[...]
    f = jnp.zeros((ROW_BLK, D_MODEL), F32)
    for k, y_ref in enumerate((y0_ref, y1_ref, y2_ref, y3_ref)):
        f = f + y_ref[0] * tg[:, k:k + 1]
    x = x_ref[...] + g2 * f
    if final:
        x = x * lax.rsqrt(jnp.mean(x * x, axis=-1, keepdims=True) + EPS) * nf_ref[...]
    o_ref[...] = x


def _combine(y_g, tg, x_mid, mods_l, nf, final):
    n_rows = x_mid.shape[0]
    full = lambda a: pl.BlockSpec(a.shape, lambda i: (0,) * a.ndim)
    yk = lambda k: pl.BlockSpec((1, ROW_BLK, D_MODEL), lambda i: (k, i, 0))
    return pl.pallas_call(
        functools.partial(_combine_kernel, final=final),
        grid=(n_rows // ROW_BLK,),
        in_specs=[yk(0), yk(1), yk(2), yk(3),
                  pl.BlockSpec((ROW_BLK, LANES), lambda i: (i, 0)),
                  pl.BlockSpec((ROW_BLK, D_MODEL), lambda i: (i, 0)), full(mods_l), full(nf)],
        out_specs=pl.BlockSpec((ROW_BLK, D_MODEL), lambda i: (i, 0)),
        out_shape=jax.ShapeDtypeStruct((n_rows, D_MODEL), F32),
        compiler_params=_cparams(("arbitrary",)),
        name="moe_combine_final" if final else "moe_combine",
    )(y_g, y_g, y_g, y_g, tg, x_mid, mods_l, nf)


def _moe(h2, top_i, top_g, x_mid, mods_l, w1p, b1p, w2b, b2, nf, tri, final):
    n_rows = h2.shape[0]
    tk = n_rows * TOP_K
    n_blocks = -(-(tk + N_EXPERTS * (MOE_BLOCK - 1)) // MOE_BLOCK)
    p_rows = n_blocks * MOE_BLOCK

    rank, cnt = _expert_ranks(top_i, tri)
    counts = cnt[0, :N_EXPERTS].astype(jnp.int32)
    padded = (counts + MOE_BLOCK - 1) // MOE_BLOCK * MOE_BLOCK
    pend = jnp.cumsum(padded)
    pstart = pend - padded
    ti4 = top_i[:, :TOP_K]
    dest = pstart[ti4] + rank[:, :TOP_K]
    tok = jnp.broadcast_to(jnp.arange(n_rows, dtype=jnp.int32)[:, None], (n_rows, TOP_K))
    buf_tok = jnp.zeros((p_rows,), jnp.int32).at[dest.reshape(-1)].set(tok.reshape(-1))
    blk_e = jnp.minimum(jnp.searchsorted(pend, jnp.arange(n_blocks, dtype=jnp.int32) * MOE_BLOCK, side='right'),
                        N_EXPERTS - 1).astype(jnp.int32)
    nused = (pend[-1:] // MOE_BLOCK).astype(jnp.int32)

    x_sorted = _gather_rows(h2, buf_tok)
    y_sorted = _expert_mlp(x_sorted, blk_e, nused, w1p, b1p, w2b, b2)
    y_g = _gather_rows(y_sorted, dest.T.reshape(-1)).reshape(TOP_K, n_rows, D_MODEL)
    return _combine(y_g, top_g, x_mid, mods_l, nf, final)


def _rope_tables():
    rows = SEQ // GRID_W
    row = jnp.repeat(jnp.arange(rows), GRID_W).astype(F32)
    col = jnp.tile(jnp.arange(GRID_W), rows).astype(F32)
    half = HEAD_DIM // 2
    inv = 1.0 / (ROPE_THETA ** (jnp.arange(0, half, 2, dtype=F32) / half))
    ang_r = row[:, None] * inv
    ang_c = col[:, None] * inv
    ang = jnp.concatenate([ang_r, ang_r, ang_c, ang_c], axis=-1)
    cos, sin = jnp.cos(ang), jnp.sin(ang)
    first = (jnp.arange(HEAD_DIM) % 32) < 16
    sa = jnp.where(first, -sin, 0.0)
    sb = jnp.where(first, 0.0, sin)
    ext = lambda t, fill: jnp.concatenate(
        [jnp.tile(t, (1, 2)), jnp.full((ROW_BLK, LANES), fill, F32)], axis=0)
    return ext(cos, 1.0), ext(sa, 0.0), ext(sb, 0.0)


def kernel(x, c, ctx, c_ctx, w_mod, b_mod, norm1_w, norm2_w, w_in, w_out, attn_sink, q_norm_w, k_norm_w,
           hgrn_lb, gate_norm_w, router_w, router_b, w1, b1, w2, b2, final_norm_w):
    cos_t, sa_t, sb_t = _rope_tables()
    lbs = jax.nn.softmax(hgrn_lb.astype(F32), axis=0)
    lbs = jnp.cumsum(lbs, axis=0) - lbs[0]
    mall_np, masks_np = _scan_constants()
    mall = jnp.asarray(mall_np, BF16)
    masks = jnp.asarray(masks_np, F32)
    gm = jnp.asarray(np.kron(np.eye(2), np.ones((HEAD_DIM, HEAD_DIM))), BF16)
    tri = jnp.asarray(np.tril(np.ones((ROW_BLK, ROW_BLK)), -1), BF16)

    c_all = jnp.concatenate([c, c_ctx[None, :], jnp.zeros((3, D_MODEL), F32)], axis=0)
    mods = _modulation(c_all, w_mod, b_mod)
    x_all = jnp.concatenate([x.reshape(N_LAT, D_MODEL), ctx.reshape(N_CTX, D_MODEL)], axis=0)
    nf = final_norm_w.reshape(1, D_MODEL)

    out = None
    for l in range(DEPTH):
        last = l == DEPTH - 1
        mods_l = mods[l]
        qn = jnp.tile(q_norm_w[l], 2).reshape(1, LANES)
        kn = jnp.tile(k_norm_w[l], 2).reshape(1, LANES)
        sink = jnp.zeros((1, LANES), F32).at[0, :A_HEADS].set(attn_sink[l])
        (qa, ka, va, qb, kb, vb, hq, hk, hg, hv, hgate) = _in_proj(
            x_all, mods_l, norm1_w[l].reshape(1, D_MODEL), w_in[l].astype(BF16), cos_t, sa_t, sb_t, qn, kn, lbs[l], gm)

        ya = _attn_a_latent(qa, ka, va, sink)
        yb = _attn_b_latent(qb, kb, vb)
        o = _hgrn_scan(hq, hk, hg, hv, mall, masks)
        if not last:
            ya = jnp.concatenate([ya, _attn_ctx(qa, ka, va, sink)], axis=0)
            yb = jnp.concatenate([yb, _attn_ctx(qb, kb, vb, None)], axis=0)
        n_rows = N_LAT if last else T_ALL

        rw = jnp.zeros((D_MODEL, LANES), F32).at[:, :N_EXPERTS].set(router_w[l])
        rwh, rwl = _split_bf16(rw)
        rb = jnp.full((1, LANES), NEG_INF, F32).at[0, :N_EXPERTS].set(router_b[l])
        x_mid, h2, top_i, top_g = _out_proj(
            n_rows, ya, yb, o, hgate, x_all, mods_l, jnp.tile(gate_norm_w[l], 1).reshape(1, C_VDIM),
            w_out[l].astype(BF16), norm2_w[l].reshape(1, D_MODEL), rwh, rwl, rb)

        w1p = jnp.concatenate([w1[l][:, :, 0::2], w1[l][:, :, 1::2]], axis=-1).astype(BF16)
        b1p = jnp.concatenate([b1[l][:, 0::2], b1[l][:, 1::2]], axis=-1).reshape(N_EXPERTS, 1, 2 * D_FF)
        x_all = _moe(h2, top_i, top_g, x_mid, mods_l, w1p, b1p, w2[l].astype(BF16),
                     b2[l].reshape(N_EXPERTS, 1, D_MODEL), nf, tri, last)
        out = x_all
    return out.reshape(BATCH, SEQ, D_MODEL)
```

```python
import functools

import numpy as np
import jax
import jax.numpy as jnp
from jax import lax
from jax.experimental import pallas as pl
from jax.experimental.pallas import tpu as pltpu

F32 = jnp.float32
BF16 = jnp.bfloat16

D_MODEL = 1024
BATCH = 4
SEQ = 4096
DEPTH = 2
GRID_W = 64
CTX_LEN = 256
HEAD_DIM = 64
ATTN_SCALE = HEAD_DIM ** -0.5
ROPE_THETA = 10000.0
A_HEADS = 4
B_HEADS = 4
WINDOW = 128
C_HEADS = 4
C_KDIM = 128
C_VDIM = 128
IN_WIDTH = 3584
MIX_WIDTH = 1024
N_EXPERTS = 32
TOP_K = 4
D_FF = 1024
SWIGLU_ALPHA = 1.702
SWIGLU_LIMIT = 7.0
MOE_BLOCK = 256
EPS = 1e-6
NEG_INF = -1e30
TINY = 1e-30

N_LAT = BATCH * SEQ
N_CTX = BATCH * CTX_LEN
T_ALL = N_LAT + N_CTX
ROW_BLK = 256
LANES = 128
SCAN_CHUNK = 128
SCAN_LEVELS = 7
VMEM_LIMIT = 56 * 1024 * 1024


def _cparams(sem):
    return pltpu.CompilerParams(dimension_semantics=sem, vmem_limit_bytes=VMEM_LIMIT)


def _split_bf16(a):
    hi = a.astype(BF16)
    lo = (a - hi.astype(F32)).astype(BF16)
    return hi, lo


def _dot(a, b):
    return jnp.dot(a, b, preferred_element_type=F32)


def _dot_nt(a, b):
    return lax.dot_general(a, b, (((1,), (1,)), ((), ())), preferred_element_type=F32)


def _silu(a):
    return a * jax.nn.sigmoid(a)


def _mod_kernel(c_ref, w_ref, b_ref, o_ref):
    a = _silu(c_ref[...]).astype(BF16)
    o_ref[0] = _dot(a, w_ref[0].astype(BF16)) + b_ref[0]


def _modulation(c_all, w_mod, b_mod):
    tn = 1536
    return pl.pallas_call(
        _mod_kernel,
        grid=(DEPTH, 6 * D_MODEL // tn),
        in_specs=[pl.BlockSpec((8, D_MODEL), lambda l, j: (0, 0)),
                  pl.BlockSpec((1, D_MODEL, tn), lambda l, j: (l, 0, j)),
                  pl.BlockSpec((1, 1, tn), lambda l, j: (l, 0, j))],
        out_specs=pl.BlockSpec((1, 8, tn), lambda l, j: (l, 0, j)),
        out_shape=jax.ShapeDtypeStruct((DEPTH, 8, 6 * D_MODEL), F32),
        compiler_params=_cparams(("arbitrary", "arbitrary")),
        name="modulation",
    )(c_all, w_mod, b_mod.reshape(DEPTH, 1, 6 * D_MODEL))


def _in_kernel(x_ref, mod_ref, n1_ref, w_ref, cos_ref, sa_ref, sb_ref, qn_ref, kn_ref, lb_ref, gm_ref,
               qa_ref, ka_ref, va_ref, qb_ref, kb_ref, vb_ref, hq_ref, hk_ref, hg_ref, hv_ref, hgate_ref):
    i = pl.program_id(0)
    midx = jnp.minimum(i // (SEQ // ROW_BLK), BATCH)
    mod = mod_ref[pl.ds(midx, 1), :]
    sh1 = mod[:, 0:D_MODEL]
    sc1 = mod[:, D_MODEL:2 * D_MODEL]
    x = x_ref[...]
    h = x * lax.rsqrt(jnp.mean(x * x, axis=-1, keepdims=True) + EPS) * n1_ref[...]
    hb = (h * (1.0 + sc1) + sh1).astype(BF16)

    cos = cos_ref[...]
    sa = sa_ref[...]
    sb = sb_ref[...]
    gm = gm_ref[...]

    def rope(t):
        return t * cos + pltpu.roll(t, LANES - 16, 1) * sa + pltpu.roll(t, 16, 1) * sb

    def headnorm(t, w):
        hi, lo = _split_bf16(t * t)
        ss = _dot(hi, gm) + _dot(lo, gm)
        return t * lax.rsqrt(ss * (1.0 / HEAD_DIM) + EPS) * w

    pa = _dot(hb, w_ref[:, 0:1024])
    for c in range(2):
        qa_ref[:, c * LANES:(c + 1) * LANES] = (rope(pa[:, c * LANES:(c + 1) * LANES]) * ATTN_SCALE).astype(BF16)
    ka_ref[...] = rope(pa[:, 256:384]).astype(BF16)
    va_ref[...] = pa[:, 384:512].astype(BF16)
    qn = qn_ref[...]
    kn = kn_ref[...]
    for c in range(2):
        t = headnorm(pa[:, 512 + c * LANES:512 + (c + 1) * LANES], qn)
        qb_ref[:, c * LANES:(c + 1) * LANES] = (rope(t) * ATTN_SCALE).astype(BF16)
    kb_ref[...] = rope(headnorm(pa[:, 768:896], kn)).astype(BF16)
    vb_ref[...] = pa[:, 896:1024].astype(BF16)

    hq_ref[...] = _silu(_dot(hb, w_ref[:, 1024:1536]))
    for d in range(2):
        z = _dot(hb, w_ref[:, 1536 + 512 * d:2048 + 512 * d])
        lb = lb_ref[d:d + 1, :]
        e = jnp.exp(-jnp.abs(z))
        r = 1.0 / (1.0 + e)
        er = e * r
        pos = z >= 0.0
        sig = jnp.where(pos, r, er)
        sig_neg = jnp.where(pos, er, r)
        f = lb + (1.0 - lb) * sig
        hg_ref[d] = jnp.log(jnp.maximum(f, TINY))
        hk_ref[d] = (1.0 - lb) * sig_neg
    hv_ref[...] = _dot(hb, w_ref[:, 2560:3072])
    hgate_ref[...] = _silu(_dot(hb, w_ref[:, 3072:3584]))


def _in_proj(x_all, mods_l, n1, w_in_b, cos_t, sa_t, sb_t, qn, kn, lb, gm):
    nblk = T_ALL // ROW_BLK
    row = lambda w: pl.BlockSpec((ROW_BLK, w), lambda i: (i, 0))
    full = lambda a: pl.BlockSpec(a.shape, lambda i: (0,) * a.ndim)
    tab = pl.BlockSpec((ROW_BLK, LANES), lambda i: (jnp.where(i < N_LAT // ROW_BLK, i % (SEQ // ROW_BLK), SEQ // ROW_BLK), 0))
    two = pl.BlockSpec((2, ROW_BLK, 512), lambda i: (0, i, 0))
    sd = lambda w, dt: jax.ShapeDtypeStruct((T_ALL, w), dt)
    return pl.pallas_call(
        _in_kernel,
        grid=(nblk,),
        in_specs=[row(D_MODEL), full(mods_l), full(n1), full(w_in_b), tab, tab, tab, full(qn), full(kn), full(lb), full(gm)],
        out_specs=[row(256), row(128), row(128), row(256), row(128), row(128), row(512), two, two, row(512), row(512)],
        out_shape=[sd(256, BF16), sd(128, BF16), sd(128, BF16), sd(256, BF16), sd(128, BF16), sd(128, BF16),
                   sd(512, F32), jax.ShapeDtypeStruct((2, T_ALL, 512), F32), jax.ShapeDtypeStruct((2, T_ALL, 512), F32),
                   sd(512, F32), sd(512, F32)],
        compiler_params=_cparams(("arbitrary",)),
        name="in_proj",
    )(x_all, mods_l, n1, w_in_b, cos_t, sa_t, sb_t, qn, kn, lb, gm)


def _attn_core(q_ref, parts, sink_ref, o_ref, tq):
    lane = lax.broadcasted_iota(jnp.int32, (tq, LANES), 1)
    outs = []
    for h in range(4):
        kvh = h // 2
        q128 = q_ref[:, kvh * LANES:(kvh + 1) * LANES].astype(F32)
        if (h % 2) != kvh:
            q128 = pltpu.roll(q128, HEAD_DIM, 1)
        qm = jnp.where((lane >= kvh * HEAD_DIM) & (lane < (kvh + 1) * HEAD_DIM), q128, 0.0).astype(BF16)
        scores = []
        for (k, _, mask) in parts:
            s = _dot_nt(qm, k)
            if mask is not None:
                s = jnp.where(mask, s, NEG_INF)
            scores.append(s)
        m = scores[0].max(axis=-1, keepdims=True)
        for s in scores[1:]:
            m = jnp.maximum(m, s.max(axis=-1, keepdims=True))
        if sink_ref is not None:
            sk = sink_ref[:, h:h + 1]
            m = jnp.maximum(m, sk)
            den = jnp.exp(sk - m)
        else:
            den = jnp.zeros_like(m)
        acc = jnp.zeros((tq, LANES), F32)
        for s, (_, v, _) in zip(scores, parts):
            p = jnp.exp(s - m)
            den = den + p.sum(axis=-1, keepdims=True)
            acc = acc + _dot(p.astype(BF16), v)
        outs.append(acc / den)
    for c in range(2):
        a, b = outs[2 * c], outs[2 * c + 1]
        if c == 0:
            b = pltpu.roll(b, HEAD_DIM, 1)
        else:
            a = pltpu.roll(a, HEAD_DIM, 1)
        o_ref[:, c * LANES:(c + 1) * LANES] = jnp.where(lane < HEAD_DIM, a, b).astype(BF16)


def _attn_a_kernel(q_ref, kp_ref, kc_ref, kn_ref, vp_ref, vc_ref, vn_ref, kx_ref, vx_ref, sink_ref, o_ref):
    n = pl.program_id(1)
    qi = lax.broadcasted_iota(jnp.int32, (WINDOW, WINDOW), 0)
    kj = lax.broadcasted_iota(jnp.int32, (WINDOW, WINDOW), 1)
    mask_prev = (kj >= qi) & (n > 0)
    mask_next = (kj <= qi) & (n < SEQ // WINDOW - 1)
    parts = [(kp_ref[...], vp_ref[...], mask_prev), (kc_ref[...], vc_ref[...], None),
             (kn_ref[...], vn_ref[...], mask_next), (kx_ref[...], vx_ref[...], None)]
    _attn_core(q_ref, parts, sink_ref, o_ref, WINDOW)


def _attn_a_latent(qa, ka, va, sink):
    nb = SEQ // WINDOW
    q_spec = pl.BlockSpec((WINDOW, 256), lambda b, n: (b * nb + n, 0))
    prev = pl.BlockSpec((WINDOW, LANES), lambda b, n: (b * nb + jnp.maximum(n - 1, 0), 0))
    cur = pl.BlockSpec((WINDOW, LANES), lambda b, n: (b * nb + n, 0))
    nxt = pl.BlockSpec((WINDOW, LANES), lambda b, n: (b * nb + jnp.minimum(n + 1, nb - 1), 0))
    ctx = pl.BlockSpec((CTX_LEN, LANES), lambda b, n: (N_LAT // CTX_LEN + b, 0))
    return pl.pallas_call(
        _attn_a_kernel,
        grid=(BATCH, nb),
        in_specs=[q_spec, prev, cur, nxt, prev, cur, nxt, ctx, ctx, pl.BlockSpec((1, LANES), lambda b, n: (0, 0))],
        out_specs=pl.BlockSpec((WINDOW, 256), lambda b, n: (b * nb + n, 0)),
        out_shape=jax.ShapeDtypeStruct((N_LAT, 256), BF16),
        compiler_params=_cparams(("arbitrary", "arbitrary")),
        name="attn_window",
    )(qa, ka, ka, ka, va, va, va, ka, va, sink)


B_TQ = 256


def _attn_b_kernel(q_ref, kl_ref, vl_ref, kx_ref, vx_ref, o_ref):
    parts = [(kl_ref[...], vl_ref[...], None), (kx_ref[...], vx_ref[...], None)]
    _attn_core(q_ref, parts, None, o_ref, B_TQ)


def _attn_b_latent(qb, kb, vb):
    nq = SEQ // B_TQ
    lat = pl.BlockSpec((SEQ, LANES), lambda b, j: (b, 0))
    ctx = pl.BlockSpec((CTX_LEN, LANES), lambda b, j: (N_LAT // CTX_LEN + b, 0))
    return pl.pallas_call(
        _attn_b_kernel,
        grid=(BATCH, nq),
        in_specs=[pl.BlockSpec((B_TQ, 256), lambda b, j: (b * nq + j, 0)), lat, lat, ctx, ctx],
        out_specs=pl.BlockSpec((B_TQ, 256), lambda b, j: (b * nq + j, 0)),
        out_shape=jax.ShapeDtypeStruct((N_LAT, 256), BF16),
        compiler_params=_cparams(("arbitrary", "arbitrary")),
        name="attn_full",
    )(qb, kb, vb, kb, vb)


def _attn_ctx_sink_kernel(q_ref, kx_ref, vx_ref, sink_ref, o_ref):
    _attn_core(q_ref, [(kx_ref[...], vx_ref[...], None)], sink_ref, o_ref, CTX_LEN)


def _attn_ctx_kernel(q_ref, kx_ref, vx_ref, o_ref):
    _attn_core(q_ref, [(kx_ref[...], vx_ref[...], None)], None, o_ref, CTX_LEN)


def _attn_ctx(q, k, v, sink):
    blk = lambda w: pl.BlockSpec((CTX_LEN, w), lambda b: (N_LAT // CTX_LEN + b, 0))
    in_specs = [blk(256), blk(LANES), blk(LANES)]
    args = [q, k, v]
    if sink is not None:
        in_specs.append(pl.BlockSpec((1, LANES), lambda b: (0, 0)))
        args.append(sink)
    return pl.pallas_call(
        _attn_ctx_sink_kernel if sink is not None else _attn_ctx_kernel,
        grid=(BATCH,),
        in_specs=in_specs,
        out_specs=pl.BlockSpec((CTX_LEN, 256), lambda b: (b, 0)),
        out_shape=jax.ShapeDtypeStruct((N_CTX, 256), BF16),
        compiler_params=_cparams(("arbitrary",)),
        name="attn_ctx_sink" if sink is not None else "attn_ctx",
    )(*args)


def _scan_constants():
    c = SCAN_CHUNK
    t = np.arange(c)[:, None]
    u = np.arange(c)[None, :]
    mats = [(u <= t), (u > t)]
    qs, ks, masks = [], [], []
    for lvl in range(SCAN_LEVELS):
        m = 2 ** lvl
        second = ((t // m) % 2 == 1)
        end_first = (t // (2 * m)) * 2 * m + m - 1
        qs.append(second & (u > end_first) & (u <= t))
        ks.append((~second) & (u > t) & (u <= end_first))
        s = u
        masks.append(second & ((s // m) % 2 == 0) & ((s // (2 * m)) == (t // (2 * m))))
    fwd = np.concatenate(mats + qs + ks, axis=0).astype(np.float32)
    fmask = np.stack(masks).astype(np.float32)
    bwd = fwd.reshape(16, c, c)[:, ::-1, ::-1].reshape(16 * c, c)
    bmask = fmask[:, ::-1, ::-1]
    return np.stack([fwd, bwd]), np.stack([fmask, bmask])


def _scan_kernel(q_ref, k_ref, g_ref, v_ref, mall_ref, mask_ref, o_ref, st_ref):
    j = pl.program_id(2)

    @pl.when(j == 0)
    def _():
        st_ref[...] = jnp.zeros_like(st_ref)

    c = SCAN_CHUNK
    lf = g_ref[0]
    hi, lo = _split_bf16(lf)
    mall = mall_ref[0]
    ex = jnp.exp(_dot(mall, hi) + _dot(mall, lo))
    tot = jnp.exp(jnp.sum(lf, axis=0, keepdims=True))
    ones = jnp.ones((c, c), BF16)
    for h in range(C_HEADS):
        sl = slice(h * C_KDIM, (h + 1) * C_KDIM)
        q = q_ref[:, sl]
        k = k_ref[0, :, sl]
        v = v_ref[:, sl]
        vb = v.astype(BF16)
        exh = lambda idx: ex[idx * c:(idx + 1) * c, sl]
        st = st_ref[h]
        inter = _dot_nt((q * exh(0)).astype(BF16), st.astype(BF16))
        att = jnp.zeros((c, c), F32)
        for lvl in range(SCAN_LEVELS):
            pr = _dot_nt((q * exh(2 + lvl)).astype(BF16), (k * exh(9 + lvl)).astype(BF16))
            att = att + mask_ref[0, lvl] * pr
        diag = _dot((q * k).astype(BF16), ones)
        o_ref[0, :, sl] = inter + _dot(att.astype(BF16), vb) + diag * v
        khat = (k * exh(1)).astype(BF16)
        st_ref[h] = st * tot[:, sl] + _dot(v.T.astype(BF16), khat)


def _hgrn_scan(hq, hk, hg, hv, mall, masks):
    c = SCAN_CHUNK
    n_ctx_chunks = CTX_LEN // c
    n_lat_chunks = SEQ // c
    steps = n_ctx_chunks + n_lat_chunks

    def rb(b, d, j):
        ctx_blk = N_LAT // c + n_ctx_chunks * b + jnp.where(d == 0, j, n_ctx_chunks - 1 - j)
        jl = j - n_ctx_chunks
        lat_blk = b * n_lat_chunks + jnp.where(d == 0, jl, n_lat_chunks - 1 - jl)
        return jnp.where(j < n_ctx_chunks, ctx_blk, lat_blk)

    row = pl.BlockSpec((c, 512), lambda b, d, j: (rb(b, d, j), 0))
    drow = pl.BlockSpec((1, c, 512), lambda b, d, j: (d, rb(b, d, j), 0))
    return pl.pallas_call(
        _scan_kernel,
        grid=(BATCH, 2, steps),
        in_specs=[row, drow, drow, row,
                  pl.BlockSpec((1, 16 * c, c), lambda b, d, j: (d, 0, 0)),
                  pl.BlockSpec((1, SCAN_LEVELS, c, c), lambda b, d, j: (d, 0, 0, 0))],
        out_specs=drow,
        out_shape=jax.ShapeDtypeStruct((2, T_ALL, 512), F32),
        scratch_shapes=[pltpu.VMEM((C_HEADS, C_VDIM, C_KDIM), F32)],
        compiler_params=_cparams(("arbitrary", "arbitrary", "arbitrary")),
        name="hgrn_scan",
    )(hq, hk, hg, hv, mall, masks)


def _out_kernel(ya_ref, yb_ref, of_ref, ob_ref, gate_ref, x_ref, mod_ref, gw_ref, w_ref, n2_ref,
                rwh_ref, rwl_ref, rb_ref, xo_ref, h2_ref, ti_ref, tg_ref):
    i = pl.program_id(0)
    midx = jnp.minimum(i // (SEQ // ROW_BLK), BATCH)
    mod = mod_ref[pl.ds(midx, 1), :]
    g1 = mod[:, 2 * D_MODEL:3 * D_MODEL]
    sh2 = mod[:, 3 * D_MODEL:4 * D_MODEL]
    sc2 = mod[:, 4 * D_MODEL:5 * D_MODEL]

    o = of_ref[0] + ob_ref[0]
    y = _dot(ya_ref[...], w_ref[0:256, :]) + _dot(yb_ref[...], w_ref[256:512, :])
    for h in range(C_HEADS):
        sl = slice(h * C_VDIM, (h + 1) * C_VDIM)
        oh = o[:, sl]
        yn = oh * lax.rsqrt(jnp.mean(oh * oh, axis=-1, keepdims=True) + EPS) * gw_ref[...]
        yc = (yn * gate_ref[:, sl]).astype(BF16)
        y = y + _dot(yc, w_ref[512 + h * C_VDIM:512 + (h + 1) * C_VDIM, :])
    x = x_ref[...] + g1 * y
    xo_ref[...] = x
    h2 = x * lax.rsqrt(jnp.mean(x * x, axis=-1, keepdims=True) + EPS) * n2_ref[...]
    h2 = h2 * (1.0 + sc2) + sh2
    h2_ref[...] = h2

    hi, lo = _split_bf16(h2)
    rwh = rwh_ref[...]
    logits = _dot(hi, rwh) + _dot(hi, rwl_ref[...]) + _dot(lo, rwh) + rb_ref[...]
    lane = lax.broadcasted_iota(jnp.int32, (ROW_BLK, LANES), 1).astype(F32)
    ti = jnp.zeros((ROW_BLK, LANES), F32)
    ex = jnp.zeros((ROW_BLK, LANES), F32)
    den = jnp.zeros((ROW_BLK, 1), F32)
    top = None
    for k in range(TOP_K):
        m = logits.max(axis=-1, keepdims=True)
        idx = jnp.min(jnp.where(logits == m, lane, float(LANES)), axis=-1, keepdims=True)
        if top is None:
            top = m
        e = jnp.exp(m - top)
        den = den + e
        ti = jnp.where(lane == float(k), idx, ti)
        ex = jnp.where(lane == float(k), e, ex)
        logits = jnp.where(lane == idx, -3.0e38, logits)
    ti_ref[...] = ti.astype(jnp.int32)
    tg_ref[...] = ex / den


def _out_proj(n_rows, ya, yb, o, gate, x_all, mods_l, gw, w_out_b, n2, rwh, rwl, rb):
    nblk = n_rows // ROW_BLK
    row = lambda w: pl.BlockSpec((ROW_BLK, w), lambda i: (i, 0))
    full = lambda a: pl.BlockSpec(a.shape, lambda i: (0,) * a.ndim)
    sd = lambda w, dt: jax.ShapeDtypeStruct((n_rows, w), dt)
    return pl.pallas_call(
        _out_kernel,
        grid=(nblk,),
        in_specs=[row(256), row(256),
                  pl.BlockSpec((1, ROW_BLK, 512), lambda i: (0, i, 0)),
                  pl.BlockSpec((1, ROW_BLK, 512), lambda i: (1, i, 0)),
                  row(512), row(D_MODEL), full(mods_l), full(gw), full(w_out_b), full(n2),
                  full(rwh), full(rwl), full(rb)],
        out_specs=[row(D_MODEL), row(D_MODEL), row(LANES), row(LANES)],
        out_shape=[sd(D_MODEL, F32), sd(D_MODEL, F32), sd(LANES, jnp.int32), sd(LANES, F32)],
        compiler_params=_cparams(("arbitrary",)),
        name="out_proj_router",
    )(ya, yb, o, o, gate, x_all, mods_l, gw, w_out_b, n2, rwh, rwl, rb)


def _rank_kernel(ti_ref, tri_ref, rank_ref, cnt_ref, carry_ref):
    i = pl.program_id(0)

    @pl.when(i == 0)
    def _():
        carry_ref[...] = jnp.zeros_like(carry_ref)

    ti = ti_ref[...]
    lane = lax.broadcasted_iota(jnp.int32, (ROW_BLK, LANES), 1)
    onehots = [(lane == ti[:, k:k + 1]).astype(F32) for k in range(TOP_K)]
    tot = onehots[0] + onehots[1] + onehots[2] + onehots[3]
    before = _dot(tri_ref[...], tot.astype(BF16)) + carry_ref[0:1, :]
    r = jnp.zeros((ROW_BLK, LANES), F32)
    for k in range(TOP_K):
        rk = jnp.sum(onehots[k] * before, axis=-1, keepdims=True)
        r = jnp.where(lane == k, rk, r)
    rank_ref[...] = r.astype(jnp.int32)
    carry = carry_ref[0:1, :] + jnp.sum(tot, axis=0, keepdims=True)
    carry_ref[0:1, :] = carry
    cnt_ref[...] = jnp.broadcast_to(carry, (8, LANES))


def _expert_ranks(top_i, tri):
    n_rows = top_i.shape[0]
    return pl.pallas_call(
        _rank_kernel,
        grid=(n_rows // ROW_BLK,),
        in_specs=[pl.BlockSpec((ROW_BLK, LANES), lambda i: (i, 0)),
                  pl.BlockSpec((ROW_BLK, ROW_BLK), lambda i: (0, 0))],
        out_specs=[pl.BlockSpec((ROW_BLK, LANES), lambda i: (i, 0)),
                   pl.BlockSpec((8, LANES), lambda i: (0, 0))],
        out_shape=[jax.ShapeDtypeStruct((n_rows, LANES), jnp.int32), jax.ShapeDtypeStruct((8, LANES), F32)],
        scratch_shapes=[pltpu.VMEM((8, LANES), F32)],
        compiler_params=_cparams(("arbitrary",)),
        name="expert_ranks",
    )(top_i, tri)


GATHER_ROWS = 256


def _row_copy(src_ref, out_ref, sem, src_row, dst_row):
    return pltpu.make_async_copy(src_ref.at[pl.ds(src_row, 1), :], out_ref.at[pl.ds(dst_row, 1), :], sem)


def _gather_kernel(idx_ref, src_ref, out_ref, sem):
    def start(r, carry):
        _row_copy(src_ref, out_ref, sem, idx_ref[0, 0, r], r).start()
        return carry

    lax.fori_loop(0, GATHER_ROWS, start, 0, unroll=8)

    def wait(r, carry):
        _row_copy(src_ref, out_ref, sem, 0, r).wait()
        return carry

    lax.fori_loop(0, GATHER_ROWS, wait, 0, unroll=8)


def _gather_rows(src, idx):
    n = idx.shape[0]
    nblk = n // GATHER_ROWS
    width = src.shape[1]
    return pl.pallas_call(
        _gather_kernel,
        grid=(nblk,),
        in_specs=[pl.BlockSpec((1, 1, GATHER_ROWS), lambda i: (i, 0, 0), memory_space=pltpu.SMEM),
                  pl.BlockSpec(memory_space=pl.ANY)],
        out_specs=pl.BlockSpec((GATHER_ROWS, width), lambda i: (i, 0)),
        out_shape=jax.ShapeDtypeStruct((n, width), src.dtype),
        scratch_shapes=[pltpu.SemaphoreType.DMA(())],
        compiler_params=_cparams(("arbitrary",)),
        name="gather_rows",
    )(idx.reshape(nblk, 1, GATHER_ROWS), src)


DEINT = 256


def _mlp_kernel(blk_e_ref, nused_ref, x_ref, w1_ref, b1_ref, w2_ref, b2_ref, perm_ref, y_ref, w1p_ref, w2p_ref):
    i = pl.program_id(0)
    new_expert = (i == 0) | (blk_e_ref[i] != blk_e_ref[jnp.maximum(i - 1, 0)])

    @pl.when(new_expert & (i < nused_ref[0]))
    def _():
        perm = perm_ref[...]
        half = DEINT // 2
        for j in range(2 * D_FF // DEINT):
            r = _dot(w1_ref[0, 0, :, j * DEINT:(j + 1) * DEINT].astype(BF16), perm)
            w1p_ref[:, j * half:(j + 1) * half] = r[:, 0:half].astype(BF16)
            w1p_ref[:, D_FF + j * half:D_FF + (j + 1) * half] = r[:, half:DEINT].astype(BF16)
        w2p_ref[...] = w2_ref[0, 0].astype(BF16)

    @pl.when(i < nused_ref[0])
    def _():
        u = _dot(x_ref[...].astype(BF16), w1p_ref[...]) + b1_ref[0, 0]
        u_glu = jnp.minimum(u[:, 0:D_FF], SWIGLU_LIMIT)
        u_lin = jnp.clip(u[:, D_FF:2 * D_FF], -SWIGLU_LIMIT, SWIGLU_LIMIT)
        act = u_glu * jax.nn.sigmoid(SWIGLU_ALPHA * u_glu) * (u_lin + 1.0)
        y_ref[...] = _dot(act.astype(BF16), w2p_ref[...]) + b2_ref[0, 0]

    @pl.when(i >= nused_ref[0])
    def _():
        y_ref[...] = jnp.zeros_like(y_ref)


def _expert_mlp(layer, x_sorted, blk_e, nused, w1, b1p, w2, b2, perm):
    n_blocks = x_sorted.shape[0] // MOE_BLOCK
    grid_spec = pltpu.PrefetchScalarGridSpec(
        num_scalar_prefetch=2,
        grid=(n_blocks,),
        in_specs=[pl.BlockSpec((MOE_BLOCK, D_MODEL), lambda i, be, nu: (i, 0)),
                  pl.BlockSpec((1, 1, D_MODEL, 2 * D_FF), lambda i, be, nu: (layer, be[i], 0, 0)),
                  pl.BlockSpec((1, 1, 1, 2 * D_FF), lambda i, be, nu: (layer, be[i], 0, 0)),
                  pl.BlockSpec((1, 1, D_FF, D_MODEL), lambda i, be, nu: (layer, be[i], 0, 0)),
                  pl.BlockSpec((1, 1, 1, D_MODEL), lambda i, be, nu: (layer, be[i], 0, 0)),
                  pl.BlockSpec((DEINT, DEINT), lambda i, be, nu: (0, 0))],
        out_specs=pl.BlockSpec((MOE_BLOCK, D_MODEL), lambda i, be, nu: (i, 0)),
        scratch_shapes=[pltpu.VMEM((D_MODEL, 2 * D_FF), BF16), pltpu.VMEM((D_FF, D_MODEL), BF16)],
    )
    return pl.pallas_call(
        _mlp_kernel,
        grid_spec=grid_spec,
        out_shape=jax.ShapeDtypeStruct(x_sorted.shape, F32),
        compiler_params=_cparams(("arbitrary",)),
        name="expert_mlp",
    )(blk_e, nused, x_sorted, w1, b1p, w2, b2, perm)


def _combine_kernel(y0_ref, y1_ref, y2_ref, y3_ref, tg_ref, x_ref, mod_ref, nf_ref, o_ref, *, final):
    i = pl.program_id(0)
    midx = jnp.minimum(i // (SEQ // ROW_BLK), BATCH)
    g2 = mod_ref[pl.ds(midx, 1), 5 * D_MODEL:6 * D_MODEL]
    tg = tg_ref[...]
    f = jnp.zeros((ROW_BLK, D_MODEL), F32)
    for k, y_ref in enumerate((y0_ref, y1_ref, y2_ref, y3_ref)):
        f = f + y_ref[0] * tg[:, k:k + 1]
    x = x_ref[...] + g2 * f
    if final:
        x = x * lax.rsqrt(jnp.mean(x * x, axis=-1, keepdims=True) + EPS) * nf_ref[...]
    o_ref[...] = x


def _combine(y_g, tg, x_mid, mods_l, nf, final):
    n_rows = x_mid.shape[0]
    full = lambda a: pl.BlockSpec(a.shape, lambda i: (0,) * a.ndim)
    yk = lambda k: pl.BlockSpec((1, ROW_BLK, D_MODEL), lambda i: (k, i, 0))
    return pl.pallas_call(
        functools.partial(_combine_kernel, final=final),
        grid=(n_rows // ROW_BLK,),
        in_specs=[yk(0), yk(1), yk(2), yk(3),
                  pl.BlockSpec((ROW_BLK, LANES), lambda i: (i, 0)),
                  pl.BlockSpec((ROW_BLK, D_MODEL), lambda i: (i, 0)), full(mods_l), full(nf)],
        out_specs=pl.BlockSpec((ROW_BLK, D_MODEL), lambda i: (i, 0)),
        out_shape=jax.ShapeDtypeStruct((n_rows, D_MODEL), F32),
        compiler_params=_cparams(("arbitrary",)),
        name="moe_combine_final" if final else "moe_combine",
    )(y_g, y_g, y_g, y_g, tg, x_mid, mods_l, nf)


def _moe(layer, h2, top_i, top_g, x_mid, mods_l, w1, b1p, w2, b2, nf, tri, perm, final):
    n_rows = h2.shape[0]
    tk = n_rows * TOP_K
    n_blocks = -(-(tk + N_EXPERTS * (MOE_BLOCK - 1)) // MOE_BLOCK)
    p_rows = n_blocks * MOE_BLOCK

    rank, cnt = _expert_ranks(top_i, tri)
    counts = cnt[0, :N_EXPERTS].astype(jnp.int32)
    padded = (counts + MOE_BLOCK - 1) // MOE_BLOCK * MOE_BLOCK
    pend = jnp.cumsum(padded)
    pstart = pend - padded
    ti4 = top_i[:, :TOP_K]
    dest = pstart[ti4] + rank[:, :TOP_K]
    tok = jnp.broadcast_to(jnp.arange(n_rows, dtype=jnp.int32)[:, None], (n_rows, TOP_K))
    buf_tok = jnp.zeros((p_rows,), jnp.int32).at[dest.reshape(-1)].set(tok.reshape(-1))
    blk_start = jnp.arange(n_blocks, dtype=jnp.int32) * MOE_BLOCK
    blk_e = jnp.minimum(jnp.sum((pend[None, :] <= blk_start[:, None]).astype(jnp.int32), axis=1), N_EXPERTS - 1)
    nused = (pend[-1:] // MOE_BLOCK).astype(jnp.int32)

    x_sorted = _gather_rows(h2, buf_tok)
    y_sorted = _expert_mlp(layer, x_sorted, blk_e, nused, w1, b1p, w2, b2, perm)
    y_g = _gather_rows(y_sorted, dest.T.reshape(-1)).reshape(TOP_K, n_rows, D_MODEL)
    return _combine(y_g, top_g, x_mid, mods_l, nf, final)


def _rope_tables():
    rows = SEQ // GRID_W
    row = jnp.repeat(jnp.arange(rows), GRID_W).astype(F32)
    col = jnp.tile(jnp.arange(GRID_W), rows).astype(F32)
    half = HEAD_DIM // 2
    inv = 1.0 / (ROPE_THETA ** (jnp.arange(0, half, 2, dtype=F32) / half))
    ang_r = row[:, None] * inv
    ang_c = col[:, None] * inv
    ang = jnp.concatenate([ang_r, ang_r, ang_c, ang_c], axis=-1)
    cos, sin = jnp.cos(ang), jnp.sin(ang)
    first = (jnp.arange(HEAD_DIM) % 32) < 16
    sa = jnp.where(first, -sin, 0.0)
    sb = jnp.where(first, 0.0, sin)
    ext = lambda t, fill: jnp.concatenate(
        [jnp.tile(t, (1, 2)), jnp.full((ROW_BLK, LANES), fill, F32)], axis=0)
    return ext(cos, 1.0), ext(sa, 0.0), ext(sb, 0.0)


def kernel(x, c, ctx, c_ctx, w_mod, b_mod, norm1_w, norm2_w, w_in, w_out, attn_sink, q_norm_w, k_norm_w,
           hgrn_lb, gate_norm_w, router_w, router_b, w1, b1, w2, b2, final_norm_w):
    cos_t, sa_t, sb_t = _rope_tables()
    lbs = jax.nn.softmax(hgrn_lb.astype(F32), axis=0)
    lbs = jnp.cumsum(lbs, axis=0) - lbs[0]
    mall_np, masks_np = _scan_constants()
    mall = jnp.asarray(mall_np, BF16)
    masks = jnp.asarray(masks_np, F32)
    gm = jnp.asarray(np.kron(np.eye(2), np.ones((HEAD_DIM, HEAD_DIM))), BF16)
    tri = jnp.asarray(np.tril(np.ones((ROW_BLK, ROW_BLK)), -1), BF16)

    c_all = jnp.concatenate([c, c_ctx[None, :], jnp.zeros((3, D_MODEL), F32)], axis=0)
    mods = _modulation(c_all, w_mod, b_mod)
    x_all = jnp.concatenate([x.reshape(N_LAT, D_MODEL), ctx.reshape(N_CTX, D_MODEL)], axis=0)
    nf = final_norm_w.reshape(1, D_MODEL)
    b1p = jnp.concatenate([b1[..., 0::2], b1[..., 1::2]], axis=-1).reshape(DEPTH, N_EXPERTS, 1, 2 * D_FF)
    b2r = b2.reshape(DEPTH, N_EXPERTS, 1, D_MODEL)
    src = np.concatenate([2 * np.arange(DEINT // 2), 2 * np.arange(DEINT // 2) + 1])
    perm = jnp.asarray(np.arange(DEINT)[:, None] == src[None, :], BF16)

    out = None
    for l in range(DEPTH):
        last = l == DEPTH - 1
        mods_l = mods[l]
        qn = jnp.tile(q_norm_w[l], 2).reshape(1, LANES)
        kn = jnp.tile(k_norm_w[l], 2).reshape(1, LANES)
        sink = jnp.zeros((1, LANES), F32).at[0, :A_HEADS].set(attn_sink[l])
        (qa, ka, va, qb, kb, vb, hq, hk, hg, hv, hgate) = _in_proj(
            x_all, mods_l, norm1_w[l].reshape(1, D_MODEL), w_in[l].astype(BF16), cos_t, sa_t, sb_t, qn, kn, lbs[l], gm)

        ya = _attn_a_latent(qa, ka, va, sink)
        yb = _attn_b_latent(qb, kb, vb)
        o = _hgrn_scan(hq, hk, hg, hv, mall, masks)
        if not last:
            ya = jnp.concatenate([ya, _attn_ctx(qa, ka, va, sink)], axis=0)
            yb = jnp.concatenate([yb, _attn_ctx(qb, kb, vb, None)], axis=0)
        n_rows = N_LAT if last else T_ALL

        rw = jnp.zeros((D_MODEL, LANES), F32).at[:, :N_EXPERTS].set(router_w[l])
        rwh, rwl = _split_bf16(rw)
        rb = jnp.full((1, LANES), NEG_INF, F32).at[0, :N_EXPERTS].set(router_b[l])
        x_mid, h2, top_i, top_g = _out_proj(
            n_rows, ya, yb, o, hgate, x_all, mods_l, jnp.tile(gate_norm_w[l], 1).reshape(1, C_VDIM),
            w_out[l].astype(BF16), norm2_w[l].reshape(1, D_MODEL), rwh, rwl, rb)

        x_all = _moe(l, h2, top_i, top_g, x_mid, mods_l, w1, b1p, w2, b2r, nf, tri, perm, last)
        out = x_all
    return out.reshape(BATCH, SEQ, D_MODEL)
```

```python
import functools

import numpy as np
import jax
import jax.numpy as jnp
from jax import lax
from jax.experimental import pallas as pl
from jax.experimental.pallas import tpu as pltpu

F32 = jnp.float32
BF16 = jnp.bfloat16

D_MODEL = 1024
BATCH = 4
SEQ = 4096
DEPTH = 2
GRID_W = 64
CTX_LEN = 256
HEAD_DIM = 64
ATTN_SCALE = HEAD_DIM ** -0.5
ROPE_THETA = 10000.0
A_HEADS = 4
B_HEADS = 4
WINDOW = 128
C_HEADS = 4
C_KDIM = 128
C_VDIM = 128
IN_WIDTH = 3584
MIX_WIDTH = 1024
N_EXPERTS = 32
TOP_K = 4
D_FF = 1024
SWIGLU_ALPHA = 1.702
SWIGLU_LIMIT = 7.0
MOE_BLOCK = 256
EPS = 1e-6
NEG_INF = -1e30
TINY = 1e-30

N_LAT = BATCH * SEQ
N_CTX = BATCH * CTX_LEN
T_ALL = N_LAT + N_CTX
ROW_BLK = 256
LANES = 128
SCAN_CHUNK = 128
SCAN_LEVELS = 7
SCAN_MATS = SCAN_LEVELS + 1
VMEM_LIMIT = 56 * 1024 * 1024


def _cparams(sem):
    return pltpu.CompilerParams(dimension_semantics=sem, vmem_limit_bytes=VMEM_LIMIT)


def _split_bf16(a):
    hi = a.astype(BF16)
    lo = (a - hi.astype(F32)).astype(BF16)
    return hi, lo


def _dot(a, b):
    return jnp.dot(a, b, preferred_element_type=F32)


def _dot_nt(a, b):
    return lax.dot_general(a, b, (((1,), (1,)), ((), ())), preferred_element_type=F32)


def _silu(a):
    return a * jax.nn.sigmoid(a)


TILE_ROWS = D_MODEL // LANES


def _store_token_tiles(ref, val):
    n = val.shape[0]
    for c in range(TILE_ROWS):
        ref[pl.ds(c, n, stride=TILE_ROWS), :] = val[:, c * LANES:(c + 1) * LANES]


def _load_token_tiles(ref, n, lead=None):
    cols = []
    for c in range(TILE_ROWS):
        if lead is None:
            cols.append(ref[pl.ds(c, n, stride=TILE_ROWS), :])
        else:
            cols.append(ref[lead, pl.ds(c, n, stride=TILE_ROWS), :])
    return cols


def _mod_kernel(c_ref, w_ref, b_ref, o_ref):
    a = _silu(c_ref[...]).astype(BF16)
    o_ref[0] = _dot(a, w_ref[0].astype(BF16)) + b_ref[0]


def _modulation(c_all, w_mod, b_mod):
    tn = 1536
    return pl.pallas_call(
        _mod_kernel,
        grid=(DEPTH, 6 * D_MODEL // tn),
        in_specs=[pl.BlockSpec((8, D_MODEL), lambda l, j: (0, 0)),
                  pl.BlockSpec((1, D_MODEL, tn), lambda l, j: (l, 0, j)),
                  pl.BlockSpec((1, 1, tn), lambda l, j: (l, 0, j))],
        out_specs=pl.BlockSpec((1, 8, tn), lambda l, j: (l, 0, j)),
        out_shape=jax.ShapeDtypeStruct((DEPTH, 8, 6 * D_MODEL), F32),
        compiler_params=_cparams(("arbitrary", "arbitrary")),
        name="modulation",
    )(c_all, w_mod, b_mod.reshape(DEPTH, 1, 6 * D_MODEL))


def _in_kernel(x_ref, mod_ref, n1_ref, w_ref, cos_ref, sa_ref, sb_ref, qn_ref, kn_ref, lb_ref, gm_ref,
               qa_ref, ka_ref, va_ref, qb_ref, kb_ref, vb_ref, hq_ref, hk_ref, hg_ref, hv_ref, hgate_ref):
    i = pl.program_id(0)
    midx = jnp.minimum(i // (SEQ // ROW_BLK), BATCH)
    mod = mod_ref[pl.ds(midx, 1), :]
    sh1 = mod[:, 0:D_MODEL]
    sc1 = mod[:, D_MODEL:2 * D_MODEL]
    x = x_ref[...]
    h = x * lax.rsqrt(jnp.mean(x * x, axis=-1, keepdims=True) + EPS) * n1_ref[...]
    hb = (h * (1.0 + sc1) + sh1).astype(BF16)

    cos = cos_ref[...]
    sa = sa_ref[...]
    sb = sb_ref[...]
    gm = gm_ref[...]

    def rope(t):
        return t * cos + pltpu.roll(t, LANES - 16, 1) * sa + pltpu.roll(t, 16, 1) * sb

    def headnorm(t, w):
        hi, lo = _split_bf16(t * t)
        ss = _dot(hi, gm) + _dot(lo, gm)
        return t * lax.rsqrt(ss * (1.0 / HEAD_DIM) + EPS) * w

    pa = _dot(hb, w_ref[:, 0:1024])
    for c in range(2):
        qa_ref[:, c * LANES:(c + 1) * LANES] = (rope(pa[:, c * LANES:(c + 1) * LANES]) * ATTN_SCALE).astype(BF16)
    ka_ref[...] = rope(pa[:, 256:384]).astype(BF16)
    va_ref[...] = pa[:, 384:512].astype(BF16)
    qn = qn_ref[...]
    kn = kn_ref[...]
    for c in range(2):
        t = headnorm(pa[:, 512 + c * LANES:512 + (c + 1) * LANES], qn)
        qb_ref[:, c * LANES:(c + 1) * LANES] = (rope(t) * ATTN_SCALE).astype(BF16)
    kb_ref[...] = rope(headnorm(pa[:, 768:896], kn)).astype(BF16)
    vb_ref[...] = pa[:, 896:1024].astype(BF16)

    hq_ref[...] = _silu(_dot(hb, w_ref[:, 1024:1536]))
    for d in range(2):
        z = _dot(hb, w_ref[:, 1536 + 512 * d:2048 + 512 * d])
        lb = lb_ref[d:d + 1, :]
        e = jnp.exp(-jnp.abs(z))
        r = 1.0 / (1.0 + e)
        er = e * r
        pos = z >= 0.0
        sig = jnp.where(pos, r, er)
        sig_neg = jnp.where(pos, er, r)
        f = lb + (1.0 - lb) * sig
        hg_ref[d] = jnp.log(jnp.maximum(f, TINY))
        hk_ref[d] = (1.0 - lb) * sig_neg
    hv_ref[...] = _dot(hb, w_ref[:, 2560:3072])
    hgate_ref[...] = _silu(_dot(hb, w_ref[:, 3072:3584]))


def _in_proj(x_all, mods_l, n1, w_in_b, cos_t, sa_t, sb_t, qn, kn, lb, gm):
    nblk = T_ALL // ROW_BLK
    row = lambda w: pl.BlockSpec((ROW_BLK, w), lambda i: (i, 0))
    full = lambda a: pl.BlockSpec(a.shape, lambda i: (0,) * a.ndim)
    tab = pl.BlockSpec((ROW_BLK, LANES), lambda i: (jnp.where(i < N_LAT // ROW_BLK, i % (SEQ // ROW_BLK), SEQ // ROW_BLK), 0))
    two = pl.BlockSpec((2, ROW_BLK, 512), lambda i: (0, i, 0))
    sd = lambda w, dt: jax.ShapeDtypeStruct((T_ALL, w), dt)
    return pl.pallas_call(
        _in_kernel,
        grid=(nblk,),
        in_specs=[row(D_MODEL), full(mods_l), full(n1), full(w_in_b), tab, tab, tab, full(qn), full(kn), full(lb), full(gm)],
        out_specs=[row(256), row(128), row(128), row(256), row(128), row(128), row(512), two, two, row(512), row(512)],
        out_shape=[sd(256, BF16), sd(128, BF16), sd(128, BF16), sd(256, BF16), sd(128, BF16), sd(128, BF16),
                   sd(512, F32), jax.ShapeDtypeStruct((2, T_ALL, 512), F32), jax.ShapeDtypeStruct((2, T_ALL, 512), F32),
                   sd(512, F32), sd(512, F32)],
        compiler_params=_cparams(("arbitrary",)),
        name="in_proj",
    )(x_all, mods_l, n1, w_in_b, cos_t, sa_t, sb_t, qn, kn, lb, gm)


def _attn_core(q_ref, parts, sink_ref, o_ref, tq):
    lane = lax.broadcasted_iota(jnp.int32, (tq, LANES), 1)
    outs = []
    for h in range(4):
        kvh = h // 2
        q128 = q_ref[:, kvh * LANES:(kvh + 1) * LANES].astype(F32)
        if (h % 2) != kvh:
            q128 = pltpu.roll(q128, HEAD_DIM, 1)
        qm = jnp.where((lane >= kvh * HEAD_DIM) & (lane < (kvh + 1) * HEAD_DIM), q128, 0.0).astype(BF16)
        scores = []
        for (k, _, mask) in parts:
            s = _dot_nt(qm, k)
            if mask is not None:
                s = jnp.where(mask, s, NEG_INF)
            scores.append(s)
        m = scores[0].max(axis=-1, keepdims=True)
        for s in scores[1:]:
            m = jnp.maximum(m, s.max(axis=-1, keepdims=True))
        if sink_ref is not None:
            sk = sink_ref[:, h:h + 1]
            m = jnp.maximum(m, sk)
            den = jnp.exp(sk - m)
        else:
            den = jnp.zeros_like(m)
        acc = jnp.zeros((tq, LANES), F32)
        for s, (_, v, _) in zip(scores, parts):
            p = jnp.exp(s - m)
            den = den + p.sum(axis=-1, keepdims=True)
            acc = acc + _dot(p.astype(BF16), v)
        outs.append(acc / den)
    for c in range(2):
        a, b = outs[2 * c], outs[2 * c + 1]
        if c == 0:
            b = pltpu.roll(b, HEAD_DIM, 1)
        else:
            a = pltpu.roll(a, HEAD_DIM, 1)
        o_ref[:, c * LANES:(c + 1) * LANES] = jnp.where(lane < HEAD_DIM, a, b).astype(BF16)


A_TQ = 512


def _attn_a_kernel(q_ref, kp_ref, kc_ref, kn_ref, vp_ref, vc_ref, vn_ref, kx_ref, vx_ref, sink_ref, o_ref):
    n = pl.program_id(1)
    qi = lax.broadcasted_iota(jnp.int32, (A_TQ, WINDOW), 0)
    kj = lax.broadcasted_iota(jnp.int32, (A_TQ, WINDOW), 1)
    mask_prev = (kj >= qi) & (n > 0)
    mask_next = (kj <= qi - (A_TQ - WINDOW)) & (n < SEQ // A_TQ - 1)
    qc = lax.broadcasted_iota(jnp.int32, (A_TQ, A_TQ), 0)
    kc = lax.broadcasted_iota(jnp.int32, (A_TQ, A_TQ), 1)
    mask_cur = jnp.abs(qc - kc) <= WINDOW
    parts = [(kp_ref[...], vp_ref[...], mask_prev), (kc_ref[...], vc_ref[...], mask_cur),
             (kn_ref[...], vn_ref[...], mask_next), (kx_ref[...], vx_ref[...], None)]
    _attn_core(q_ref, parts, sink_ref, o_ref, A_TQ)


def _attn_a_latent(qa, ka, va, sink):
    nq = SEQ // A_TQ
    nb = SEQ // WINDOW
    per = A_TQ // WINDOW
    q_spec = pl.BlockSpec((A_TQ, 256), lambda b, n: (b * nq + n, 0))
    prev = pl.BlockSpec((WINDOW, LANES), lambda b, n: (b * nb + jnp.maximum(n * per - 1, 0), 0))
    cur = pl.BlockSpec((A_TQ, LANES), lambda b, n: (b * nq + n, 0))
    nxt = pl.BlockSpec((WINDOW, LANES), lambda b, n: (b * nb + jnp.minimum((n + 1) * per, nb - 1), 0))
    ctx = pl.BlockSpec((CTX_LEN, LANES), lambda b, n: (N_LAT // CTX_LEN + b, 0))
    return pl.pallas_call(
        _attn_a_kernel,
        grid=(BATCH, nq),
        in_specs=[q_spec, prev, cur, nxt, prev, cur, nxt, ctx, ctx, pl.BlockSpec((1, LANES), lambda b, n: (0, 0))],
        out_specs=pl.BlockSpec((A_TQ, 256), lambda b, n: (b * nq + n, 0)),
        out_shape=jax.ShapeDtypeStruct((N_LAT, 256), BF16),
        compiler_params=_cparams(("arbitrary", "arbitrary")),
        name="attn_window",
    )(qa, ka, ka, ka, va, va, va, ka, va, sink)


B_TQ = 256


def _attn_b_kernel(q_ref, kl_ref, vl_ref, kx_ref, vx_ref, o_ref):
    parts = [(kl_ref[...], vl_ref[...], None), (kx_ref[...], vx_ref[...], None)]
    _attn_core(q_ref, parts, None, o_ref, B_TQ)


def _attn_b_latent(qb, kb, vb):
    nq = SEQ // B_TQ
    lat = pl.BlockSpec((SEQ, LANES), lambda b, j: (b, 0))
    ctx = pl.BlockSpec((CTX_LEN, LANES), lambda b, j: (N_LAT // CTX_LEN + b, 0))
    return pl.pallas_call(
        _attn_b_kernel,
        grid=(BATCH, nq),
        in_specs=[pl.BlockSpec((B_TQ, 256), lambda b, j: (b * nq + j, 0)), lat, lat, ctx, ctx],
        out_specs=pl.BlockSpec((B_TQ, 256), lambda b, j: (b * nq + j, 0)),
        out_shape=jax.ShapeDtypeStruct((N_LAT, 256), BF16),
        compiler_params=_cparams(("arbitrary", "arbitrary")),
        name="attn_full",
    )(qb, kb, vb, kb, vb)


def _attn_ctx_sink_kernel(q_ref, kx_ref, vx_ref, sink_ref, o_ref):
    _attn_core(q_ref, [(kx_ref[...], vx_ref[...], None)], sink_ref, o_ref, CTX_LEN)


def _attn_ctx_kernel(q_ref, kx_ref, vx_ref, o_ref):
    _attn_core(q_ref, [(kx_ref[...], vx_ref[...], None)], None, o_ref, CTX_LEN)


def _attn_ctx(q, k, v, sink):
    blk = lambda w: pl.BlockSpec((CTX_LEN, w), lambda b: (N_LAT // CTX_LEN + b, 0))
    in_specs = [blk(256), blk(LANES), blk(LANES)]
    args = [q, k, v]
    if sink is not None:
        in_specs.append(pl.BlockSpec((1, LANES), lambda b: (0, 0)))
        args.append(sink)
    return pl.pallas_call(
        _attn_ctx_sink_kernel if sink is not None else _attn_ctx_kernel,
        grid=(BATCH,),
        in_specs=in_specs,
        out_specs=pl.BlockSpec((CTX_LEN, 256), lambda b: (b, 0)),
        out_shape=jax.ShapeDtypeStruct((N_CTX, 256), BF16),
        compiler_params=_cparams(("arbitrary",)),
        name="attn_ctx_sink" if sink is not None else "attn_ctx",
    )(*args)


def _scan_constants():
    c = SCAN_CHUNK
    t = np.arange(c)[:, None]
    u = np.arange(c)[None, :]
    mats = [(u <= t)]
    masks = []
    for lvl in range(SCAN_LEVELS):
        m = 2 ** lvl
        second = ((t // m) % 2 == 1)
        end_first = (t // (2 * m)) * 2 * m + m - 1
        mats.append((second & (u > end_first) & (u <= t)) | ((~second) & (u > t) & (u <= end_first)))
        s = u
        masks.append(second & ((s // m) % 2 == 0) & ((s // (2 * m)) == (t // (2 * m))))
    fwd = np.concatenate(mats, axis=0).astype(np.float32)
    fmask = np.stack(masks).astype(np.float32)
    bwd = fwd.reshape(SCAN_MATS, c, c)[:, ::-1, ::-1].reshape(SCAN_MATS * c, c)
    bmask = fmask[:, ::-1, ::-1]
    return np.stack([fwd, bwd]), np.stack([fmask, bmask])


def _scan_kernel(q_ref, k_ref, g_ref, v_ref, mall_ref, mask_ref, o_ref, st_ref):
    j = pl.program_id(2)

    @pl.when(j == 0)
    def _():
        st_ref[...] = jnp.zeros_like(st_ref)

    c = SCAN_CHUNK
    lf = g_ref[0]
    hi, lo = _split_bf16(lf)
    mall = mall_ref[0]
    sums = _dot(mall, hi) + _dot(mall, lo)
    ex = jnp.exp(sums)
    total = jnp.sum(lf, axis=0, keepdims=True)
    ex_after = jnp.exp(total - sums[0:c])
    tot = jnp.exp(total)
    ones = jnp.ones((c, c), BF16)
    for h in range(C_HEADS):
        sl = slice(h * C_KDIM, (h + 1) * C_KDIM)
        q = q_ref[:, sl]
        k = k_ref[0, :, sl]
        v = v_ref[:, sl]
        vb = v.astype(BF16)
        exh = lambda idx: ex[idx * c:(idx + 1) * c, sl]
        st = st_ref[h]
        inter = _dot_nt((q * exh(0)).astype(BF16), st.astype(BF16))
        att = jnp.zeros((c, c), F32)
        for lvl in range(SCAN_LEVELS):
            e = exh(1 + lvl)
            pr = _dot_nt((q * e).astype(BF16), (k * e).astype(BF16))
            att = att + mask_ref[0, lvl] * pr
        diag = _dot((q * k).astype(BF16), ones)
        o_ref[0, :, sl] = inter + _dot(att.astype(BF16), vb) + diag * v
        khat = (k * ex_after[:, sl]).astype(BF16)
        st_ref[h] = st * tot[:, sl] + _dot(v.T.astype(BF16), khat)


def _hgrn_scan(hq, hk, hg, hv, mall, masks):
    c = SCAN_CHUNK
    n_ctx_chunks = CTX_LEN // c
    n_lat_chunks = SEQ // c
    steps = n_ctx_chunks + n_lat_chunks

    def rb(b, d, j):
        ctx_blk = N_LAT // c + n_ctx_chunks * b + jnp.where(d == 0, j, n_ctx_chunks - 1 - j)
        jl = j - n_ctx_chunks
        lat_blk = b * n_lat_chunks + jnp.where(d == 0, jl, n_lat_chunks - 1 - jl)
        return jnp.where(j < n_ctx_chunks, ctx_blk, lat_blk)

    row = pl.BlockSpec((c, 512), lambda b, d, j: (rb(b, d, j), 0))
    drow = pl.BlockSpec((1, c, 512), lambda b, d, j: (d, rb(b, d, j), 0))
    return pl.pallas_call(
        _scan_kernel,
        grid=(BATCH, 2, steps),
        in_specs=[row, drow, drow, row,
                  pl.BlockSpec((1, SCAN_MATS * c, c), lambda b, d, j: (d, 0, 0)),
                  pl.BlockSpec((1, SCAN_LEVELS, c, c), lambda b, d, j: (d, 0, 0, 0))],
        out_specs=drow,
        out_shape=jax.ShapeDtypeStruct((2, T_ALL, 512), F32),
        scratch_shapes=[pltpu.VMEM((C_HEADS, C_VDIM, C_KDIM), F32)],
        compiler_params=_cparams(("arbitrary", "arbitrary", "arbitrary")),
        name="hgrn_scan",
    )(hq, hk, hg, hv, mall, masks)


def _out_kernel(ya_ref, yb_ref, of_ref, ob_ref, gate_ref, x_ref, mod_ref, gw_ref, w_ref, n2_ref,
                rwh_ref, rwl_ref, rb_ref, xo_ref, h2_ref, ti_ref, tg_ref):
    i = pl.program_id(0)
    midx = jnp.minimum(i // (SEQ // ROW_BLK), BATCH)
    mod = mod_ref[pl.ds(midx, 1), :]
    g1 = mod[:, 2 * D_MODEL:3 * D_MODEL]
    sh2 = mod[:, 3 * D_MODEL:4 * D_MODEL]
    sc2 = mod[:, 4 * D_MODEL:5 * D_MODEL]

    o = of_ref[0] + ob_ref[0]
    y = _dot(ya_ref[...], w_ref[0:256, :]) + _dot(yb_ref[...], w_ref[256:512, :])
    for h in range(C_HEADS):
        sl = slice(h * C_VDIM, (h + 1) * C_VDIM)
        oh = o[:, sl]
        yn = oh * lax.rsqrt(jnp.mean(oh * oh, axis=-1, keepdims=True) + EPS) * gw_ref[...]
        yc = (yn * gate_ref[:, sl]).astype(BF16)
        y = y + _dot(yc, w_ref[512 + h * C_VDIM:512 + (h + 1) * C_VDIM, :])
    x = x_ref[...] + g1 * y
    xo_ref[...] = x
    h2 = x * lax.rsqrt(jnp.mean(x * x, axis=-1, keepdims=True) + EPS) * n2_ref[...]
    h2 = h2 * (1.0 + sc2) + sh2
    _store_token_tiles(h2_ref, h2)

    hi, lo = _split_bf16(h2)
    rwh = rwh_ref[...]
    logits = _dot(hi, rwh) + _dot(hi, rwl_ref[...]) + _dot(lo, rwh) + rb_ref[...]
    lane = lax.broadcasted_iota(jnp.int32, (ROW_BLK, LANES), 1).astype(F32)
    ti = jnp.zeros((ROW_BLK, LANES), F32)
    ex = jnp.zeros((ROW_BLK, LANES), F32)
    den = jnp.zeros((ROW_BLK, 1), F32)
    top = None
    for k in range(TOP_K):
        m = logits.max(axis=-1, keepdims=True)
        idx = jnp.min(jnp.where(logits == m, lane, float(LANES)), axis=-1, keepdims=True)
        if top is None:
            top = m
        e = jnp.exp(m - top)
        den = den + e
        ti = jnp.where(lane == float(k), idx, ti)
        ex = jnp.where(lane == float(k), e, ex)
        logits = jnp.where(lane == idx, -3.0e38, logits)
    ti_ref[...] = ti.astype(jnp.int32)
    tg_ref[...] = ex / den


def _out_proj(n_rows, ya, yb, o, gate, x_all, mods_l, gw, w_out_b, n2, rwh, rwl, rb):
    nblk = n_rows // ROW_BLK
    row = lambda w: pl.BlockSpec((ROW_BLK, w), lambda i: (i, 0))
    full = lambda a: pl.BlockSpec(a.shape, lambda i: (0,) * a.ndim)
    sd = lambda w, dt: jax.ShapeDtypeStruct((n_rows, w), dt)
    return pl.pallas_call(
        _out_kernel,
        grid=(nblk,),
        in_specs=[row(256), row(256),
                  pl.BlockSpec((1, ROW_BLK, 512), lambda i: (0, i, 0)),
                  pl.BlockSpec((1, ROW_BLK, 512), lambda i: (1, i, 0)),
                  row(512), row(D_MODEL), full(mods_l), full(gw), full(w_out_b), full(n2),
                  full(rwh), full(rwl), full(rb)],
        out_specs=[row(D_MODEL), pl.BlockSpec((ROW_BLK * TILE_ROWS, LANES), lambda i: (i, 0)), row(LANES), row(LANES)],
        out_shape=[sd(D_MODEL, F32), jax.ShapeDtypeStruct((n_rows * TILE_ROWS, LANES), F32),
                   sd(LANES, jnp.int32), sd(LANES, F32)],
        compiler_params=_cparams(("arbitrary",)),
        name="out_proj_router",
    )(ya, yb, o, o, gate, x_all, mods_l, gw, w_out_b, n2, rwh, rwl, rb)


def _rank_kernel(ti_ref, tri_ref, rank_ref, cnt_ref, carry_ref):
    i = pl.program_id(0)

    @pl.when(i == 0)
    def _():
        carry_ref[...] = jnp.zeros_like(carry_ref)

    ti = ti_ref[...]
    lane = lax.broadcasted_iota(jnp.int32, (ROW_BLK, LANES), 1)
    onehots = [(lane == ti[:, k:k + 1]).astype(F32) for k in range(TOP_K)]
    tot = onehots[0] + onehots[1] + onehots[2] + onehots[3]
    before = _dot(tri_ref[...], tot.astype(BF16)) + carry_ref[0:1, :]
    r = jnp.zeros((ROW_BLK, LANES), F32)
    for k in range(TOP_K):
        rk = jnp.sum(onehots[k] * before, axis=-1, keepdims=True)
        r = jnp.where(lane == k, rk, r)
    rank_ref[...] = r.astype(jnp.int32)
    carry = carry_ref[0:1, :] + jnp.sum(tot, axis=0, keepdims=True)
    carry_ref[0:1, :] = carry
    cnt_ref[...] = jnp.broadcast_to(carry, (8, LANES))


def _expert_ranks(top_i, tri):
    n_rows = top_i.shape[0]
    return pl.pallas_call(
        _rank_kernel,
        grid=(n_rows // ROW_BLK,),
        in_specs=[pl.BlockSpec((ROW_BLK, LANES), lambda i: (i, 0)),
                  pl.BlockSpec((ROW_BLK, ROW_BLK), lambda i: (0, 0))],
        out_specs=[pl.BlockSpec((ROW_BLK, LANES), lambda i: (i, 0)),
                   pl.BlockSpec((8, LANES), lambda i: (0, 0))],
        out_shape=[jax.ShapeDtypeStruct((n_rows, LANES), jnp.int32), jax.ShapeDtypeStruct((8, LANES), F32)],
        scratch_shapes=[pltpu.VMEM((8, LANES), F32)],
        compiler_params=_cparams(("arbitrary",)),
        name="expert_ranks",
    )(top_i, tri)


DEINT = 256


def _moe_kernel(blk_e_ref, src_ref, src_next_ref, dst_ref, h_ref, w1_ref, b1_ref, w2_ref, b2_ref, perm_ref, ys_ref,
                xbuf, ybuf, w1p_ref, w2p_ref, gsem, ssem, *, spare_row):
    i = pl.program_id(0)
    last = pl.num_programs(0) - 1
    slot = i % 2
    rows = lambda start: pl.ds(pl.multiple_of(start, TILE_ROWS), TILE_ROWS)

    def gather(src_row, r, s):
        return pltpu.make_async_copy(h_ref.at[rows(src_row), :], xbuf.at[s, rows(r * TILE_ROWS), :], gsem.at[s])

    def scatter(dst_row, r):
        return pltpu.make_async_copy(ybuf.at[rows(r * TILE_ROWS), :], ys_ref.at[rows(dst_row), :], ssem)

    @pl.when(i == 0)
    def _():
        ybuf[...] = jnp.zeros_like(ybuf)
        for r in range(MOE_BLOCK):
            scatter(spare_row + r * TILE_ROWS, r).start()
            gather(src_ref[0, 0, r], r, 0).start()

    new_expert = (i == 0) | (blk_e_ref[i] != blk_e_ref[jnp.maximum(i - 1, 0)])

    @pl.when(new_expert)
    def _():
        perm = perm_ref[...]
        half = DEINT // 2
        for j in range(2 * D_FF // DEINT):
            r = _dot(w1_ref[0, 0, :, j * DEINT:(j + 1) * DEINT].astype(BF16), perm)
            w1p_ref[:, j * half:(j + 1) * half] = r[:, 0:half].astype(BF16)
            w1p_ref[:, D_FF + j * half:D_FF + (j + 1) * half] = r[:, half:DEINT].astype(BF16)
        w2p_ref[...] = w2_ref[0, 0].astype(BF16)

    for r in range(MOE_BLOCK):
        gather(src_next_ref[0, 0, r], r, 1 - slot).start()
    for r in range(MOE_BLOCK):
        gather(0, r, slot).wait()
    x = jnp.concatenate(_load_token_tiles(xbuf, MOE_BLOCK, lead=slot), axis=1).astype(BF16)
    u = _dot(x, w1p_ref[...]) + b1_ref[0, 0]
    u_glu = jnp.minimum(u[:, 0:D_FF], SWIGLU_LIMIT)
    u_lin = jnp.clip(u[:, D_FF:2 * D_FF], -SWIGLU_LIMIT, SWIGLU_LIMIT)
    act = u_glu * jax.nn.sigmoid(SWIGLU_ALPHA * u_glu) * (u_lin + 1.0)
    y = _dot(act.astype(BF16), w2p_ref[...]) + b2_ref[0, 0]
    for r in range(MOE_BLOCK):
        scatter(0, r).wait()
    _store_token_tiles(ybuf, y)
    for r in range(MOE_BLOCK):
        scatter(dst_ref[0, 0, r], r).start()

    @pl.when(i == last)
    def _():
        for r in range(MOE_BLOCK):
            gather(0, r, 1 - slot).wait()
            scatter(0, r).wait()


def _expert_mlp(layer, n_rows, h2t, blk_e, src, dst, w1, b1p, w2, b2, perm):
    n_blocks = src.shape[0]
    nxt = lambda i, be: (jnp.minimum(i + 1, n_blocks - 1), 0, 0)
    grid_spec = pltpu.PrefetchScalarGridSpec(
        num_scalar_prefetch=1,
        grid=(n_blocks,),
        in_specs=[pl.BlockSpec((1, 1, MOE_BLOCK), lambda i, be: (i, 0, 0), memory_space=pltpu.SMEM),
                  pl.BlockSpec((1, 1, MOE_BLOCK), nxt, memory_space=pltpu.SMEM),
                  pl.BlockSpec((1, 1, MOE_BLOCK), lambda i, be: (i, 0, 0), memory_space=pltpu.SMEM),
                  pl.BlockSpec(memory_space=pl.ANY),
                  pl.BlockSpec((1, 1, D_MODEL, 2 * D_FF), lambda i, be: (layer, be[i], 0, 0)),
                  pl.BlockSpec((1, 1, 1, 2 * D_FF), lambda i, be: (layer, be[i], 0, 0)),
                  pl.BlockSpec((1, 1, D_FF, D_MODEL), lambda i, be: (layer, be[i], 0, 0)),
                  pl.BlockSpec((1, 1, 1, D_MODEL), lambda i, be: (layer, be[i], 0, 0)),
                  pl.BlockSpec((DEINT, DEINT), lambda i, be: (0, 0))],
        out_specs=pl.BlockSpec(memory_space=pl.ANY),
        scratch_shapes=[pltpu.VMEM((2, MOE_BLOCK * TILE_ROWS, LANES), F32),
                        pltpu.VMEM((MOE_BLOCK * TILE_ROWS, LANES), F32),
                        pltpu.VMEM((D_MODEL, 2 * D_FF), BF16), pltpu.VMEM((D_FF, D_MODEL), BF16),
                        pltpu.SemaphoreType.DMA((2,)), pltpu.SemaphoreType.DMA(())],
    )
    return pl.pallas_call(
        functools.partial(_moe_kernel, spare_row=TOP_K * n_rows * TILE_ROWS),
        grid_spec=grid_spec,
        out_shape=jax.ShapeDtypeStruct(((TOP_K * n_rows + MOE_BLOCK) * TILE_ROWS, LANES), F32),
        compiler_params=_cparams(("arbitrary",)),
        name="expert_mlp",
    )(blk_e, src, src, dst, h2t, w1, b1p, w2, b2, perm)


def _combine_kernel(y0_ref, y1_ref, y2_ref, y3_ref, tg_ref, x_ref, mod_ref, nf_ref, o_ref, *, final):
    i = pl.program_id(0)
    midx = jnp.minimum(i // (SEQ // ROW_BLK), BATCH)
    g2 = mod_ref[pl.ds(midx, 1), 5 * D_MODEL:6 * D_MODEL]
    tg = tg_ref[...]
    f = jnp.zeros((ROW_BLK, D_MODEL), F32)
    for k, y_ref in enumerate((y0_ref, y1_ref, y2_ref, y3_ref)):
        f = f + jnp.concatenate(_load_token_tiles(y_ref, ROW_BLK), axis=1) * tg[:, k:k + 1]
    x = x_ref[...] + g2 * f
    if final:
        x = x * lax.rsqrt(jnp.mean(x * x, axis=-1, keepdims=True) + EPS) * nf_ref[...]
    o_ref[...] = x


def _combine(ys, tg, x_mid, mods_l, nf, final):
    n_rows = x_mid.shape[0]
    nblk = n_rows // ROW_BLK
    full = lambda a: pl.BlockSpec(a.shape, lambda i: (0,) * a.ndim)
    yk = lambda k: pl.BlockSpec((ROW_BLK * TILE_ROWS, LANES), lambda i: (k * nblk + i, 0))
    return pl.pallas_call(
        functools.partial(_combine_kernel, final=final),
        grid=(nblk,),
        in_specs=[yk(0), yk(1), yk(2), yk(3),
                  pl.BlockSpec((ROW_BLK, LANES), lambda i: (i, 0)),
                  pl.BlockSpec((ROW_BLK, D_MODEL), lambda i: (i, 0)), full(mods_l), full(nf)],
        out_specs=pl.BlockSpec((ROW_BLK, D_MODEL), lambda i: (i, 0)),
        out_shape=jax.ShapeDtypeStruct((n_rows, D_MODEL), F32),
        compiler_params=_cparams(("arbitrary",)),
        name="moe_combine_final" if final else "moe_combine",
    )(ys, ys, ys, ys, tg, x_mid, mods_l, nf)


def _moe(layer, h2t, top_i, top_g, x_mid, mods_l, w1, b1p, w2, b2, nf, tri, perm, final):
    n_rows = x_mid.shape[0]
    tk = n_rows * TOP_K
    n_blocks = -(-(tk + N_EXPERTS * (MOE_BLOCK - 1)) // MOE_BLOCK)
    p_rows = n_blocks * MOE_BLOCK

    rank, cnt = _expert_ranks(top_i, tri)
    counts = cnt[0, :N_EXPERTS].astype(jnp.int32)
    padded = (counts + MOE_BLOCK - 1) // MOE_BLOCK * MOE_BLOCK
    pend = jnp.cumsum(padded)
    pstart = pend - padded
    ti4 = top_i[:, :TOP_K]
    dest = pstart[ti4] + rank[:, :TOP_K]
    flat = jnp.arange(1, tk + 1, dtype=jnp.int32)
    code = jnp.zeros((p_rows,), jnp.int32).at[dest.reshape(-1)].set(flat)
    v = jnp.maximum(code - 1, 0)
    src = ((v >> 2) * TILE_ROWS).reshape(n_blocks, 1, MOE_BLOCK)
    spare = TOP_K * n_rows + jnp.arange(p_rows, dtype=jnp.int32) % MOE_BLOCK
    dst = (jnp.where(code > 0, (v & 3) * n_rows + (v >> 2), spare) * TILE_ROWS).reshape(n_blocks, 1, MOE_BLOCK)
    blk_start = jnp.arange(n_blocks, dtype=jnp.int32) * MOE_BLOCK
    blk_e = jnp.minimum(jnp.sum((pend[None, :] <= blk_start[:, None]).astype(jnp.int32), axis=1), N_EXPERTS - 1)

    ys = _expert_mlp(layer, n_rows, h2t, blk_e, src, dst, w1, b1p, w2, b2, perm)
    return _combine(ys, top_g, x_mid, mods_l, nf, final)


def _rope_tables():
    rows = SEQ // GRID_W
    row = jnp.repeat(jnp.arange(rows), GRID_W).astype(F32)
    col = jnp.tile(jnp.arange(GRID_W), rows).astype(F32)
    half = HEAD_DIM // 2
    inv = 1.0 / (ROPE_THETA ** (jnp.arange(0, half, 2, dtype=F32) / half))
    ang_r = row[:, None] * inv
    ang_c = col[:, None] * inv
    ang = jnp.concatenate([ang_r, ang_r, ang_c, ang_c], axis=-1)
    cos, sin = jnp.cos(ang), jnp.sin(ang)
    first = (jnp.arange(HEAD_DIM) % 32) < 16
    sa = jnp.where(first, -sin, 0.0)
    sb = jnp.where(first, 0.0, sin)
    ext = lambda t, fill: jnp.concatenate(
        [jnp.tile(t, (1, 2)), jnp.full((ROW_BLK, LANES), fill, F32)], axis=0)
    return ext(cos, 1.0), ext(sa, 0.0), ext(sb, 0.0)


def kernel(x, c, ctx, c_ctx, w_mod, b_mod, norm1_w, norm2_w, w_in, w_out, attn_sink, q_norm_w, k_norm_w,
           hgrn_lb, gate_norm_w, router_w, router_b, w1, b1, w2, b2, final_norm_w):
    cos_t, sa_t, sb_t = _rope_tables()
    lbs = jax.nn.softmax(hgrn_lb.astype(F32), axis=0)
    lbs = jnp.cumsum(lbs, axis=0) - lbs[0]
    mall_np, masks_np = _scan_constants()
    mall = jnp.asarray(mall_np, BF16)
    masks = jnp.asarray(masks_np, F32)
    gm = jnp.asarray(np.kron(np.eye(2), np.ones((HEAD_DIM, HEAD_DIM))), BF16)
    tri = jnp.asarray(np.tril(np.ones((ROW_BLK, ROW_BLK)), -1), BF16)

    c_all = jnp.concatenate([c, c_ctx[None, :], jnp.zeros((3, D_MODEL), F32)], axis=0)
    mods = _modulation(c_all, w_mod, b_mod)
    x_all = jnp.concatenate([x.reshape(N_LAT, D_MODEL), ctx.reshape(N_CTX, D_MODEL)], axis=0)
    nf = final_norm_w.reshape(1, D_MODEL)
    b1p = jnp.concatenate([b1[..., 0::2], b1[..., 1::2]], axis=-1).reshape(DEPTH, N_EXPERTS, 1, 2 * D_FF)
    b2r = b2.reshape(DEPTH, N_EXPERTS, 1, D_MODEL)
    src = np.concatenate([2 * np.arange(DEINT // 2), 2 * np.arange(DEINT // 2) + 1])
    perm = jnp.asarray(np.arange(DEINT)[:, None] == src[None, :], BF16)

    out = None
    for l in range(DEPTH):
        last = l == DEPTH - 1
        mods_l = mods[l]
        qn = jnp.tile(q_norm_w[l], 2).reshape(1, LANES)
        kn = jnp.tile(k_norm_w[l], 2).reshape(1, LANES)
        sink = jnp.zeros((1, LANES), F32).at[0, :A_HEADS].set(attn_sink[l])
        (qa, ka, va, qb, kb, vb, hq, hk, hg, hv, hgate) = _in_proj(
            x_all, mods_l, norm1_w[l].reshape(1, D_MODEL), w_in[l].astype(BF16), cos_t, sa_t, sb_t, qn, kn, lbs[l], gm)

        ya = _attn_a_latent(qa, ka, va, sink)
        yb = _attn_b_latent(qb, kb, vb)
        o = _hgrn_scan(hq, hk, hg, hv, mall, masks)
        if not last:
            ya = jnp.concatenate([ya, _attn_ctx(qa, ka, va, sink)], axis=0)
            yb = jnp.concatenate([yb, _attn_ctx(qb, kb, vb, None)], axis=0)
        n_rows = N_LAT if last else T_ALL

        rw = jnp.zeros((D_MODEL, LANES), F32).at[:, :N_EXPERTS].set(router_w[l])
        rwh, rwl = _split_bf16(rw)
        rb = jnp.full((1, LANES), NEG_INF, F32).at[0, :N_EXPERTS].set(router_b[l])
        x_mid, h2, top_i, top_g = _out_proj(
            n_rows, ya, yb, o, hgate, x_all, mods_l, jnp.tile(gate_norm_w[l], 1).reshape(1, C_VDIM),
            w_out[l].astype(BF16), norm2_w[l].reshape(1, D_MODEL), rwh, rwl, rb)

        x_all = _moe(l, h2, top_i, top_g, x_mid, mods_l, w1, b1p, w2, b2r, nf, tri, perm, last)
        out = x_all
    return out.reshape(BATCH, SEQ, D_MODEL)
```

```python
import functools

import numpy as np
import jax
import jax.numpy as jnp
from jax import lax
from jax.experimental import pallas as pl
from jax.experimental.pallas import tpu as pltpu

F32 = jnp.float32
BF16 = jnp.bfloat16

D_MODEL = 1024
BATCH = 4
SEQ = 4096
DEPTH = 2
GRID_W = 64
CTX_LEN = 256
HEAD_DIM = 64
ATTN_SCALE = HEAD_DIM ** -0.5
ROPE_THETA = 10000.0
A_HEADS = 4
B_HEADS = 4
WINDOW = 128
C_HEADS = 4
C_KDIM = 128
C_VDIM = 128
IN_WIDTH = 3584
MIX_WIDTH = 1024
N_EXPERTS = 32
TOP_K = 4
D_FF = 1024
SWIGLU_ALPHA = 1.702
SWIGLU_LIMIT = 7.0
MOE_BLOCK = 256
EPS = 1e-6
NEG_INF = -1e30
TINY = 1e-30

N_LAT = BATCH * SEQ
N_CTX = BATCH * CTX_LEN
T_ALL = N_LAT + N_CTX
ROW_BLK = 256
LANES = 128
SCAN_CHUNK = 128
SCAN_LEVELS = 7
SCAN_MATS = SCAN_LEVELS + 1
VMEM_LIMIT = 56 * 1024 * 1024


def _cparams(sem):
    return pltpu.CompilerParams(dimension_semantics=sem, vmem_limit_bytes=VMEM_LIMIT)


def _split_bf16(a):
    hi = a.astype(BF16)
    lo = (a - hi.astype(F32)).astype(BF16)
    return hi, lo


def _dot(a, b):
    return jnp.dot(a, b, preferred_element_type=F32)


def _dot_nt(a, b):
    return lax.dot_general(a, b, (((1,), (1,)), ((), ())), preferred_element_type=F32)


def _silu(a):
    return a * jax.nn.sigmoid(a)


TILE_ROWS = D_MODEL // LANES


def _store_token_tiles(ref, val):
    n = val.shape[0]
    for c in range(TILE_ROWS):
        ref[pl.ds(c, n, stride=TILE_ROWS), :] = val[:, c * LANES:(c + 1) * LANES]


def _load_token_tiles(ref, n, lead=None):
    cols = []
    for c in range(TILE_ROWS):
        if lead is None:
            cols.append(ref[pl.ds(c, n, stride=TILE_ROWS), :])
        else:
            cols.append(ref[lead, pl.ds(c, n, stride=TILE_ROWS), :])
    return cols


def _mod_kernel(c_ref, w_ref, b_ref, o_ref):
    a = _silu(c_ref[...]).astype(BF16)
    o_ref[0] = _dot(a, w_ref[0].astype(BF16)) + b_ref[0]


def _modulation(c_all, w_mod, b_mod):
    tn = 1536
    return pl.pallas_call(
        _mod_kernel,
        grid=(DEPTH, 6 * D_MODEL // tn),
        in_specs=[pl.BlockSpec((8, D_MODEL), lambda l, j: (0, 0)),
                  pl.BlockSpec((1, D_MODEL, tn), lambda l, j: (l, 0, j)),
                  pl.BlockSpec((1, 1, tn), lambda l, j: (l, 0, j))],
        out_specs=pl.BlockSpec((1, 8, tn), lambda l, j: (l, 0, j)),
        out_shape=jax.ShapeDtypeStruct((DEPTH, 8, 6 * D_MODEL), F32),
        compiler_params=_cparams(("arbitrary", "arbitrary")),
        name="modulation",
    )(c_all, w_mod, b_mod.reshape(DEPTH, 1, 6 * D_MODEL))


def _in_kernel(x_ref, mod_ref, n1_ref, w_ref, cos_ref, sa_ref, sb_ref, qn_ref, kn_ref, lb_ref, gm_ref,
               qa_ref, ka_ref, va_ref, qb_ref, kb_ref, vb_ref, hq_ref, hk_ref, hg_ref, hv_ref, hgate_ref):
    i = pl.program_id(0)
    midx = jnp.minimum(i // (SEQ // ROW_BLK), BATCH)
    mod = mod_ref[pl.ds(midx, 1), :]
    sh1 = mod[:, 0:D_MODEL]
    sc1 = mod[:, D_MODEL:2 * D_MODEL]
    x = x_ref[...]
    h = x * lax.rsqrt(jnp.mean(x * x, axis=-1, keepdims=True) + EPS) * n1_ref[...]
    hb = (h * (1.0 + sc1) + sh1).astype(BF16)

    cos = cos_ref[...]
    sa = sa_ref[...]
    sb = sb_ref[...]
    gm = gm_ref[...]

    def rope(t):
        return t * cos + pltpu.roll(t, LANES - 16, 1) * sa + pltpu.roll(t, 16, 1) * sb

    def headnorm(t, w):
        hi, lo = _split_bf16(t * t)
        ss = _dot(hi, gm) + _dot(lo, gm)
        return t * lax.rsqrt(ss * (1.0 / HEAD_DIM) + EPS) * w

    pa = _dot(hb, w_ref[:, 0:1024])
    for c in range(2):
        qa_ref[:, c * LANES:(c + 1) * LANES] = (rope(pa[:, c * LANES:(c + 1) * LANES]) * ATTN_SCALE).astype(BF16)
    ka_ref[...] = rope(pa[:, 256:384]).astype(BF16)
    va_ref[...] = pa[:, 384:512].astype(BF16)
    qn = qn_ref[...]
    kn = kn_ref[...]
    for c in range(2):
        t = headnorm(pa[:, 512 + c * LANES:512 + (c + 1) * LANES], qn)
        qb_ref[:, c * LANES:(c + 1) * LANES] = (rope(t) * ATTN_SCALE).astype(BF16)
    kb_ref[...] = rope(headnorm(pa[:, 768:896], kn)).astype(BF16)
    vb_ref[...] = pa[:, 896:1024].astype(BF16)

    hq_ref[...] = _silu(_dot(hb, w_ref[:, 1024:1536]))
    for d in range(2):
        z = _dot(hb, w_ref[:, 1536 + 512 * d:2048 + 512 * d])
        lb = lb_ref[d:d + 1, :]
        e = jnp.exp(-jnp.abs(z))
        r = 1.0 / (1.0 + e)
        er = e * r
        pos = z >= 0.0
        sig = jnp.where(pos, r, er)
        sig_neg = jnp.where(pos, er, r)
        f = lb + (1.0 - lb) * sig
        hg_ref[d] = jnp.log(jnp.maximum(f, TINY))
        hk_ref[d] = (1.0 - lb) * sig_neg
    hv_ref[...] = _dot(hb, w_ref[:, 2560:3072])
    hgate_ref[...] = _silu(_dot(hb, w_ref[:, 3072:3584]))


def _in_proj(x_all, mods_l, n1, w_in_b, cos_t, sa_t, sb_t, qn, kn, lb, gm):
    nblk = T_ALL // ROW_BLK
    row = lambda w: pl.BlockSpec((ROW_BLK, w), lambda i: (i, 0))
    full = lambda a: pl.BlockSpec(a.shape, lambda i: (0,) * a.ndim)
    tab = pl.BlockSpec((ROW_BLK, LANES), lambda i: (jnp.where(i < N_LAT // ROW_BLK, i % (SEQ // ROW_BLK), SEQ // ROW_BLK), 0))
    two = pl.BlockSpec((2, ROW_BLK, 512), lambda i: (0, i, 0))
    sd = lambda w, dt: jax.ShapeDtypeStruct((T_ALL, w), dt)
    return pl.pallas_call(
        _in_kernel,
        grid=(nblk,),
        in_specs=[row(D_MODEL), full(mods_l), full(n1), full(w_in_b), tab, tab, tab, full(qn), full(kn), full(lb), full(gm)],
        out_specs=[row(256), row(128), row(128), row(256), row(128), row(128), row(512), two, two, row(512), row(512)],
        out_shape=[sd(256, BF16), sd(128, BF16), sd(128, BF16), sd(256, BF16), sd(128, BF16), sd(128, BF16),
                   sd(512, F32), jax.ShapeDtypeStruct((2, T_ALL, 512), F32), jax.ShapeDtypeStruct((2, T_ALL, 512), F32),
                   sd(512, F32), sd(512, F32)],
        compiler_params=_cparams(("arbitrary",)),
        name="in_proj",
    )(x_all, mods_l, n1, w_in_b, cos_t, sa_t, sb_t, qn, kn, lb, gm)


def _attn_core(q_ref, parts, sink_ref, o_ref, tq):
    lane = lax.broadcasted_iota(jnp.int32, (tq, LANES), 1)
    outs = []
    for h in range(4):
        kvh = h // 2
        q128 = q_ref[:, kvh * LANES:(kvh + 1) * LANES].astype(F32)
        if (h % 2) != kvh:
            q128 = pltpu.roll(q128, HEAD_DIM, 1)
        qm = jnp.where((lane >= kvh * HEAD_DIM) & (lane < (kvh + 1) * HEAD_DIM), q128, 0.0).astype(BF16)
        scores = []
        for (k, _, mask) in parts:
            s = _dot_nt(qm, k)
            if mask is not None:
                s = jnp.where(mask, s, NEG_INF)
            scores.append(s)
        m = scores[0].max(axis=-1, keepdims=True)
        for s in scores[1:]:
            m = jnp.maximum(m, s.max(axis=-1, keepdims=True))
        if sink_ref is not None:
            sk = sink_ref[:, h:h + 1]
            m = jnp.maximum(m, sk)
            den = jnp.exp(sk - m)
        else:
            den = jnp.zeros_like(m)
        acc = jnp.zeros((tq, LANES), F32)
        for s, (_, v, _) in zip(scores, parts):
            p = jnp.exp(s - m)
            den = den + p.sum(axis=-1, keepdims=True)
            acc = acc + _dot(p.astype(BF16), v)
        outs.append(acc / den)
    for c in range(2):
        a, b = outs[2 * c], outs[2 * c + 1]
        if c == 0:
            b = pltpu.roll(b, HEAD_DIM, 1)
        else:
            a = pltpu.roll(a, HEAD_DIM, 1)
        o_ref[:, c * LANES:(c + 1) * LANES] = jnp.where(lane < HEAD_DIM, a, b).astype(BF16)


A_TQ = 512


def _attn_a_kernel(q_ref, kp_ref, kc_ref, kn_ref, vp_ref, vc_ref, vn_ref, kx_ref, vx_ref, sink_ref, o_ref):
    n = pl.program_id(1)
    qi = lax.broadcasted_iota(jnp.int32, (A_TQ, WINDOW), 0)
    kj = lax.broadcasted_iota(jnp.int32, (A_TQ, WINDOW), 1)
    mask_prev = (kj >= qi) & (n > 0)
    mask_next = (kj <= qi - (A_TQ - WINDOW)) & (n < SEQ // A_TQ - 1)
    qc = lax.broadcasted_iota(jnp.int32, (A_TQ, A_TQ), 0)
    kc = lax.broadcasted_iota(jnp.int32, (A_TQ, A_TQ), 1)
    mask_cur = jnp.abs(qc - kc) <= WINDOW
    parts = [(kp_ref[...], vp_ref[...], mask_prev), (kc_ref[...], vc_ref[...], mask_cur),
             (kn_ref[...], vn_ref[...], mask_next), (kx_ref[...], vx_ref[...], None)]
    _attn_core(q_ref, parts, sink_ref, o_ref, A_TQ)


def _attn_a_latent(qa, ka, va, sink):
    nq = SEQ // A_TQ
    nb = SEQ // WINDOW
    per = A_TQ // WINDOW
    q_spec = pl.BlockSpec((A_TQ, 256), lambda b, n: (b * nq + n, 0))
    prev = pl.BlockSpec((WINDOW, LANES), lambda b, n: (b * nb + jnp.maximum(n * per - 1, 0), 0))
    cur = pl.BlockSpec((A_TQ, LANES), lambda b, n: (b * nq + n, 0))
    nxt = pl.BlockSpec((WINDOW, LANES), lambda b, n: (b * nb + jnp.minimum((n + 1) * per, nb - 1), 0))
    ctx = pl.BlockSpec((CTX_LEN, LANES), lambda b, n: (N_LAT // CTX_LEN + b, 0))
    return pl.pallas_call(
        _attn_a_kernel,
        grid=(BATCH, nq),
        in_specs=[q_spec, prev, cur, nxt, prev, cur, nxt, ctx, ctx, pl.BlockSpec((1, LANES), lambda b, n: (0, 0))],
        out_specs=pl.BlockSpec((A_TQ, 256), lambda b, n: (b * nq + n, 0)),
        out_shape=jax.ShapeDtypeStruct((N_LAT, 256), BF16),
        compiler_params=_cparams(("arbitrary", "arbitrary")),
        name="attn_window",
    )(qa, ka, ka, ka, va, va, va, ka, va, sink)


B_TQ = 256


def _attn_b_kernel(q_ref, kl_ref, vl_ref, kx_ref, vx_ref, o_ref):
    parts = [(kl_ref[...], vl_ref[...], None), (kx_ref[...], vx_ref[...], None)]
    _attn_core(q_ref, parts, None, o_ref, B_TQ)


def _attn_b_latent(qb, kb, vb):
    nq = SEQ // B_TQ
    lat = pl.BlockSpec((SEQ, LANES), lambda b, j: (b, 0))
    ctx = pl.BlockSpec((CTX_LEN, LANES), lambda b, j: (N_LAT // CTX_LEN + b, 0))
    return pl.pallas_call(
        _attn_b_kernel,
        grid=(BATCH, nq),
        in_specs=[pl.BlockSpec((B_TQ, 256), lambda b, j: (b * nq + j, 0)), lat, lat, ctx, ctx],
        out_specs=pl.BlockSpec((B_TQ, 256), lambda b, j: (b * nq + j, 0)),
        out_shape=jax.ShapeDtypeStruct((N_LAT, 256), BF16),
        compiler_params=_cparams(("arbitrary", "arbitrary")),
        name="attn_full",
    )(qb, kb, vb, kb, vb)


def _attn_ctx_sink_kernel(q_ref, kx_ref, vx_ref, sink_ref, o_ref):
    _attn_core(q_ref, [(kx_ref[...], vx_ref[...], None)], sink_ref, o_ref, CTX_LEN)


def _attn_ctx_kernel(q_ref, kx_ref, vx_ref, o_ref):
    _attn_core(q_ref, [(kx_ref[...], vx_ref[...], None)], None, o_ref, CTX_LEN)


def _attn_ctx(q, k, v, sink):
    blk = lambda w: pl.BlockSpec((CTX_LEN, w), lambda b: (N_LAT // CTX_LEN + b, 0))
    in_specs = [blk(256), blk(LANES), blk(LANES)]
    args = [q, k, v]
    if sink is not None:
        in_specs.append(pl.BlockSpec((1, LANES), lambda b: (0, 0)))
        args.append(sink)
    return pl.pallas_call(
        _attn_ctx_sink_kernel if sink is not None else _attn_ctx_kernel,
        grid=(BATCH,),
        in_specs=in_specs,
        out_specs=pl.BlockSpec((CTX_LEN, 256), lambda b: (b, 0)),
        out_shape=jax.ShapeDtypeStruct((N_CTX, 256), BF16),
        compiler_params=_cparams(("arbitrary",)),
        name="attn_ctx_sink" if sink is not None else "attn_ctx",
    )(*args)


def _scan_constants():
    c = SCAN_CHUNK
    t = np.arange(c)[:, None]
    u = np.arange(c)[None, :]
    mats = [(u <= t)]
    masks = []
    for lvl in range(SCAN_LEVELS):
        m = 2 ** lvl
        second = ((t // m) % 2 == 1)
        end_first = (t // (2 * m)) * 2 * m + m - 1
        mats.append((second & (u > end_first) & (u <= t)) | ((~second) & (u > t) & (u <= end_first)))
        s = u
        masks.append(second & ((s // m) % 2 == 0) & ((s // (2 * m)) == (t // (2 * m))))
    fwd = np.concatenate(mats, axis=0).astype(np.float32)
    fmask = np.stack(masks).astype(np.float32)
    bwd = fwd.reshape(SCAN_MATS, c, c)[:, ::-1, ::-1].reshape(SCAN_MATS * c, c)
    bmask = fmask[:, ::-1, ::-1]
    both = np.stack([fwd, bwd])
    return np.concatenate([both, both], axis=2), np.stack([fmask, bmask])


def _scan_kernel(q_ref, k_ref, g_ref, v_ref, mall_ref, mask_ref, o_ref, st_ref):
    j = pl.program_id(2)

    @pl.when(j == 0)
    def _():
        st_ref[...] = jnp.zeros_like(st_ref)

    c = SCAN_CHUNK
    lf = g_ref[0]
    hi, lo = _split_bf16(lf)
    sums = _dot(mall_ref[0], jnp.concatenate([hi, lo], axis=0))
    ex = jnp.exp(sums)
    total = jnp.sum(lf, axis=0, keepdims=True)
    ex_after = jnp.exp(total - sums[0:c])
    tot = jnp.exp(total)
    ones = jnp.ones((c, c), BF16)
    for h in range(C_HEADS):
        sl = slice(h * C_KDIM, (h + 1) * C_KDIM)
        q = q_ref[:, sl]
        k = k_ref[0, :, sl]
        v = v_ref[:, sl]
        vb = v.astype(BF16)
        exh = lambda idx: ex[idx * c:(idx + 1) * c, sl]
        st = st_ref[h]
        inter = _dot_nt((q * exh(0)).astype(BF16), st.astype(BF16))
        att = jnp.zeros((c, c), F32)
        for lvl in range(SCAN_LEVELS):
            e = exh(1 + lvl)
            pr = _dot_nt((q * e).astype(BF16), (k * e).astype(BF16))
            att = att + mask_ref[0, lvl] * pr
        diag = _dot((q * k).astype(BF16), ones)
        o_ref[0, :, sl] = inter + _dot(att.astype(BF16), vb) + diag * v
        khat = (k * ex_after[:, sl]).astype(BF16)
        st_ref[h] = st * tot[:, sl] + _dot(v.T.astype(BF16), khat)


def _hgrn_scan(hq, hk, hg, hv, mall, masks):
    c = SCAN_CHUNK
    n_ctx_chunks = CTX_LEN // c
    n_lat_chunks = SEQ // c
    steps = n_ctx_chunks + n_lat_chunks

    def rb(b, d, j):
        ctx_blk = N_LAT // c + n_ctx_chunks * b + jnp.where(d == 0, j, n_ctx_chunks - 1 - j)
        jl = j - n_ctx_chunks
        lat_blk = b * n_lat_chunks + jnp.where(d == 0, jl, n_lat_chunks - 1 - jl)
        return jnp.where(j < n_ctx_chunks, ctx_blk, lat_blk)

    row = pl.BlockSpec((c, 512), lambda b, d, j: (rb(b, d, j), 0))
    drow = pl.BlockSpec((1, c, 512), lambda b, d, j: (d, rb(b, d, j), 0))
    return pl.pallas_call(
        _scan_kernel,
        grid=(BATCH, 2, steps),
        in_specs=[row, drow, drow, row,
                  pl.BlockSpec((1, SCAN_MATS * c, 2 * c), lambda b, d, j: (d, 0, 0)),
                  pl.BlockSpec((1, SCAN_LEVELS, c, c), lambda b, d, j: (d, 0, 0, 0))],
        out_specs=drow,
        out_shape=jax.ShapeDtypeStruct((2, T_ALL, 512), F32),
        scratch_shapes=[pltpu.VMEM((C_HEADS, C_VDIM, C_KDIM), F32)],
        compiler_params=_cparams(("arbitrary", "arbitrary", "arbitrary")),
        name="hgrn_scan",
    )(hq, hk, hg, hv, mall, masks)


def _out_kernel(ya_ref, yb_ref, of_ref, ob_ref, gate_ref, x_ref, mod_ref, gw_ref, w_ref, n2_ref,
                rwh_ref, rwl_ref, rb_ref, xo_ref, h2_ref, ti_ref, tg_ref):
    i = pl.program_id(0)
    midx = jnp.minimum(i // (SEQ // ROW_BLK), BATCH)
    mod = mod_ref[pl.ds(midx, 1), :]
    g1 = mod[:, 2 * D_MODEL:3 * D_MODEL]
    sh2 = mod[:, 3 * D_MODEL:4 * D_MODEL]
    sc2 = mod[:, 4 * D_MODEL:5 * D_MODEL]

    o = of_ref[0] + ob_ref[0]
    y = _dot(ya_ref[...], w_ref[0:256, :]) + _dot(yb_ref[...], w_ref[256:512, :])
    for h in range(C_HEADS):
        sl = slice(h * C_VDIM, (h + 1) * C_VDIM)
        oh = o[:, sl]
        yn = oh * lax.rsqrt(jnp.mean(oh * oh, axis=-1, keepdims=True) + EPS) * gw_ref[...]
        yc = (yn * gate_ref[:, sl]).astype(BF16)
        y = y + _dot(yc, w_ref[512 + h * C_VDIM:512 + (h + 1) * C_VDIM, :])
    x = x_ref[...] + g1 * y
    xo_ref[...] = x
    h2 = x * lax.rsqrt(jnp.mean(x * x, axis=-1, keepdims=True) + EPS) * n2_ref[...]
    h2 = h2 * (1.0 + sc2) + sh2
    _store_token_tiles(h2_ref, h2)

    hi, lo = _split_bf16(h2)
    rwh = rwh_ref[...]
    logits = _dot(hi, rwh) + _dot(hi, rwl_ref[...]) + _dot(lo, rwh) + rb_ref[...]
    lane = lax.broadcasted_iota(jnp.int32, (ROW_BLK, LANES), 1).astype(F32)
    ti = jnp.zeros((ROW_BLK, LANES), F32)
    ex = jnp.zeros((ROW_BLK, LANES), F32)
    den = jnp.zeros((ROW_BLK, 1), F32)
    top = None
    for k in range(TOP_K):
        m = logits.max(axis=-1, keepdims=True)
        idx = jnp.min(jnp.where(logits == m, lane, float(LANES)), axis=-1, keepdims=True)
        if top is None:
            top = m
        e = jnp.exp(m - top)
        den = den + e
        ti = jnp.where(lane == float(k), idx, ti)
        ex = jnp.where(lane == float(k), e, ex)
        logits = jnp.where(lane == idx, -3.0e38, logits)
    ti_ref[...] = ti.astype(jnp.int32)
    tg_ref[...] = ex / den


def _out_proj(n_rows, ya, yb, o, gate, x_all, mods_l, gw, w_out_b, n2, rwh, rwl, rb):
    nblk = n_rows // ROW_BLK
    row = lambda w: pl.BlockSpec((ROW_BLK, w), lambda i: (i, 0))
    full = lambda a: pl.BlockSpec(a.shape, lambda i: (0,) * a.ndim)
    sd = lambda w, dt: jax.ShapeDtypeStruct((n_rows, w), dt)
    return pl.pallas_call(
        _out_kernel,
        grid=(nblk,),
        in_specs=[row(256), row(256),
                  pl.BlockSpec((1, ROW_BLK, 512), lambda i: (0, i, 0)),
                  pl.BlockSpec((1, ROW_BLK, 512), lambda i: (1, i, 0)),
                  row(512), row(D_MODEL), full(mods_l), full(gw), full(w_out_b), full(n2),
                  full(rwh), full(rwl), full(rb)],
        out_specs=[row(D_MODEL), pl.BlockSpec((ROW_BLK * TILE_ROWS, LANES), lambda i: (i, 0)), row(LANES), row(LANES)],
        out_shape=[sd(D_MODEL, F32), jax.ShapeDtypeStruct((n_rows * TILE_ROWS, LANES), F32),
                   sd(LANES, jnp.int32), sd(LANES, F32)],
        compiler_params=_cparams(("arbitrary",)),
        name="out_proj_router",
    )(ya, yb, o, o, gate, x_all, mods_l, gw, w_out_b, n2, rwh, rwl, rb)


def _rank_kernel(ti_ref, tri_ref, rank_ref, cnt_ref, carry_ref):
    i = pl.program_id(0)

    @pl.when(i == 0)
    def _():
        carry_ref[...] = jnp.zeros_like(carry_ref)

    ti = ti_ref[...]
    lane = lax.broadcasted_iota(jnp.int32, (ROW_BLK, LANES), 1)
    onehots = [(lane == ti[:, k:k + 1]).astype(F32) for k in range(TOP_K)]
    tot = onehots[0] + onehots[1] + onehots[2] + onehots[3]
    before = _dot(tri_ref[...], tot.astype(BF16)) + carry_ref[0:1, :]
    r = jnp.zeros((ROW_BLK, LANES), F32)
    for k in range(TOP_K):
        rk = jnp.sum(onehots[k] * before, axis=-1, keepdims=True)
        r = jnp.where(lane == k, rk, r)
    rank_ref[...] = r.astype(jnp.int32)
    carry = carry_ref[0:1, :] + jnp.sum(tot, axis=0, keepdims=True)
    carry_ref[0:1, :] = carry
    cnt_ref[...] = jnp.broadcast_to(carry, (8, LANES))


def _expert_ranks(top_i, tri):
    n_rows = top_i.shape[0]
    return pl.pallas_call(
        _rank_kernel,
        grid=(n_rows // ROW_BLK,),
        in_specs=[pl.BlockSpec((ROW_BLK, LANES), lambda i: (i, 0)),
                  pl.BlockSpec((ROW_BLK, ROW_BLK), lambda i: (0, 0))],
        out_specs=[pl.BlockSpec((ROW_BLK, LANES), lambda i: (i, 0)),
                   pl.BlockSpec((8, LANES), lambda i: (0, 0))],
        out_shape=[jax.ShapeDtypeStruct((n_rows, LANES), jnp.int32), jax.ShapeDtypeStruct((8, LANES), F32)],
        scratch_shapes=[pltpu.VMEM((8, LANES), F32)],
        compiler_params=_cparams(("arbitrary",)),
        name="expert_ranks",
    )(top_i, tri)


DEINT = 256


def _moe_kernel(blk_e_ref, src_ref, src_next_ref, dst_ref, h_ref, w1_ref, b1_ref, w2_ref, b2_ref, perm_ref, ys_ref,
                xbuf, ybuf, w1p_ref, w2p_ref, gsem, ssem, *, spare_row):
    i = pl.program_id(0)
    last = pl.num_programs(0) - 1
    slot = i % 2
    rows = lambda start: pl.ds(pl.multiple_of(start, TILE_ROWS), TILE_ROWS)

    def gather(src_row, r, s):
        return pltpu.make_async_copy(h_ref.at[rows(src_row), :], xbuf.at[s, rows(r * TILE_ROWS), :], gsem.at[s])

    def scatter(dst_row, r):
        return pltpu.make_async_copy(ybuf.at[rows(r * TILE_ROWS), :], ys_ref.at[rows(dst_row), :], ssem)

    @pl.when(i == 0)
    def _():
        ybuf[...] = jnp.zeros_like(ybuf)
        for r in range(MOE_BLOCK):
            scatter(spare_row + r * TILE_ROWS, r).start()
            gather(src_ref[0, 0, r], r, 0).start()

    new_expert = (i == 0) | (blk_e_ref[i] != blk_e_ref[jnp.maximum(i - 1, 0)])

    @pl.when(new_expert)
    def _():
        perm = perm_ref[...]
        half = DEINT // 2
        for j in range(2 * D_FF // DEINT):
            r = _dot(w1_ref[0, 0, :, j * DEINT:(j + 1) * DEINT].astype(BF16), perm)
            w1p_ref[:, j * half:(j + 1) * half] = r[:, 0:half].astype(BF16)
            w1p_ref[:, D_FF + j * half:D_FF + (j + 1) * half] = r[:, half:DEINT].astype(BF16)
        w2p_ref[...] = w2_ref[0, 0].astype(BF16)

    for r in range(MOE_BLOCK):
        gather(src_next_ref[0, 0, r], r, 1 - slot).start(priority=r % 2)
    for r in range(MOE_BLOCK):
        gather(0, r, slot).wait()
    x = jnp.concatenate(_load_token_tiles(xbuf, MOE_BLOCK, lead=slot), axis=1).astype(BF16)
    u = _dot(x, w1p_ref[...]) + b1_ref[0, 0]
    u_glu = jnp.minimum(u[:, 0:D_FF], SWIGLU_LIMIT)
    u_lin = jnp.clip(u[:, D_FF:2 * D_FF], -SWIGLU_LIMIT, SWIGLU_LIMIT)
    act = u_glu * jax.nn.sigmoid(SWIGLU_ALPHA * u_glu) * (u_lin + 1.0)
    y = _dot(act.astype(BF16), w2p_ref[...]) + b2_ref[0, 0]
    for r in range(MOE_BLOCK):
        scatter(0, r).wait()
    _store_token_tiles(ybuf, y)
    for r in range(MOE_BLOCK):
        scatter(dst_ref[0, 0, r], r).start(priority=r % 2)

    @pl.when(i == last)
    def _():
        for r in range(MOE_BLOCK):
            gather(0, r, 1 - slot).wait()
            scatter(0, r).wait()


def _expert_mlp(layer, n_rows, h2t, blk_e, src, dst, w1, b1p, w2, b2, perm):
    n_blocks = src.shape[0]
    nxt = lambda i, be: (jnp.minimum(i + 1, n_blocks - 1), 0, 0)
    grid_spec = pltpu.PrefetchScalarGridSpec(
        num_scalar_prefetch=1,
        grid=(n_blocks,),
        in_specs=[pl.BlockSpec((1, 1, MOE_BLOCK), lambda i, be: (i, 0, 0), memory_space=pltpu.SMEM),
                  pl.BlockSpec((1, 1, MOE_BLOCK), nxt, memory_space=pltpu.SMEM),
                  pl.BlockSpec((1, 1, MOE_BLOCK), lambda i, be: (i, 0, 0), memory_space=pltpu.SMEM),
                  pl.BlockSpec(memory_space=pl.ANY),
                  pl.BlockSpec((1, 1, D_MODEL, 2 * D_FF), lambda i, be: (layer, be[i], 0, 0)),
                  pl.BlockSpec((1, 1, 1, 2 * D_FF), lambda i, be: (layer, be[i], 0, 0)),
                  pl.BlockSpec((1, 1, D_FF, D_MODEL), lambda i, be: (layer, be[i], 0, 0)),
                  pl.BlockSpec((1, 1, 1, D_MODEL), lambda i, be: (layer, be[i], 0, 0)),
                  pl.BlockSpec((DEINT, DEINT), lambda i, be: (0, 0))],
        out_specs=pl.BlockSpec(memory_space=pl.ANY),
        scratch_shapes=[pltpu.VMEM((2, MOE_BLOCK * TILE_ROWS, LANES), F32),
                        pltpu.VMEM((MOE_BLOCK * TILE_ROWS, LANES), F32),
                        pltpu.VMEM((D_MODEL, 2 * D_FF), BF16), pltpu.VMEM((D_FF, D_MODEL), BF16),
                        pltpu.SemaphoreType.DMA((2,)), pltpu.SemaphoreType.DMA(())],
    )
    return pl.pallas_call(
        functools.partial(_moe_kernel, spare_row=TOP_K * n_rows * TILE_ROWS),
        grid_spec=grid_spec,
        out_shape=jax.ShapeDtypeStruct(((TOP_K * n_rows + MOE_BLOCK) * TILE_ROWS, LANES), F32),
        compiler_params=_cparams(("arbitrary",)),
        name="expert_mlp",
    )(blk_e, src, src, dst, h2t, w1, b1p, w2, b2, perm)


def _combine_kernel(y0_ref, y1_ref, y2_ref, y3_ref, tg_ref, x_ref, mod_ref, nf_ref, o_ref, *, final):
    i = pl.program_id(0)
    midx = jnp.minimum(i // (SEQ // ROW_BLK), BATCH)
    g2 = mod_ref[pl.ds(midx, 1), 5 * D_MODEL:6 * D_MODEL]
    tg = tg_ref[...]
    f = jnp.zeros((ROW_BLK, D_MODEL), F32)
    for k, y_ref in enumerate((y0_ref, y1_ref, y2_ref, y3_ref)):
        f = f + jnp.concatenate(_load_token_tiles(y_ref, ROW_BLK), axis=1) * tg[:, k:k + 1]
    x = x_ref[...] + g2 * f
    if final:
        x = x * lax.rsqrt(jnp.mean(x * x, axis=-1, keepdims=True) + EPS) * nf_ref[...]
    o_ref[...] = x


def _combine(ys, tg, x_mid, mods_l, nf, final):
    n_rows = x_mid.shape[0]
    nblk = n_rows // ROW_BLK
    full = lambda a: pl.BlockSpec(a.shape, lambda i: (0,) * a.ndim)
    yk = lambda k: pl.BlockSpec((ROW_BLK * TILE_ROWS, LANES), lambda i: (k * nblk + i, 0))
    return pl.pallas_call(
        functools.partial(_combine_kernel, final=final),
        grid=(nblk,),
        in_specs=[yk(0), yk(1), yk(2), yk(3),
                  pl.BlockSpec((ROW_BLK, LANES), lambda i: (i, 0)),
                  pl.BlockSpec((ROW_BLK, D_MODEL), lambda i: (i, 0)), full(mods_l), full(nf)],
        out_specs=pl.BlockSpec((ROW_BLK, D_MODEL), lambda i: (i, 0)),
        out_shape=jax.ShapeDtypeStruct((n_rows, D_MODEL), F32),
        compiler_params=_cparams(("arbitrary",)),
        name="moe_combine_final" if final else "moe_combine",
    )(ys, ys, ys, ys, tg, x_mid, mods_l, nf)


def _moe(layer, h2t, top_i, top_g, x_mid, mods_l, w1, b1p, w2, b2, nf, tri, perm, final):
    n_rows = x_mid.shape[0]
    tk = n_rows * TOP_K
    n_blocks = -(-(tk + N_EXPERTS * (MOE_BLOCK - 1)) // MOE_BLOCK)
    p_rows = n_blocks * MOE_BLOCK

    rank, cnt = _expert_ranks(top_i, tri)
    counts = cnt[0, :N_EXPERTS].astype(jnp.int32)
    padded = (counts + MOE_BLOCK - 1) // MOE_BLOCK * MOE_BLOCK
    pend = jnp.cumsum(padded)
    pstart = pend - padded
    ti4 = top_i[:, :TOP_K]
    dest = pstart[ti4] + rank[:, :TOP_K]
    flat = jnp.arange(1, tk + 1, dtype=jnp.int32)
    code = jnp.zeros((p_rows,), jnp.int32).at[dest.reshape(-1)].set(flat)
    v = jnp.maximum(code - 1, 0)
    src = ((v >> 2) * TILE_ROWS).reshape(n_blocks, 1, MOE_BLOCK)
    spare = TOP_K * n_rows + jnp.arange(p_rows, dtype=jnp.int32) % MOE_BLOCK
    dst = (jnp.where(code > 0, (v & 3) * n_rows + (v >> 2), spare) * TILE_ROWS).reshape(n_blocks, 1, MOE_BLOCK)
    blk_start = jnp.arange(n_blocks, dtype=jnp.int32) * MOE_BLOCK
    blk_e = jnp.minimum(jnp.sum((pend[None, :] <= blk_start[:, None]).astype(jnp.int32), axis=1), N_EXPERTS - 1)

    ys = _expert_mlp(layer, n_rows, h2t, blk_e, src, dst, w1, b1p, w2, b2, perm)
    return _combine(ys, top_g, x_mid, mods_l, nf, final)


def _rope_tables():
    rows = SEQ // GRID_W
    row = jnp.repeat(jnp.arange(rows), GRID_W).astype(F32)
    col = jnp.tile(jnp.arange(GRID_W), rows).astype(F32)
    half = HEAD_DIM // 2
    inv = 1.0 / (ROPE_THETA ** (jnp.arange(0, half, 2, dtype=F32) / half))
    ang_r = row[:, None] * inv
    ang_c = col[:, None] * inv
    ang = jnp.concatenate([ang_r, ang_r, ang_c, ang_c], axis=-1)
    cos, sin = jnp.cos(ang), jnp.sin(ang)
    first = (jnp.arange(HEAD_DIM) % 32) < 16
    sa = jnp.where(first, -sin, 0.0)
    sb = jnp.where(first, 0.0, sin)
    ext = lambda t, fill: jnp.concatenate(
        [jnp.tile(t, (1, 2)), jnp.full((ROW_BLK, LANES), fill, F32)], axis=0)
    return ext(cos, 1.0), ext(sa, 0.0), ext(sb, 0.0)


def kernel(x, c, ctx, c_ctx, w_mod, b_mod, norm1_w, norm2_w, w_in, w_out, attn_sink, q_norm_w, k_norm_w,
           hgrn_lb, gate_norm_w, router_w, router_b, w1, b1, w2, b2, final_norm_w):
    cos_t, sa_t, sb_t = _rope_tables()
    lbs = jax.nn.softmax(hgrn_lb.astype(F32), axis=0)
    lbs = jnp.cumsum(lbs, axis=0) - lbs[0]
    mall_np, masks_np = _scan_constants()
    mall = jnp.asarray(mall_np, BF16)
    masks = jnp.asarray(masks_np, F32)
    gm = jnp.asarray(np.kron(np.eye(2), np.ones((HEAD_DIM, HEAD_DIM))), BF16)
    tri = jnp.asarray(np.tril(np.ones((ROW_BLK, ROW_BLK)), -1), BF16)

    c_all = jnp.concatenate([c, c_ctx[None, :], jnp.zeros((3, D_MODEL), F32)], axis=0)
    mods = _modulation(c_all, w_mod, b_mod)
    x_all = jnp.concatenate([x.reshape(N_LAT, D_MODEL), ctx.reshape(N_CTX, D_MODEL)], axis=0)
    nf = final_norm_w.reshape(1, D_MODEL)
    b1p = jnp.concatenate([b1[..., 0::2], b1[..., 1::2]], axis=-1).reshape(DEPTH, N_EXPERTS, 1, 2 * D_FF)
    b2r = b2.reshape(DEPTH, N_EXPERTS, 1, D_MODEL)
    src = np.concatenate([2 * np.arange(DEINT // 2), 2 * np.arange(DEINT // 2) + 1])
    perm = jnp.asarray(np.arange(DEINT)[:, None] == src[None, :], BF16)

    out = None
    for l in range(DEPTH):
        last = l == DEPTH - 1
        mods_l = mods[l]
        qn = jnp.tile(q_norm_w[l], 2).reshape(1, LANES)
        kn = jnp.tile(k_norm_w[l], 2).reshape(1, LANES)
        sink = jnp.zeros((1, LANES), F32).at[0, :A_HEADS].set(attn_sink[l])
        (qa, ka, va, qb, kb, vb, hq, hk, hg, hv, hgate) = _in_proj(
            x_all, mods_l, norm1_w[l].reshape(1, D_MODEL), w_in[l].astype(BF16), cos_t, sa_t, sb_t, qn, kn, lbs[l], gm)

        ya = _attn_a_latent(qa, ka, va, sink)
        yb = _attn_b_latent(qb, kb, vb)
        o = _hgrn_scan(hq, hk, hg, hv, mall, masks)
        if not last:
            ya = jnp.concatenate([ya, _attn_ctx(qa, ka, va, sink)], axis=0)
            yb = jnp.concatenate([yb, _attn_ctx(qb, kb, vb, None)], axis=0)
        n_rows = N_LAT if last else T_ALL

        rw = jnp.zeros((D_MODEL, LANES), F32).at[:, :N_EXPERTS].set(router_w[l])
        rwh, rwl = _split_bf16(rw)
        rb = jnp.full((1, LANES), NEG_INF, F32).at[0, :N_EXPERTS].set(router_b[l])
        x_mid, h2, top_i, top_g = _out_proj(
            n_rows, ya, yb, o, hgate, x_all, mods_l, jnp.tile(gate_norm_w[l], 1).reshape(1, C_VDIM),
            w_out[l].astype(BF16), norm2_w[l].reshape(1, D_MODEL), rwh, rwl, rb)

        x_all = _moe(l, h2, top_i, top_g, x_mid, mods_l, w1, b1p, w2, b2r, nf, tri, perm, last)
        out = x_all
    return out.reshape(BATCH, SEQ, D_MODEL)
```

```python
import functools

import numpy as np
import jax
import jax.numpy as jnp
from jax import lax
from jax.experimental import pallas as pl
from jax.experimental.pallas import tpu as pltpu

F32 = jnp.float32
BF16 = jnp.bfloat16

D_MODEL = 1024
BATCH = 4
SEQ = 4096
DEPTH = 2
GRID_W = 64
CTX_LEN = 256
HEAD_DIM = 64
ATTN_SCALE = HEAD_DIM ** -0.5
ROPE_THETA = 10000.0
A_HEADS = 4
B_HEADS = 4
WINDOW = 128
C_HEADS = 4
C_KDIM = 128
C_VDIM = 128
IN_WIDTH = 3584
MIX_WIDTH = 1024
N_EXPERTS = 32
TOP_K = 4
D_FF = 1024
SWIGLU_ALPHA = 1.702
SWIGLU_LIMIT = 7.0
MOE_BLOCK = 256
EPS = 1e-6
NEG_INF = -1e30
TINY = 1e-30

N_LAT = BATCH * SEQ
N_CTX = BATCH * CTX_LEN
T_ALL = N_LAT + N_CTX
ROW_BLK = 256
LANES = 128
SCAN_CHUNK = 128
SCAN_LEVELS = 7
SCAN_MATS = SCAN_LEVELS + 1
VMEM_LIMIT = 56 * 1024 * 1024


def _cparams(sem):
    return pltpu.CompilerParams(dimension_semantics=sem, vmem_limit_bytes=VMEM_LIMIT)


def _split_bf16(a):
    hi = a.astype(BF16)
    lo = (a - hi.astype(F32)).astype(BF16)
    return hi, lo


def _dot(a, b):
    return jnp.dot(a, b, preferred_element_type=F32)


def _dot_nt(a, b):
    return lax.dot_general(a, b, (((1,), (1,)), ((), ())), preferred_element_type=F32)


def _silu(a):
    return a * jax.nn.sigmoid(a)


TILE_ROWS = D_MODEL // LANES
PACKED_TILE_ROWS = TILE_ROWS // 2


def _store_token_tiles(ref, val, rows_per_token=TILE_ROWS):
    n = val.shape[0]
    for c in range(rows_per_token):
        ref[pl.ds(c, n, stride=rows_per_token), :] = val[:, c * LANES:(c + 1) * LANES]


def _load_token_tiles(ref, n, lead=None, rows_per_token=TILE_ROWS):
    cols = []
    for c in range(rows_per_token):
        if lead is None:
            cols.append(ref[pl.ds(c, n, stride=rows_per_token), :])
        else:
            cols.append(ref[lead, pl.ds(c, n, stride=rows_per_token), :])
    return cols


def _pack_bf16_pairs(v):
    half = v.shape[1] // 2
    hi = pltpu.bitcast(v[:, :half].astype(BF16).astype(F32), jnp.uint32)
    lo = pltpu.bitcast(v[:, half:].astype(BF16).astype(F32), jnp.uint32)
    return hi | (lo >> 16)


def _unpack_bf16_pairs(words):
    his = [pltpu.bitcast(w & jnp.uint32(0xFFFF0000), F32) for w in words]
    los = [pltpu.bitcast(w << 16, F32) for w in words]
    return jnp.concatenate(his + los, axis=1).astype(BF16)


def _mod_kernel(c_ref, w_ref, b_ref, o_ref):
    a = _silu(c_ref[...]).astype(BF16)
    o_ref[0] = _dot(a, w_ref[0].astype(BF16)) + b_ref[0]


def _modulation(c_all, w_mod, b_mod):
    tn = 1536
    return pl.pallas_call(
        _mod_kernel,
        grid=(DEPTH, 6 * D_MODEL // tn),
        in_specs=[pl.BlockSpec((8, D_MODEL), lambda l, j: (0, 0)),
                  pl.BlockSpec((1, D_MODEL, tn), lambda l, j: (l, 0, j)),
                  pl.BlockSpec((1, 1, tn), lambda l, j: (l, 0, j))],
        out_specs=pl.BlockSpec((1, 8, tn), lambda l, j: (l, 0, j)),
        out_shape=jax.ShapeDtypeStruct((DEPTH, 8, 6 * D_MODEL), F32),
        compiler_params=_cparams(("arbitrary", "arbitrary")),
        name="modulation",
    )(c_all, w_mod, b_mod.reshape(DEPTH, 1, 6 * D_MODEL))


def _in_kernel(x_ref, mod_ref, n1_ref, w_ref, cos_ref, sa_ref, sb_ref, qn_ref, kn_ref, lb_ref, gm_ref,
               qa_ref, ka_ref, va_ref, qb_ref, kb_ref, vb_ref, hq_ref, hk_ref, hg_ref, hv_ref, hgate_ref):
    i = pl.program_id(0)
    midx = jnp.minimum(i // (SEQ // ROW_BLK), BATCH)
    mod = mod_ref[pl.ds(midx, 1), :]
    sh1 = mod[:, 0:D_MODEL]
    sc1 = mod[:, D_MODEL:2 * D_MODEL]
    x = x_ref[...]
    h = x * lax.rsqrt(jnp.mean(x * x, axis=-1, keepdims=True) + EPS) * n1_ref[...]
    hb = (h * (1.0 + sc1) + sh1).astype(BF16)

    cos = cos_ref[...]
    sa = sa_ref[...]
    sb = sb_ref[...]
    gm = gm_ref[...]

    def rope(t):
        return t * cos + pltpu.roll(t, LANES - 16, 1) * sa + pltpu.roll(t, 16, 1) * sb

    def headnorm(t, w):
        hi, lo = _split_bf16(t * t)
        ss = _dot(hi, gm) + _dot(lo, gm)
        return t * lax.rsqrt(ss * (1.0 / HEAD_DIM) + EPS) * w

    pa = _dot(hb, w_ref[:, 0:1024])
    for c in range(2):
        qa_ref[:, c * LANES:(c + 1) * LANES] = (rope(pa[:, c * LANES:(c + 1) * LANES]) * ATTN_SCALE).astype(BF16)
    ka_ref[...] = rope(pa[:, 256:384]).astype(BF16)
    va_ref[...] = pa[:, 384:512].astype(BF16)
    qn = qn_ref[...]
    kn = kn_ref[...]
    for c in range(2):
        t = headnorm(pa[:, 512 + c * LANES:512 + (c + 1) * LANES], qn)
        qb_ref[:, c * LANES:(c + 1) * LANES] = (rope(t) * ATTN_SCALE).astype(BF16)
    kb_ref[...] = rope(headnorm(pa[:, 768:896], kn)).astype(BF16)
    vb_ref[...] = pa[:, 896:1024].astype(BF16)

    hq_ref[...] = _silu(_dot(hb, w_ref[:, 1024:1536]))
    for d in range(2):
        z = _dot(hb, w_ref[:, 1536 + 512 * d:2048 + 512 * d])
        lb = lb_ref[d:d + 1, :]
        e = jnp.exp(-jnp.abs(z))
        r = 1.0 / (1.0 + e)
        er = e * r
        pos = z >= 0.0
        sig = jnp.where(pos, r, er)
        sig_neg = jnp.where(pos, er, r)
        f = lb + (1.0 - lb) * sig
        hg_ref[d] = jnp.log(jnp.maximum(f, TINY))
        hk_ref[d] = (1.0 - lb) * sig_neg
    hv_ref[...] = _dot(hb, w_ref[:, 2560:3072])
    hgate_ref[...] = _silu(_dot(hb, w_ref[:, 3072:3584]))


def _in_proj(x_all, mods_l, n1, w_in_b, cos_t, sa_t, sb_t, qn, kn, lb, gm):
    nblk = T_ALL // ROW_BLK
    row = lambda w: pl.BlockSpec((ROW_BLK, w), lambda i: (i, 0))
    full = lambda a: pl.BlockSpec(a.shape, lambda i: (0,) * a.ndim)
    tab = pl.BlockSpec((ROW_BLK, LANES), lambda i: (jnp.where(i < N_LAT // ROW_BLK, i % (SEQ // ROW_BLK), SEQ // ROW_BLK), 0))
    two = pl.BlockSpec((2, ROW_BLK, 512), lambda i: (0, i, 0))
    sd = lambda w, dt: jax.ShapeDtypeStruct((T_ALL, w), dt)
    return pl.pallas_call(
        _in_kernel,
        grid=(nblk,),
        in_specs=[row(D_MODEL), full(mods_l), full(n1), full(w_in_b), tab, tab, tab, full(qn), full(kn), full(lb), full(gm)],
        out_specs=[row(256), row(128), row(128), row(256), row(128), row(128), row(512), two, two, row(512), row(512)],
        out_shape=[sd(256, BF16), sd(128, BF16), sd(128, BF16), sd(256, BF16), sd(128, BF16), sd(128, BF16),
                   sd(512, F32), jax.ShapeDtypeStruct((2, T_ALL, 512), F32), jax.ShapeDtypeStruct((2, T_ALL, 512), F32),
                   sd(512, F32), sd(512, F32)],
        compiler_params=_cparams(("arbitrary",)),
        name="in_proj",
    )(x_all, mods_l, n1, w_in_b, cos_t, sa_t, sb_t, qn, kn, lb, gm)


def _attn_core(q_ref, parts, sink_ref, o_ref, tq):
    lane = lax.broadcasted_iota(jnp.int32, (tq, LANES), 1)
    outs = []
    for h in range(4):
        kvh = h // 2
        q128 = q_ref[:, kvh * LANES:(kvh + 1) * LANES].astype(F32)
        if (h % 2) != kvh:
            q128 = pltpu.roll(q128, HEAD_DIM, 1)
        qm = jnp.where((lane >= kvh * HEAD_DIM) & (lane < (kvh + 1) * HEAD_DIM), q128, 0.0).astype(BF16)
        scores = []
        for (k, _, mask) in parts:
            s = _dot_nt(qm, k)
            if mask is not None:
                s = jnp.where(mask, s, NEG_INF)
            scores.append(s)
        m = scores[0].max(axis=-1, keepdims=True)
        for s in scores[1:]:
            m = jnp.maximum(m, s.max(axis=-1, keepdims=True))
        if sink_ref is not None:
            sk = sink_ref[:, h:h + 1]
            m = jnp.maximum(m, sk)
            den = jnp.exp(sk - m)
        else:
            den = jnp.zeros_like(m)
        acc = jnp.zeros((tq, LANES), F32)
        for s, (_, v, _) in zip(scores, parts):
            p = jnp.exp(s - m)
            den = den + p.sum(axis=-1, keepdims=True)
            acc = acc + _dot(p.astype(BF16), v)
        outs.append(acc / den)
    for c in range(2):
        a, b = outs[2 * c], outs[2 * c + 1]
        if c == 0:
            b = pltpu.roll(b, HEAD_DIM, 1)
        else:
            a = pltpu.roll(a, HEAD_DIM, 1)
        o_ref[:, c * LANES:(c + 1) * LANES] = jnp.where(lane < HEAD_DIM, a, b).astype(BF16)


A_TQ = 512


def _attn_a_kernel(q_ref, kp_ref, kc_ref, kn_ref, vp_ref, vc_ref, vn_ref, kx_ref, vx_ref, sink_ref, o_ref):
    n = pl.program_id(1)
    qi = lax.broadcasted_iota(jnp.int32, (A_TQ, WINDOW), 0)
    kj = lax.broadcasted_iota(jnp.int32, (A_TQ, WINDOW), 1)
    mask_prev = (kj >= qi) & (n > 0)
    mask_next = (kj <= qi - (A_TQ - WINDOW)) & (n < SEQ // A_TQ - 1)
    qc = lax.broadcasted_iota(jnp.int32, (A_TQ, A_TQ), 0)
    kc = lax.broadcasted_iota(jnp.int32, (A_TQ, A_TQ), 1)
    mask_cur = jnp.abs(qc - kc) <= WINDOW
    parts = [(kp_ref[...], vp_ref[...], mask_prev), (kc_ref[...], vc_ref[...], mask_cur),
             (kn_ref[...], vn_ref[...], mask_next), (kx_ref[...], vx_ref[...], None)]
    _attn_core(q_ref, parts, sink_ref, o_ref, A_TQ)


def _attn_a_latent(qa, ka, va, sink):
    nq = SEQ // A_TQ
    nb = SEQ // WINDOW
    per = A_TQ // WINDOW
    q_spec = pl.BlockSpec((A_TQ, 256), lambda b, n: (b * nq + n, 0))
    prev = pl.BlockSpec((WINDOW, LANES), lambda b, n: (b * nb + jnp.maximum(n * per - 1, 0), 0))
    cur = pl.BlockSpec((A_TQ, LANES), lambda b, n: (b * nq + n, 0))
    nxt = pl.BlockSpec((WINDOW, LANES), lambda b, n: (b * nb + jnp.minimum((n + 1) * per, nb - 1), 0))
    ctx = pl.BlockSpec((CTX_LEN, LANES), lambda b, n: (N_LAT // CTX_LEN + b, 0))
    return pl.pallas_call(
        _attn_a_kernel,
        grid=(BATCH, nq),
        in_specs=[q_spec, prev, cur, nxt, prev, cur, nxt, ctx, ctx, pl.BlockSpec((1, LANES), lambda b, n: (0, 0))],
        out_specs=pl.BlockSpec((A_TQ, 256), lambda b, n: (b * nq + n, 0)),
        out_shape=jax.ShapeDtypeStruct((N_LAT, 256), BF16),
        compiler_params=_cparams(("arbitrary", "arbitrary")),
        name="attn_window",
    )(qa, ka, ka, ka, va, va, va, ka, va, sink)


B_TQ = 256


def _attn_b_kernel(q_ref, kl_ref, vl_ref, kx_ref, vx_ref, o_ref):
    parts = [(kl_ref[...], vl_ref[...], None), (kx_ref[...], vx_ref[...], None)]
    _attn_core(q_ref, parts, None, o_ref, B_TQ)


def _attn_b_latent(qb, kb, vb):
    nq = SEQ // B_TQ
    lat = pl.BlockSpec((SEQ, LANES), lambda b, j: (b, 0))
    ctx = pl.BlockSpec((CTX_LEN, LANES), lambda b, j: (N_LAT // CTX_LEN + b, 0))
    return pl.pallas_call(
        _attn_b_kernel,
        grid=(BATCH, nq),
        in_specs=[pl.BlockSpec((B_TQ, 256), lambda b, j: (b * nq + j, 0)), lat, lat, ctx, ctx],
        out_specs=pl.BlockSpec((B_TQ, 256), lambda b, j: (b * nq + j, 0)),
        out_shape=jax.ShapeDtypeStruct((N_LAT, 256), BF16),
        compiler_params=_cparams(("arbitrary", "arbitrary")),
        name="attn_full",
    )(qb, kb, vb, kb, vb)


def _attn_ctx_sink_kernel(q_ref, kx_ref, vx_ref, sink_ref, o_ref):
    _attn_core(q_ref, [(kx_ref[...], vx_ref[...], None)], sink_ref, o_ref, CTX_LEN)


def _attn_ctx_kernel(q_ref, kx_ref, vx_ref, o_ref):
    _attn_core(q_ref, [(kx_ref[...], vx_ref[...], None)], None, o_ref, CTX_LEN)


def _attn_ctx(q, k, v, sink):
    blk = lambda w: pl.BlockSpec((CTX_LEN, w), lambda b: (N_LAT // CTX_LEN + b, 0))
    in_specs = [blk(256), blk(LANES), blk(LANES)]
    args = [q, k, v]
    if sink is not None:
        in_specs.append(pl.BlockSpec((1, LANES), lambda b: (0, 0)))
        args.append(sink)
    return pl.pallas_call(
        _attn_ctx_sink_kernel if sink is not None else _attn_ctx_kernel,
        grid=(BATCH,),
        in_specs=in_specs,
        out_specs=pl.BlockSpec((CTX_LEN, 256), lambda b: (b, 0)),
        out_shape=jax.ShapeDtypeStruct((N_CTX, 256), BF16),
        compiler_params=_cparams(("arbitrary",)),
        name="attn_ctx_sink" if sink is not None else "attn_ctx",
    )(*args)


def _scan_constants():
    c = SCAN_CHUNK
    t = np.arange(c)[:, None]
    u = np.arange(c)[None, :]
    mats = [(u <= t)]
    masks = []
    for lvl in range(SCAN_LEVELS):
        m = 2 ** lvl
        second = ((t // m) % 2 == 1)
        end_first = (t // (2 * m)) * 2 * m + m - 1
        mats.append((second & (u > end_first) & (u <= t)) | ((~second) & (u > t) & (u <= end_first)))
        s = u
        masks.append(second & ((s // m) % 2 == 0) & ((s // (2 * m)) == (t // (2 * m))))
    fwd = np.concatenate(mats, axis=0).astype(np.float32)
    fmask = np.stack(masks).astype(np.float32)
    bwd = fwd.reshape(SCAN_MATS, c, c)[:, ::-1, ::-1].reshape(SCAN_MATS * c, c)
    bmask = fmask[:, ::-1, ::-1]
    both = np.stack([fwd, bwd])
    return np.concatenate([both, both], axis=2), np.stack([fmask, bmask])


def _scan_chunk(d, q_ref, k_ref, g_ref, v_ref, mall_ref, mask_ref, o_ref, st_ref):
    c = SCAN_CHUNK
    lf = g_ref[0]
    hi, lo = _split_bf16(lf)
    sums = _dot(mall_ref[d], jnp.concatenate([hi, lo], axis=0))
    ex = jnp.exp(sums)
    total = jnp.sum(lf, axis=0, keepdims=True)
    ex_after = jnp.exp(total - sums[0:c])
    tot = jnp.exp(total)
    ones = jnp.ones((c, c), BF16)
    for h in range(C_HEADS):
        sl = slice(h * C_KDIM, (h + 1) * C_KDIM)
        q = q_ref[:, sl]
        k = k_ref[0, :, sl]
        v = v_ref[:, sl]
        vb = v.astype(BF16)
        exh = lambda idx: ex[idx * c:(idx + 1) * c, sl]
        st = st_ref[d, h]
        inter = _dot_nt((q * exh(0)).astype(BF16), st.astype(BF16))
        att = jnp.zeros((c, c), F32)
        for lvl in range(SCAN_LEVELS):
            e = exh(1 + lvl)
            pr = _dot_nt((q * e).astype(BF16), (k * e).astype(BF16))
            att = att + mask_ref[d, lvl] * pr
        diag = _dot((q * k).astype(BF16), ones)
        o_ref[:, sl] = inter + _dot(att.astype(BF16), vb) + diag * v
        khat = (k * ex_after[:, sl]).astype(BF16)
        st_ref[d, h] = st * tot[:, sl] + _dot(v.T.astype(BF16), khat)


def _scan_kernel(qf_ref, kf_ref, gf_ref, vf_ref, qb_ref, kb_ref, gb_ref, vb_ref, mall_ref, mask_ref,
                 of_ref, ob_ref, st_ref):
    @pl.when(pl.program_id(1) == 0)
    def _():
        st_ref[...] = jnp.zeros_like(st_ref)

    _scan_chunk(0, qf_ref, kf_ref, gf_ref, vf_ref, mall_ref, mask_ref, of_ref, st_ref)
    _scan_chunk(1, qb_ref, kb_ref, gb_ref, vb_ref, mall_ref, mask_ref, ob_ref, st_ref)


def _hgrn_scan(hq, hk, hg, hv, mall, masks):
    c = SCAN_CHUNK
    n_ctx_chunks = CTX_LEN // c
    n_lat_chunks = SEQ // c
    steps = n_ctx_chunks + n_lat_chunks

    def rb(d):
        def index(b, j):
            ctx_blk = N_LAT // c + n_ctx_chunks * b + (j if d == 0 else n_ctx_chunks - 1 - j)
            jl = j - n_ctx_chunks
            lat_blk = b * n_lat_chunks + (jl if d == 0 else n_lat_chunks - 1 - jl)
            return jnp.where(j < n_ctx_chunks, ctx_blk, lat_blk)
        return index

    row = lambda d: pl.BlockSpec((c, 512), lambda b, j: (rb(d)(b, j), 0))
    drow = lambda d: pl.BlockSpec((1, c, 512), lambda b, j: (d, rb(d)(b, j), 0))
    full = lambda a: pl.BlockSpec(a.shape, lambda b, j: (0,) * a.ndim)
    o_sd = jax.ShapeDtypeStruct((T_ALL, 512), F32)
    return pl.pallas_call(
        _scan_kernel,
        grid=(BATCH, steps),
        in_specs=[row(0), drow(0), drow(0), row(0), row(1), drow(1), drow(1), row(1), full(mall), full(masks)],
        out_specs=[row(0), row(1)],
        out_shape=[o_sd, o_sd],
        scratch_shapes=[pltpu.VMEM((2, C_HEADS, C_VDIM, C_KDIM), F32)],
        compiler_params=_cparams(("arbitrary", "arbitrary")),
        name="hgrn_scan",
    )(hq, hk, hg, hv, hq, hk, hg, hv, mall, masks)


def _out_kernel(ya_ref, yb_ref, of_ref, ob_ref, gate_ref, x_ref, mod_ref, gw_ref, w_ref, n2_ref,
                rwh_ref, rwl_ref, rb_ref, xo_ref, h2_ref, ti_ref, tg_ref):
    i = pl.program_id(0)
    midx = jnp.minimum(i // (SEQ // ROW_BLK), BATCH)
    mod = mod_ref[pl.ds(midx, 1), :]
    g1 = mod[:, 2 * D_MODEL:3 * D_MODEL]
    sh2 = mod[:, 3 * D_MODEL:4 * D_MODEL]
    sc2 = mod[:, 4 * D_MODEL:5 * D_MODEL]

    o = of_ref[...] + ob_ref[...]
    y = _dot(ya_ref[...], w_ref[0:256, :]) + _dot(yb_ref[...], w_ref[256:512, :])
    for h in range(C_HEADS):
        sl = slice(h * C_VDIM, (h + 1) * C_VDIM)
        oh = o[:, sl]
        yn = oh * lax.rsqrt(jnp.mean(oh * oh, axis=-1, keepdims=True) + EPS) * gw_ref[...]
        yc = (yn * gate_ref[:, sl]).astype(BF16)
        y = y + _dot(yc, w_ref[512 + h * C_VDIM:512 + (h + 1) * C_VDIM, :])
    x = x_ref[...] + g1 * y
    xo_ref[...] = x
    h2 = x * lax.rsqrt(jnp.mean(x * x, axis=-1, keepdims=True) + EPS) * n2_ref[...]
    h2 = h2 * (1.0 + sc2) + sh2
    _store_token_tiles(h2_ref, _pack_bf16_pairs(h2), PACKED_TILE_ROWS)

    hi, lo = _split_bf16(h2)
    rwh = rwh_ref[...]
    logits = _dot(hi, rwh) + _dot(hi, rwl_ref[...]) + _dot(lo, rwh) + rb_ref[...]
    lane = lax.broadcasted_iota(jnp.int32, (ROW_BLK, LANES), 1).astype(F32)
    ti = jnp.zeros((ROW_BLK, LANES), F32)
    ex = jnp.zeros((ROW_BLK, LANES), F32)
    den = jnp.zeros((ROW_BLK, 1), F32)
    top = None
    for k in range(TOP_K):
        m = logits.max(axis=-1, keepdims=True)
        idx = jnp.min(jnp.where(logits == m, lane, float(LANES)), axis=-1, keepdims=True)
        if top is None:
            top = m
        e = jnp.exp(m - top)
        den = den + e
        ti = jnp.where(lane == float(k), idx, ti)
        ex = jnp.where(lane == float(k), e, ex)
        logits = jnp.where(lane == idx, -3.0e38, logits)
    ti_ref[...] = ti.astype(jnp.int32)
    tg_ref[...] = ex / den


def _out_proj(n_rows, ya, yb, o_f, o_b, gate, x_all, mods_l, gw, w_out_b, n2, rwh, rwl, rb):
    nblk = n_rows // ROW_BLK
    row = lambda w: pl.BlockSpec((ROW_BLK, w), lambda i: (i, 0))
    full = lambda a: pl.BlockSpec(a.shape, lambda i: (0,) * a.ndim)
    sd = lambda w, dt: jax.ShapeDtypeStruct((n_rows, w), dt)
    return pl.pallas_call(
        _out_kernel,
        grid=(nblk,),
        in_specs=[row(256), row(256), row(512), row(512),
                  row(512), row(D_MODEL), full(mods_l), full(gw), full(w_out_b), full(n2),
                  full(rwh), full(rwl), full(rb)],
        out_specs=[row(D_MODEL), pl.BlockSpec((ROW_BLK * PACKED_TILE_ROWS, LANES), lambda i: (i, 0)),
                   row(LANES), row(LANES)],
        out_shape=[sd(D_MODEL, F32), jax.ShapeDtypeStruct((n_rows * PACKED_TILE_ROWS, LANES), jnp.uint32),
                   sd(LANES, jnp.int32), sd(LANES, F32)],
        compiler_params=_cparams(("arbitrary",)),
        name="out_proj_router",
    )(ya, yb, o_f, o_b, gate, x_all, mods_l, gw, w_out_b, n2, rwh, rwl, rb)


def _rank_kernel(ti_ref, tri_ref, rank_ref, cnt_ref, carry_ref):
    i = pl.program_id(0)

    @pl.when(i == 0)
    def _():
        carry_ref[...] = jnp.zeros_like(carry_ref)

    ti = ti_ref[...]
    lane = lax.broadcasted_iota(jnp.int32, (ROW_BLK, LANES), 1)
    onehots = [(lane == ti[:, k:k + 1]).astype(F32) for k in range(TOP_K)]
    tot = onehots[0] + onehots[1] + onehots[2] + onehots[3]
    before = _dot(tri_ref[...], tot.astype(BF16)) + carry_ref[0:1, :]
    r = jnp.zeros((ROW_BLK, LANES), F32)
    for k in range(TOP_K):
        rk = jnp.sum(onehots[k] * before, axis=-1, keepdims=True)
        r = jnp.where(lane == k, rk, r)
    rank_ref[...] = r.astype(jnp.int32)
    carry = carry_ref[0:1, :] + jnp.sum(tot, axis=0, keepdims=True)
    carry_ref[0:1, :] = carry
    cnt_ref[...] = jnp.broadcast_to(carry, (8, LANES))


def _expert_ranks(top_i, tri):
    n_rows = top_i.shape[0]
    return pl.pallas_call(
        _rank_kernel,
        grid=(n_rows // ROW_BLK,),
        in_specs=[pl.BlockSpec((ROW_BLK, LANES), lambda i: (i, 0)),
                  pl.BlockSpec((ROW_BLK, ROW_BLK), lambda i: (0, 0))],
        out_specs=[pl.BlockSpec((ROW_BLK, LANES), lambda i: (i, 0)),
                   pl.BlockSpec((8, LANES), lambda i: (0, 0))],
        out_shape=[jax.ShapeDtypeStruct((n_rows, LANES), jnp.int32), jax.ShapeDtypeStruct((8, LANES), F32)],
        scratch_shapes=[pltpu.VMEM((8, LANES), F32)],
        compiler_params=_cparams(("arbitrary",)),
        name="expert_ranks",
    )(top_i, tri)


DEINT = 256


def _moe_kernel(blk_e_ref, src_ref, src_next_ref, dst_ref, h_ref, w1_ref, b1_ref, w2_ref, b2_ref, perm_ref, ys_ref,
                xbuf, ybuf, w1p_ref, w2p_ref, gsem, ssem, *, spare_row):
    i = pl.program_id(0)
    last = pl.num_programs(0) - 1
    slot = i % 2
    rows = lambda start, n: pl.ds(pl.multiple_of(start, n), n)

    def gather(src_row, r, s):
        n = PACKED_TILE_ROWS
        return pltpu.make_async_copy(h_ref.at[rows(src_row, n), :], xbuf.at[s, rows(r * n, n), :], gsem.at[s])

    def scatter(dst_row, r):
        n = TILE_ROWS
        return pltpu.make_async_copy(ybuf.at[rows(r * n, n), :], ys_ref.at[rows(dst_row, n), :], ssem)

    @pl.when(i == 0)
    def _():
        ybuf[...] = jnp.zeros_like(ybuf)
        for r in range(MOE_BLOCK):
            scatter(spare_row + r * TILE_ROWS, r).start()
            gather(src_ref[0, 0, r], r, 0).start()

    new_expert = (i == 0) | (blk_e_ref[i] != blk_e_ref[jnp.maximum(i - 1, 0)])

    @pl.when(new_expert)
    def _():
        perm = perm_ref[...]
        half = DEINT // 2
        for j in range(2 * D_FF // DEINT):
            r = _dot(w1_ref[0, 0, :, j * DEINT:(j + 1) * DEINT].astype(BF16), perm)
            w1p_ref[:, j * half:(j + 1) * half] = r[:, 0:half].astype(BF16)
            w1p_ref[:, D_FF + j * half:D_FF + (j + 1) * half] = r[:, half:DEINT].astype(BF16)
        w2p_ref[...] = w2_ref[0, 0].astype(BF16)

    for r in range(MOE_BLOCK):
        gather(src_next_ref[0, 0, r], r, 1 - slot).start(priority=r % 2)
    for r in range(MOE_BLOCK):
        gather(0, r, slot).wait()
    x = _unpack_bf16_pairs(_load_token_tiles(xbuf, MOE_BLOCK, lead=slot, rows_per_token=PACKED_TILE_ROWS))
    u = _dot(x, w1p_ref[...]) + b1_ref[0, 0]
    u_glu = jnp.minimum(u[:, 0:D_FF], SWIGLU_LIMIT)
    u_lin = jnp.clip(u[:, D_FF:2 * D_FF], -SWIGLU_LIMIT, SWIGLU_LIMIT)
    act = u_glu * jax.nn.sigmoid(SWIGLU_ALPHA * u_glu) * (u_lin + 1.0)
    y = _dot(act.astype(BF16), w2p_ref[...]) + b2_ref[0, 0]
    for r in range(MOE_BLOCK):
        scatter(0, r).wait()
    _store_token_tiles(ybuf, y)
    for r in range(MOE_BLOCK):
        scatter(dst_ref[0, 0, r], r).start(priority=r % 2)

    @pl.when(i == last)
    def _():
        for r in range(MOE_BLOCK):
            gather(0, r, 1 - slot).wait()
            scatter(0, r).wait()


def _expert_mlp(layer, n_rows, h2t, blk_e, src, dst, w1, b1p, w2, b2, perm):
    n_blocks = src.shape[0]
    nxt = lambda i, be: (jnp.minimum(i + 1, n_blocks - 1), 0, 0)
    grid_spec = pltpu.PrefetchScalarGridSpec(
        num_scalar_prefetch=1,
        grid=(n_blocks,),
        in_specs=[pl.BlockSpec((1, 1, MOE_BLOCK), lambda i, be: (i, 0, 0), memory_space=pltpu.SMEM),
                  pl.BlockSpec((1, 1, MOE_BLOCK), nxt, memory_space=pltpu.SMEM),
                  pl.BlockSpec((1, 1, MOE_BLOCK), lambda i, be: (i, 0, 0), memory_space=pltpu.SMEM),
                  pl.BlockSpec(memory_space=pl.ANY),
                  pl.BlockSpec((1, 1, D_MODEL, 2 * D_FF), lambda i, be: (layer, be[i], 0, 0)),
                  pl.BlockSpec((1, 1, 1, 2 * D_FF), lambda i, be: (layer, be[i], 0, 0)),
                  pl.BlockSpec((1, 1, D_FF, D_MODEL), lambda i, be: (layer, be[i], 0, 0)),
                  pl.BlockSpec((1, 1, 1, D_MODEL), lambda i, be: (layer, be[i], 0, 0)),
                  pl.BlockSpec((DEINT, DEINT), lambda i, be: (0, 0))],
        out_specs=pl.BlockSpec(memory_space=pl.ANY),
        scratch_shapes=[pltpu.VMEM((2, MOE_BLOCK * PACKED_TILE_ROWS, LANES), jnp.uint32),
                        pltpu.VMEM((MOE_BLOCK * TILE_ROWS, LANES), F32),
                        pltpu.VMEM((D_MODEL, 2 * D_FF), BF16), pltpu.VMEM((D_FF, D_MODEL), BF16),
                        pltpu.SemaphoreType.DMA((2,)), pltpu.SemaphoreType.DMA(())],
    )
    return pl.pallas_call(
        functools.partial(_moe_kernel, spare_row=TOP_K * n_rows * TILE_ROWS),
        grid_spec=grid_spec,
        out_shape=jax.ShapeDtypeStruct(((TOP_K * n_rows + MOE_BLOCK) * TILE_ROWS, LANES), F32),
        compiler_params=_cparams(("arbitrary",)),
        name="expert_mlp",
    )(blk_e, src, src, dst, h2t, w1, b1p, w2, b2, perm)


def _combine_kernel(y0_ref, y1_ref, y2_ref, y3_ref, tg_ref, x_ref, mod_ref, nf_ref, o_ref, *, final):
    i = pl.program_id(0)
    midx = jnp.minimum(i // (SEQ // ROW_BLK), BATCH)
    g2 = mod_ref[pl.ds(midx, 1), 5 * D_MODEL:6 * D_MODEL]
    tg = tg_ref[...]
    f = jnp.zeros((ROW_BLK, D_MODEL), F32)
    for k, y_ref in enumerate((y0_ref, y1_ref, y2_ref, y3_ref)):
        f = f + jnp.concatenate(_load_token_tiles(y_ref, ROW_BLK), axis=1) * tg[:, k:k + 1]
    x = x_ref[...] + g2 * f
    if final:
        x = x * lax.rsqrt(jnp.mean(x * x, axis=-1, keepdims=True) + EPS) * nf_ref[...]
    o_ref[...] = x


def _combine(ys, tg, x_mid, mods_l, nf, final):
    n_rows = x_mid.shape[0]
    nblk = n_rows // ROW_BLK
    full = lambda a: pl.BlockSpec(a.shape, lambda i: (0,) * a.ndim)
    yk = lambda k: pl.BlockSpec((ROW_BLK * TILE_ROWS, LANES), lambda i: (k * nblk + i, 0))
    return pl.pallas_call(
        functools.partial(_combine_kernel, final=final),
        grid=(nblk,),
        in_specs=[yk(0), yk(1), yk(2), yk(3),
                  pl.BlockSpec((ROW_BLK, LANES), lambda i: (i, 0)),
                  pl.BlockSpec((ROW_BLK, D_MODEL), lambda i: (i, 0)), full(mods_l), full(nf)],
        out_specs=pl.BlockSpec((ROW_BLK, D_MODEL), lambda i: (i, 0)),
        out_shape=jax.ShapeDtypeStruct((n_rows, D_MODEL), F32),
        compiler_params=_cparams(("arbitrary",)),
        name="moe_combine_final" if final else "moe_combine",
    )(ys, ys, ys, ys, tg, x_mid, mods_l, nf)


def _moe(layer, h2t, top_i, top_g, x_mid, mods_l, w1, b1p, w2, b2, nf, tri, perm, final):
    n_rows = x_mid.shape[0]
    tk = n_rows * TOP_K
    n_blocks = -(-(tk + N_EXPERTS * (MOE_BLOCK - 1)) // MOE_BLOCK)
    p_rows = n_blocks * MOE_BLOCK

    rank, cnt = _expert_ranks(top_i, tri)
    counts = cnt[0, :N_EXPERTS].astype(jnp.int32)
    padded = (counts + MOE_BLOCK - 1) // MOE_BLOCK * MOE_BLOCK
    pend = jnp.cumsum(padded)
    pstart = pend - padded
    ti4 = top_i[:, :TOP_K]
    dest = pstart[ti4] + rank[:, :TOP_K]
    flat = jnp.arange(1, tk + 1, dtype=jnp.int32)
    code = jnp.zeros((p_rows,), jnp.int32).at[dest.reshape(-1)].set(flat)
    v = jnp.maximum(code - 1, 0)
    src = ((v >> 2) * PACKED_TILE_ROWS).reshape(n_blocks, 1, MOE_BLOCK)
    spare = TOP_K * n_rows + jnp.arange(p_rows, dtype=jnp.int32) % MOE_BLOCK
    dst = (jnp.where(code > 0, (v & 3) * n_rows + (v >> 2), spare) * TILE_ROWS).reshape(n_blocks, 1, MOE_BLOCK)
    blk_start = jnp.arange(n_blocks, dtype=jnp.int32) * MOE_BLOCK
    blk_e = jnp.minimum(jnp.sum((pend[None, :] <= blk_start[:, None]).astype(jnp.int32), axis=1), N_EXPERTS - 1)

    ys = _expert_mlp(layer, n_rows, h2t, blk_e, src, dst, w1, b1p, w2, b2, perm)
    return _combine(ys, top_g, x_mid, mods_l, nf, final)


def _rope_tables():
    rows = SEQ // GRID_W
    row = jnp.repeat(jnp.arange(rows), GRID_W).astype(F32)
    col = jnp.tile(jnp.arange(GRID_W), rows).astype(F32)
    half = HEAD_DIM // 2
    inv = 1.0 / (ROPE_THETA ** (jnp.arange(0, half, 2, dtype=F32) / half))
    ang_r = row[:, None] * inv
    ang_c = col[:, None] * inv
    ang = jnp.concatenate([ang_r, ang_r, ang_c, ang_c], axis=-1)
    cos, sin = jnp.cos(ang), jnp.sin(ang)
    first = (jnp.arange(HEAD_DIM) % 32) < 16
    sa = jnp.where(first, -sin, 0.0)
    sb = jnp.where(first, 0.0, sin)
    ext = lambda t, fill: jnp.concatenate(
        [jnp.tile(t, (1, 2)), jnp.full((ROW_BLK, LANES), fill, F32)], axis=0)
    return ext(cos, 1.0), ext(sa, 0.0), ext(sb, 0.0)


def kernel(x, c, ctx, c_ctx, w_mod, b_mod, norm1_w, norm2_w, w_in, w_out, attn_sink, q_norm_w, k_norm_w,
           hgrn_lb, gate_norm_w, router_w, router_b, w1, b1, w2, b2, final_norm_w):
    cos_t, sa_t, sb_t = _rope_tables()
    lbs = jax.nn.softmax(hgrn_lb.astype(F32), axis=0)
    lbs = jnp.cumsum(lbs, axis=0) - lbs[0]
    mall_np, masks_np = _scan_constants()
    mall = jnp.asarray(mall_np, BF16)
    masks = jnp.asarray(masks_np, F32)
    gm = jnp.asarray(np.kron(np.eye(2), np.ones((HEAD_DIM, HEAD_DIM))), BF16)
    tri = jnp.asarray(np.tril(np.ones((ROW_BLK, ROW_BLK)), -1), BF16)

    c_all = jnp.concatenate([c, c_ctx[None, :], jnp.zeros((3, D_MODEL), F32)], axis=0)
    mods = _modulation(c_all, w_mod, b_mod)
    x_all = jnp.concatenate([x.reshape(N_LAT, D_MODEL), ctx.reshape(N_CTX, D_MODEL)], axis=0)
    nf = final_norm_w.reshape(1, D_MODEL)
    b1p = jnp.concatenate([b1[..., 0::2], b1[..., 1::2]], axis=-1).reshape(DEPTH, N_EXPERTS, 1, 2 * D_FF)
    b2r = b2.reshape(DEPTH, N_EXPERTS, 1, D_MODEL)
    src = np.concatenate([2 * np.arange(DEINT // 2), 2 * np.arange(DEINT // 2) + 1])
    perm = jnp.asarray(np.arange(DEINT)[:, None] == src[None, :], BF16)

    out = None
    for l in range(DEPTH):
        last = l == DEPTH - 1
        mods_l = mods[l]
        qn = jnp.tile(q_norm_w[l], 2).reshape(1, LANES)
        kn = jnp.tile(k_norm_w[l], 2).reshape(1, LANES)
        sink = jnp.zeros((1, LANES), F32).at[0, :A_HEADS].set(attn_sink[l])
        (qa, ka, va, qb, kb, vb, hq, hk, hg, hv, hgate) = _in_proj(
            x_all, mods_l, norm1_w[l].reshape(1, D_MODEL), w_in[l].astype(BF16), cos_t, sa_t, sb_t, qn, kn, lbs[l], gm)

        ya = _attn_a_latent(qa, ka, va, sink)
        yb = _attn_b_latent(qb, kb, vb)
        o_f, o_b = _hgrn_scan(hq, hk, hg, hv, mall, masks)
        if not last:
            ya = jnp.concatenate([ya, _attn_ctx(qa, ka, va, sink)], axis=0)
            yb = jnp.concatenate([yb, _attn_ctx(qb, kb, vb, None)], axis=0)
        n_rows = N_LAT if last else T_ALL

        rw = jnp.zeros((D_MODEL, LANES), F32).at[:, :N_EXPERTS].set(router_w[l])
        rwh, rwl = _split_bf16(rw)
        rb = jnp.full((1, LANES), NEG_INF, F32).at[0, :N_EXPERTS].set(router_b[l])
        x_mid, h2, top_i, top_g = _out_proj(
            n_rows, ya, yb, o_f, o_b, hgate, x_all, mods_l, gate_norm_w[l].reshape(1, C_VDIM),
            w_out[l].astype(BF16), norm2_w[l].reshape(1, D_MODEL), rwh, rwl, rb)

        x_all = _moe(l, h2, top_i, top_g, x_mid, mods_l, w1, b1p, w2, b2r, nf, tri, perm, last)
        out = x_all
    return out.reshape(BATCH, SEQ, D_MODEL)
```

```python
import functools

import numpy as np
import jax
import jax.numpy as jnp
from jax import lax
from jax.experimental import pallas as pl
from jax.experimental.pallas import tpu as pltpu

F32 = jnp.float32
BF16 = jnp.bfloat16

D_MODEL = 1024
BATCH = 4
SEQ = 4096
DEPTH = 2
GRID_W = 64
CTX_LEN = 256
HEAD_DIM = 64
ATTN_SCALE = HEAD_DIM ** -0.5
ROPE_THETA = 10000.0
A_HEADS = 4
B_HEADS = 4
WINDOW = 128
C_HEADS = 4
C_KDIM = 128
C_VDIM = 128
IN_WIDTH = 3584
MIX_WIDTH = 1024
N_EXPERTS = 32
TOP_K = 4
D_FF = 1024
SWIGLU_ALPHA = 1.702
SWIGLU_LIMIT = 7.0
MOE_BLOCK = 256
EPS = 1e-6
NEG_INF = -1e30
TINY = 1e-30

N_LAT = BATCH * SEQ
N_CTX = BATCH * CTX_LEN
T_ALL = N_LAT + N_CTX
ROW_BLK = 256
LANES = 128
SCAN_CHUNK = 128
SCAN_LEVELS = 7
SCAN_MATS = SCAN_LEVELS + 1
VMEM_LIMIT = 56 * 1024 * 1024


def _cparams(sem):
    return pltpu.CompilerParams(dimension_semantics=sem, vmem_limit_bytes=VMEM_LIMIT)


def _split_bf16(a):
    hi = a.astype(BF16)
    lo = (a - hi.astype(F32)).astype(BF16)
    return hi, lo


def _dot(a, b):
    return jnp.dot(a, b, preferred_element_type=F32)


def _dot_nt(a, b):
    return lax.dot_general(a, b, (((1,), (1,)), ((), ())), preferred_element_type=F32)


def _silu(a):
    return a * jax.nn.sigmoid(a)


TILE_ROWS = D_MODEL // LANES
PACKED_TILE_ROWS = TILE_ROWS // 2


def _store_token_tiles(ref, val, rows_per_token=TILE_ROWS):
    n = val.shape[0]
    for c in range(rows_per_token):
        ref[pl.ds(c, n, stride=rows_per_token), :] = val[:, c * LANES:(c + 1) * LANES]


def _load_token_tiles(ref, n, lead=None, rows_per_token=TILE_ROWS):
    cols = []
    for c in range(rows_per_token):
        if lead is None:
            cols.append(ref[pl.ds(c, n, stride=rows_per_token), :])
        else:
            cols.append(ref[lead, pl.ds(c, n, stride=rows_per_token), :])
    return cols


def _pack_bf16_pairs(v):
    half = v.shape[1] // 2
    hi = pltpu.bitcast(v[:, :half].astype(BF16).astype(F32), jnp.uint32)
    lo = pltpu.bitcast(v[:, half:].astype(BF16).astype(F32), jnp.uint32)
    return hi | (lo >> 16)


def _unpack_bf16_pairs(words):
    his = [pltpu.bitcast(w & jnp.uint32(0xFFFF0000), F32) for w in words]
    los = [pltpu.bitcast(w << 16, F32) for w in words]
    return jnp.concatenate(his + los, axis=1).astype(BF16)


def _mod_kernel(c_ref, w_ref, b_ref, o_ref):
    a = _silu(c_ref[...]).astype(BF16)
    o_ref[0] = _dot(a, w_ref[0].astype(BF16)) + b_ref[0]


def _modulation(c_all, w_mod, b_mod):
    tn = 1536
    return pl.pallas_call(
        _mod_kernel,
        grid=(DEPTH, 6 * D_MODEL // tn),
        in_specs=[pl.BlockSpec((8, D_MODEL), lambda l, j: (0, 0)),
                  pl.BlockSpec((1, D_MODEL, tn), lambda l, j: (l, 0, j)),
                  pl.BlockSpec((1, 1, tn), lambda l, j: (l, 0, j))],
        out_specs=pl.BlockSpec((1, 8, tn), lambda l, j: (l, 0, j)),
        out_shape=jax.ShapeDtypeStruct((DEPTH, 8, 6 * D_MODEL), F32),
        compiler_params=_cparams(("arbitrary", "arbitrary")),
        name="modulation",
    )(c_all, w_mod, b_mod.reshape(DEPTH, 1, 6 * D_MODEL))


def _in_kernel(x_ref, mod_ref, n1_ref, w_ref, cos_ref, sa_ref, sb_ref, qn_ref, kn_ref, lb_ref, gm_ref,
               qa_ref, ka_ref, va_ref, qb_ref, kb_ref, vb_ref, hq_ref, hk_ref, hg_ref, hv_ref, hgate_ref):
    i = pl.program_id(0)
    midx = jnp.minimum(i // (SEQ // ROW_BLK), BATCH)
    mod = mod_ref[pl.ds(midx, 1), :]
    sh1 = mod[:, 0:D_MODEL]
    sc1 = mod[:, D_MODEL:2 * D_MODEL]
    x = x_ref[...]
    h = x * lax.rsqrt(jnp.mean(x * x, axis=-1, keepdims=True) + EPS) * n1_ref[...]
    hb = (h * (1.0 + sc1) + sh1).astype(BF16)

    cos = cos_ref[...]
    sa = sa_ref[...]
    sb = sb_ref[...]
    gm = gm_ref[...]

    def rope(t):
        return t * cos + pltpu.roll(t, LANES - 16, 1) * sa + pltpu.roll(t, 16, 1) * sb

    def headnorm(t, w):
        hi, lo = _split_bf16(t * t)
        ss = _dot(hi, gm) + _dot(lo, gm)
        return t * lax.rsqrt(ss * (1.0 / HEAD_DIM) + EPS) * w

    pa = _dot(hb, w_ref[:, 0:1024])
    for c in range(2):
        qa_ref[:, c * LANES:(c + 1) * LANES] = (rope(pa[:, c * LANES:(c + 1) * LANES]) * ATTN_SCALE).astype(BF16)
    ka_ref[...] = rope(pa[:, 256:384]).astype(BF16)
    va_ref[...] = pa[:, 384:512].astype(BF16)
    qn = qn_ref[...]
    kn = kn_ref[...]
    for c in range(2):
        t = headnorm(pa[:, 512 + c * LANES:512 + (c + 1) * LANES], qn)
        qb_ref[:, c * LANES:(c + 1) * LANES] = (rope(t) * ATTN_SCALE).astype(BF16)
    kb_ref[...] = rope(headnorm(pa[:, 768:896], kn)).astype(BF16)
    vb_ref[...] = pa[:, 896:1024].astype(BF16)

    hq_ref[...] = _silu(_dot(hb, w_ref[:, 1024:1536]))
    for d in range(2):
        z = _dot(hb, w_ref[:, 1536 + 512 * d:2048 + 512 * d])
        lb = lb_ref[d:d + 1, :]
        e = jnp.exp(-jnp.abs(z))
        r = 1.0 / (1.0 + e)
        er = e * r
        pos = z >= 0.0
        sig = jnp.where(pos, r, er)
        sig_neg = jnp.where(pos, er, r)
        f = lb + (1.0 - lb) * sig
        hg_ref[d] = jnp.log(jnp.maximum(f, TINY))
        hk_ref[d] = (1.0 - lb) * sig_neg
    hv_ref[...] = _dot(hb, w_ref[:, 2560:3072])
    hgate_ref[...] = _silu(_dot(hb, w_ref[:, 3072:3584]))


def _in_proj(x_all, mods_l, n1, w_in_b, cos_t, sa_t, sb_t, qn, kn, lb, gm):
    nblk = T_ALL // ROW_BLK
    row = lambda w: pl.BlockSpec((ROW_BLK, w), lambda i: (i, 0))
    full = lambda a: pl.BlockSpec(a.shape, lambda i: (0,) * a.ndim)
    tab = pl.BlockSpec((ROW_BLK, LANES), lambda i: (jnp.where(i < N_LAT // ROW_BLK, i % (SEQ // ROW_BLK), SEQ // ROW_BLK), 0))
    two = pl.BlockSpec((2, ROW_BLK, 512), lambda i: (0, i, 0))
    sd = lambda w, dt: jax.ShapeDtypeStruct((T_ALL, w), dt)
    return pl.pallas_call(
        _in_kernel,
        grid=(nblk,),
        in_specs=[row(D_MODEL), full(mods_l), full(n1), full(w_in_b), tab, tab, tab, full(qn), full(kn), full(lb), full(gm)],
        out_specs=[row(256), row(128), row(128), row(256), row(128), row(128), row(512), two, two, row(512), row(512)],
        out_shape=[sd(256, BF16), sd(128, BF16), sd(128, BF16), sd(256, BF16), sd(128, BF16), sd(128, BF16),
                   sd(512, F32), jax.ShapeDtypeStruct((2, T_ALL, 512), F32), jax.ShapeDtypeStruct((2, T_ALL, 512), F32),
                   sd(512, F32), sd(512, F32)],
        compiler_params=_cparams(("arbitrary",)),
        name="in_proj",
    )(x_all, mods_l, n1, w_in_b, cos_t, sa_t, sb_t, qn, kn, lb, gm)


def _attn_core(q_ref, parts, sink_ref, o_ref, tq):
    lane = lax.broadcasted_iota(jnp.int32, (tq, LANES), 1)
    outs = []
    for h in range(4):
        kvh = h // 2
        q128 = q_ref[:, kvh * LANES:(kvh + 1) * LANES].astype(F32)
        if (h % 2) != kvh:
            q128 = pltpu.roll(q128, HEAD_DIM, 1)
        qm = jnp.where((lane >= kvh * HEAD_DIM) & (lane < (kvh + 1) * HEAD_DIM), q128, 0.0).astype(BF16)
        scores = []
        for (k, _, mask) in parts:
            s = _dot_nt(qm, k)
            if mask is not None:
                s = jnp.where(mask, s, NEG_INF)
            scores.append(s)
        m = scores[0].max(axis=-1, keepdims=True)
        for s in scores[1:]:
            m = jnp.maximum(m, s.max(axis=-1, keepdims=True))
        if sink_ref is not None:
            sk = sink_ref[:, h:h + 1]
            m = jnp.maximum(m, sk)
            den = jnp.exp(sk - m)
        else:
            den = jnp.zeros_like(m)
        acc = jnp.zeros((tq, LANES), F32)
        for s, (_, v, _) in zip(scores, parts):
            p = jnp.exp(s - m)
            den = den + p.sum(axis=-1, keepdims=True)
            acc = acc + _dot(p.astype(BF16), v)
        outs.append(acc / den)
    for c in range(2):
        a, b = outs[2 * c], outs[2 * c + 1]
        if c == 0:
            b = pltpu.roll(b, HEAD_DIM, 1)
        else:
            a = pltpu.roll(a, HEAD_DIM, 1)
        o_ref[:, c * LANES:(c + 1) * LANES] = jnp.where(lane < HEAD_DIM, a, b).astype(BF16)


A_TQ = 512


def _attn_a_kernel(q_ref, kp_ref, kc_ref, kn_ref, vp_ref, vc_ref, vn_ref, kx_ref, vx_ref, sink_ref, o_ref):
    n = pl.program_id(1)
    qi = lax.broadcasted_iota(jnp.int32, (A_TQ, WINDOW), 0)
    kj = lax.broadcasted_iota(jnp.int32, (A_TQ, WINDOW), 1)
    mask_prev = (kj >= qi) & (n > 0)
    mask_next = (kj <= qi - (A_TQ - WINDOW)) & (n < SEQ // A_TQ - 1)
    qc = lax.broadcasted_iota(jnp.int32, (A_TQ, A_TQ), 0)
    kc = lax.broadcasted_iota(jnp.int32, (A_TQ, A_TQ), 1)
    mask_cur = jnp.abs(qc - kc) <= WINDOW
    parts = [(kp_ref[...], vp_ref[...], mask_prev), (kc_ref[...], vc_ref[...], mask_cur),
             (kn_ref[...], vn_ref[...], mask_next), (kx_ref[...], vx_ref[...], None)]
    _attn_core(q_ref, parts, sink_ref, o_ref, A_TQ)


def _attn_a_latent(qa, ka, va, sink):
    nq = SEQ // A_TQ
    nb = SEQ // WINDOW
    per = A_TQ // WINDOW
    q_spec = pl.BlockSpec((A_TQ, 256), lambda b, n: (b * nq + n, 0))
    prev = pl.BlockSpec((WINDOW, LANES), lambda b, n: (b * nb + jnp.maximum(n * per - 1, 0), 0))
    cur = pl.BlockSpec((A_TQ, LANES), lambda b, n: (b * nq + n, 0))
    nxt = pl.BlockSpec((WINDOW, LANES), lambda b, n: (b * nb + jnp.minimum((n + 1) * per, nb - 1), 0))
    ctx = pl.BlockSpec((CTX_LEN, LANES), lambda b, n: (N_LAT // CTX_LEN + b, 0))
    return pl.pallas_call(
        _attn_a_kernel,
        grid=(BATCH, nq),
        in_specs=[q_spec, prev, cur, nxt, prev, cur, nxt, ctx, ctx, pl.BlockSpec((1, LANES), lambda b, n: (0, 0))],
        out_specs=pl.BlockSpec((A_TQ, 256), lambda b, n: (b * nq + n, 0)),
        out_shape=jax.ShapeDtypeStruct((N_LAT, 256), BF16),
        compiler_params=_cparams(("arbitrary", "arbitrary")),
        name="attn_window",
    )(qa, ka, ka, ka, va, va, va, ka, va, sink)


B_TQ = 256


def _attn_b_kernel(q_ref, kl_ref, vl_ref, kx_ref, vx_ref, o_ref):
    parts = [(kl_ref[...], vl_ref[...], None), (kx_ref[...], vx_ref[...], None)]
    _attn_core(q_ref, parts, None, o_ref, B_TQ)


def _attn_b_latent(qb, kb, vb):
    nq = SEQ // B_TQ
    lat = pl.BlockSpec((SEQ, LANES), lambda b, j: (b, 0))
    ctx = pl.BlockSpec((CTX_LEN, LANES), lambda b, j: (N_LAT // CTX_LEN + b, 0))
    return pl.pallas_call(
        _attn_b_kernel,
        grid=(BATCH, nq),
        in_specs=[pl.BlockSpec((B_TQ, 256), lambda b, j: (b * nq + j, 0)), lat, lat, ctx, ctx],
        out_specs=pl.BlockSpec((B_TQ, 256), lambda b, j: (b * nq + j, 0)),
        out_shape=jax.ShapeDtypeStruct((N_LAT, 256), BF16),
        compiler_params=_cparams(("arbitrary", "arbitrary")),
        name="attn_full",
    )(qb, kb, vb, kb, vb)


def _attn_ctx_sink_kernel(q_ref, kx_ref, vx_ref, sink_ref, o_ref):
    _attn_core(q_ref, [(kx_ref[...], vx_ref[...], None)], sink_ref, o_ref, CTX_LEN)


def _attn_ctx_kernel(q_ref, kx_ref, vx_ref, o_ref):
    _attn_core(q_ref, [(kx_ref[...], vx_ref[...], None)], None, o_ref, CTX_LEN)


def _attn_ctx(q, k, v, sink):
    blk = lambda w: pl.BlockSpec((CTX_LEN, w), lambda b: (N_LAT // CTX_LEN + b, 0))
    in_specs = [blk(256), blk(LANES), blk(LANES)]
    args = [q, k, v]
    if sink is not None:
        in_specs.append(pl.BlockSpec((1, LANES), lambda b: (0, 0)))
        args.append(sink)
    return pl.pallas_call(
        _attn_ctx_sink_kernel if sink is not None else _attn_ctx_kernel,
        grid=(BATCH,),
        in_specs=in_specs,
        out_specs=pl.BlockSpec((CTX_LEN, 256), lambda b: (b, 0)),
        out_shape=jax.ShapeDtypeStruct((N_CTX, 256), BF16),
        compiler_params=_cparams(("arbitrary",)),
        name="attn_ctx_sink" if sink is not None else "attn_ctx",
    )(*args)


def _scan_constants():
    c = SCAN_CHUNK
    t = np.arange(c)[:, None]
    u = np.arange(c)[None, :]
    mats = [(u <= t)]
    masks = []
    for lvl in range(SCAN_LEVELS):
        m = 2 ** lvl
        second = ((t // m) % 2 == 1)
        end_first = (t // (2 * m)) * 2 * m + m - 1
        mats.append((second & (u > end_first) & (u <= t)) | ((~second) & (u > t) & (u <= end_first)))
        s = u
        masks.append(second & ((s // m) % 2 == 0) & ((s // (2 * m)) == (t // (2 * m))))
    fwd = np.concatenate(mats, axis=0).astype(np.float32)
    fmask = np.stack(masks).astype(np.float32)
    bwd = fwd.reshape(SCAN_MATS, c, c)[:, ::-1, ::-1].reshape(SCAN_MATS * c, c)
    bmask = fmask[:, ::-1, ::-1]
    both = np.stack([fwd, bwd])
    return np.concatenate([both, both], axis=2), np.stack([fmask, bmask])


def _scan_chunk(d, q_ref, k_ref, g_ref, v_ref, mall_ref, mask_ref, o_ref, st_ref):
    c = SCAN_CHUNK
    lf = g_ref[0]
    hi, lo = _split_bf16(lf)
    sums = _dot(mall_ref[d], jnp.concatenate([hi, lo], axis=0))
    ex = jnp.exp(sums)
    total = jnp.sum(lf, axis=0, keepdims=True)
    ex_after = jnp.exp(total - sums[0:c])
    tot = jnp.exp(total)
    ones = jnp.ones((c, c), BF16)
    for h in range(C_HEADS):
        sl = slice(h * C_KDIM, (h + 1) * C_KDIM)
        q = q_ref[:, sl]
        k = k_ref[0, :, sl]
        v = v_ref[:, sl]
        vb = v.astype(BF16)
        exh = lambda idx: ex[idx * c:(idx + 1) * c, sl]
        st = st_ref[d, h]
        inter = _dot_nt((q * exh(0)).astype(BF16), st.astype(BF16))
        att = jnp.zeros((c, c), F32)
        for lvl in range(SCAN_LEVELS):
            e = exh(1 + lvl)
            pr = _dot_nt((q * e).astype(BF16), (k * e).astype(BF16))
            att = att + mask_ref[d, lvl] * pr
        diag = _dot((q * k).astype(BF16), ones)
        o_ref[:, sl] = inter + _dot(att.astype(BF16), vb) + diag * v
        khat = (k * ex_after[:, sl]).astype(BF16)
        st_ref[d, h] = st * tot[:, sl] + _dot(v.T.astype(BF16), khat)


def _scan_kernel(qf_ref, kf_ref, gf_ref, vf_ref, qb_ref, kb_ref, gb_ref, vb_ref, mall_ref, mask_ref,
                 of_ref, ob_ref, st_ref):
    @pl.when(pl.program_id(1) == 0)
    def _():
        st_ref[...] = jnp.zeros_like(st_ref)

    _scan_chunk(0, qf_ref, kf_ref, gf_ref, vf_ref, mall_ref, mask_ref, of_ref, st_ref)
    _scan_chunk(1, qb_ref, kb_ref, gb_ref, vb_ref, mall_ref, mask_ref, ob_ref, st_ref)


def _hgrn_scan(hq, hk, hg, hv, mall, masks):
    c = SCAN_CHUNK
    n_ctx_chunks = CTX_LEN // c
    n_lat_chunks = SEQ // c
    steps = n_ctx_chunks + n_lat_chunks

    def rb(d):
        def index(b, j):
            ctx_blk = N_LAT // c + n_ctx_chunks * b + (j if d == 0 else n_ctx_chunks - 1 - j)
            jl = j - n_ctx_chunks
            lat_blk = b * n_lat_chunks + (jl if d == 0 else n_lat_chunks - 1 - jl)
            return jnp.where(j < n_ctx_chunks, ctx_blk, lat_blk)
        return index

    row = lambda d: pl.BlockSpec((c, 512), lambda b, j: (rb(d)(b, j), 0))
    drow = lambda d: pl.BlockSpec((1, c, 512), lambda b, j: (d, rb(d)(b, j), 0))
    full = lambda a: pl.BlockSpec(a.shape, lambda b, j: (0,) * a.ndim)
    o_sd = jax.ShapeDtypeStruct((T_ALL, 512), F32)
    return pl.pallas_call(
        _scan_kernel,
        grid=(BATCH, steps),
        in_specs=[row(0), drow(0), drow(0), row(0), row(1), drow(1), drow(1), row(1), full(mall), full(masks)],
        out_specs=[row(0), row(1)],
        out_shape=[o_sd, o_sd],
        scratch_shapes=[pltpu.VMEM((2, C_HEADS, C_VDIM, C_KDIM), F32)],
        compiler_params=_cparams(("arbitrary", "arbitrary")),
        name="hgrn_scan",
    )(hq, hk, hg, hv, hq, hk, hg, hv, mall, masks)


def _out_kernel(ya_ref, yb_ref, of_ref, ob_ref, gate_ref, x_ref, mod_ref, gw_ref, w_ref, n2_ref,
                rwh_ref, rwl_ref, rb_ref, xo_ref, h2_ref, ti_ref, tg_ref):
    i = pl.program_id(0)
    midx = jnp.minimum(i // (SEQ // ROW_BLK), BATCH)
    mod = mod_ref[pl.ds(midx, 1), :]
    g1 = mod[:, 2 * D_MODEL:3 * D_MODEL]
    sh2 = mod[:, 3 * D_MODEL:4 * D_MODEL]
    sc2 = mod[:, 4 * D_MODEL:5 * D_MODEL]

    o = of_ref[...] + ob_ref[...]
    y = _dot(ya_ref[...], w_ref[0:256, :]) + _dot(yb_ref[...], w_ref[256:512, :])
    for h in range(C_HEADS):
        sl = slice(h * C_VDIM, (h + 1) * C_VDIM)
        oh = o[:, sl]
        yn = oh * lax.rsqrt(jnp.mean(oh * oh, axis=-1, keepdims=True) + EPS) * gw_ref[...]
        yc = (yn * gate_ref[:, sl]).astype(BF16)
        y = y + _dot(yc, w_ref[512 + h * C_VDIM:512 + (h + 1) * C_VDIM, :])
    x = x_ref[...] + g1 * y
    xo_ref[...] = x
    h2 = x * lax.rsqrt(jnp.mean(x * x, axis=-1, keepdims=True) + EPS) * n2_ref[...]
    h2 = h2 * (1.0 + sc2) + sh2
    _store_token_tiles(h2_ref, _pack_bf16_pairs(h2), PACKED_TILE_ROWS)

    hi, lo = _split_bf16(h2)
    rwh = rwh_ref[...]
    logits = _dot(hi, rwh) + _dot(hi, rwl_ref[...]) + _dot(lo, rwh) + rb_ref[...]
    lane = lax.broadcasted_iota(jnp.int32, (ROW_BLK, LANES), 1).astype(F32)
    ti = jnp.zeros((ROW_BLK, LANES), F32)
    ex = jnp.zeros((ROW_BLK, LANES), F32)
    den = jnp.zeros((ROW_BLK, 1), F32)
    top = None
    for k in range(TOP_K):
        m = logits.max(axis=-1, keepdims=True)
        idx = jnp.min(jnp.where(logits == m, lane, float(LANES)), axis=-1, keepdims=True)
        if top is None:
            top = m
        e = jnp.exp(m - top)
        den = den + e
        ti = jnp.where(lane == float(k), idx, ti)
        ex = jnp.where(lane == float(k), e, ex)
        logits = jnp.where(lane == idx, -3.0e38, logits)
    ti_ref[...] = ti.astype(jnp.int32)
    tg_ref[...] = ex / den


def _out_proj(n_rows, ya, yb, o_f, o_b, gate, x_all, mods_l, gw, w_out_b, n2, rwh, rwl, rb):
    nblk = n_rows // ROW_BLK
    row = lambda w: pl.BlockSpec((ROW_BLK, w), lambda i: (i, 0))
    full = lambda a: pl.BlockSpec(a.shape, lambda i: (0,) * a.ndim)
    sd = lambda w, dt: jax.ShapeDtypeStruct((n_rows, w), dt)
    return pl.pallas_call(
        _out_kernel,
        grid=(nblk,),
        in_specs=[row(256), row(256), row(512), row(512),
                  row(512), row(D_MODEL), full(mods_l), full(gw), full(w_out_b), full(n2),
                  full(rwh), full(rwl), full(rb)],
        out_specs=[row(D_MODEL), pl.BlockSpec((ROW_BLK * PACKED_TILE_ROWS, LANES), lambda i: (i, 0)),
                   row(LANES), row(LANES)],
        out_shape=[sd(D_MODEL, F32), jax.ShapeDtypeStruct((n_rows * PACKED_TILE_ROWS, LANES), jnp.uint32),
                   sd(LANES, jnp.int32), sd(LANES, F32)],
        compiler_params=_cparams(("arbitrary",)),
        name="out_proj_router",
    )(ya, yb, o_f, o_b, gate, x_all, mods_l, gw, w_out_b, n2, rwh, rwl, rb)


def _rank_kernel(ti_ref, tri_ref, rank_ref, cnt_ref, carry_ref):
    i = pl.program_id(0)

    @pl.when(i == 0)
    def _():
        carry_ref[...] = jnp.zeros_like(carry_ref)

    ti = ti_ref[...]
    lane = lax.broadcasted_iota(jnp.int32, (ROW_BLK, LANES), 1)
    onehots = [(lane == ti[:, k:k + 1]).astype(F32) for k in range(TOP_K)]
    tot = onehots[0] + onehots[1] + onehots[2] + onehots[3]
    before = _dot(tri_ref[...], tot.astype(BF16)) + carry_ref[0:1, :]
    r = jnp.zeros((ROW_BLK, LANES), F32)
    for k in range(TOP_K):
        rk = jnp.sum(onehots[k] * before, axis=-1, keepdims=True)
        r = jnp.where(lane == k, rk, r)
    rank_ref[...] = r.astype(jnp.int32)
    carry = carry_ref[0:1, :] + jnp.sum(tot, axis=0, keepdims=True)
    carry_ref[0:1, :] = carry
    cnt_ref[...] = jnp.broadcast_to(carry, (8, LANES))


def _expert_ranks(top_i, tri):
    n_rows = top_i.shape[0]
    return pl.pallas_call(
        _rank_kernel,
        grid=(n_rows // ROW_BLK,),
        in_specs=[pl.BlockSpec((ROW_BLK, LANES), lambda i: (i, 0)),
                  pl.BlockSpec((ROW_BLK, ROW_BLK), lambda i: (0, 0))],
        out_specs=[pl.BlockSpec((ROW_BLK, LANES), lambda i: (i, 0)),
                   pl.BlockSpec((8, LANES), lambda i: (0, 0))],
        out_shape=[jax.ShapeDtypeStruct((n_rows, LANES), jnp.int32), jax.ShapeDtypeStruct((8, LANES), F32)],
        scratch_shapes=[pltpu.VMEM((8, LANES), F32)],
        compiler_params=_cparams(("arbitrary",)),
        name="expert_ranks",
    )(top_i, tri)


DEINT = 256


SCATTER_TOKENS = 256


def _dispatch_kernel(dst_ref, h_ref, zeros_ref, xs_ref, sem):
    del zeros_ref
    n = PACKED_TILE_ROWS

    def copy(t, k, dst_row):
        return pltpu.make_async_copy(h_ref.at[pl.ds(t * n, n), :],
                                     xs_ref.at[pl.ds(pl.multiple_of(dst_row, n), n), :], sem)

    for t in range(SCATTER_TOKENS):
        for k in range(TOP_K):
            copy(t, k, dst_ref[0, 0, t * TOP_K + k]).start(priority=k % 2)
    for t in range(SCATTER_TOKENS):
        for k in range(TOP_K):
            copy(t, k, 0).wait()


def _dispatch(h2t, dst, p_rows):
    nblk = dst.shape[0]
    shape = (p_rows * PACKED_TILE_ROWS, LANES)
    return pl.pallas_call(
        _dispatch_kernel,
        grid=(nblk,),
        in_specs=[pl.BlockSpec((1, 1, SCATTER_TOKENS * TOP_K), lambda i: (i, 0, 0), memory_space=pltpu.SMEM),
                  pl.BlockSpec((SCATTER_TOKENS * PACKED_TILE_ROWS, LANES), lambda i: (i, 0)),
                  pl.BlockSpec(memory_space=pl.ANY)],
        out_specs=pl.BlockSpec(memory_space=pl.ANY),
        out_shape=jax.ShapeDtypeStruct(shape, jnp.uint32),
        scratch_shapes=[pltpu.SemaphoreType.DMA(())],
        input_output_aliases={2: 0},
        compiler_params=_cparams(("arbitrary",)),
        name="moe_dispatch",
    )(dst, h2t, jnp.zeros(shape, jnp.uint32))


def _moe_kernel(blk_e_ref, grp_slot_ref, next_e_ref, x_ref, w1_ref, b1_ref, w2_ref, b2_ref, perm_ref, y_ref,
                wf1, wf2, w1p_ref, w2p_ref, wsem, *, layer):
    i = pl.program_id(0)

    def fetch(e, s):
        return (pltpu.make_async_copy(w1_ref.at[layer, e], wf1.at[s], wsem.at[s]),
                pltpu.make_async_copy(w2_ref.at[layer, e], wf2.at[s], wsem.at[s]))

    @pl.when(i == 0)
    def _():
        for c in fetch(blk_e_ref[0], 0):
            c.start()

    new_expert = (i == 0) | (blk_e_ref[i] != blk_e_ref[jnp.maximum(i - 1, 0)])

    @pl.when(new_expert)
    def _():
        s = grp_slot_ref[i]
        for c in fetch(0, s):
            c.wait()

        @pl.when(next_e_ref[i] >= 0)
        def _():
            for c in fetch(next_e_ref[i], 1 - s):
                c.start()

        perm = perm_ref[...]
        half = DEINT // 2
        for j in range(2 * D_FF // DEINT):
            r = _dot(wf1[s, :, j * DEINT:(j + 1) * DEINT].astype(BF16), perm)
            w1p_ref[:, j * half:(j + 1) * half] = r[:, 0:half].astype(BF16)
            w1p_ref[:, D_FF + j * half:D_FF + (j + 1) * half] = r[:, half:DEINT].astype(BF16)
        w2p_ref[...] = wf2[s].astype(BF16)

    x = _unpack_bf16_pairs(_load_token_tiles(x_ref, MOE_BLOCK, rows_per_token=PACKED_TILE_ROWS))
    u = _dot(x, w1p_ref[...]) + b1_ref[0, 0]
    u_glu = jnp.minimum(u[:, 0:D_FF], SWIGLU_LIMIT)
    u_lin = jnp.clip(u[:, D_FF:2 * D_FF], -SWIGLU_LIMIT, SWIGLU_LIMIT)
    act = u_glu * jax.nn.sigmoid(SWIGLU_ALPHA * u_glu) * (u_lin + 1.0)
    y = _dot(act.astype(BF16), w2p_ref[...]) + b2_ref[0, 0]
    _store_token_tiles(y_ref, y)


def _expert_mlp(layer, xs, blk_e, grp_slot, next_e, w1, b1p, w2, b2, perm):
    n_blocks = blk_e.shape[0]
    grid_spec = pltpu.PrefetchScalarGridSpec(
        num_scalar_prefetch=3,
        grid=(n_blocks,),
        in_specs=[pl.BlockSpec((MOE_BLOCK * PACKED_TILE_ROWS, LANES), lambda i, be, gs, ne: (i, 0)),
                  pl.BlockSpec(memory_space=pl.ANY),
                  pl.BlockSpec((1, 1, 1, 2 * D_FF), lambda i, be, gs, ne: (layer, be[i], 0, 0)),
                  pl.BlockSpec(memory_space=pl.ANY),
                  pl.BlockSpec((1, 1, 1, D_MODEL), lambda i, be, gs, ne: (layer, be[i], 0, 0)),
                  pl.BlockSpec((DEINT, DEINT), lambda i, be, gs, ne: (0, 0))],
        out_specs=pl.BlockSpec((MOE_BLOCK * TILE_ROWS, LANES), lambda i, be, gs, ne: (i, 0)),
        scratch_shapes=[pltpu.VMEM((2, D_MODEL, 2 * D_FF), F32), pltpu.VMEM((2, D_FF, D_MODEL), F32),
                        pltpu.VMEM((D_MODEL, 2 * D_FF), BF16), pltpu.VMEM((D_FF, D_MODEL), BF16),
                        pltpu.SemaphoreType.DMA((2,))],
    )
    return pl.pallas_call(
        functools.partial(_moe_kernel, layer=layer),
        grid_spec=grid_spec,
        out_shape=jax.ShapeDtypeStruct((n_blocks * MOE_BLOCK * TILE_ROWS, LANES), F32),
        compiler_params=_cparams(("arbitrary",)),
        name="expert_mlp",
    )(blk_e, grp_slot, next_e, xs, w1, b1p, w2, b2, perm)


def _combine_kernel(src_ref, src_next_ref, ys_ref, tg_ref, x_ref, mod_ref, nf_ref, o_ref, ybuf, sem, *, final):
    i = pl.program_id(0)
    last = pl.num_programs(0) - 1
    slot = i % 2
    n = TILE_ROWS
    n_copies = TOP_K * ROW_BLK
    rows = lambda start: pl.ds(pl.multiple_of(start, n), n)

    def gather(src_row, r, s):
        return pltpu.make_async_copy(ys_ref.at[rows(src_row), :], ybuf.at[s, rows(r * n), :], sem.at[s])

    @pl.when(i == 0)
    def _():
        for r in range(n_copies):
            gather(src_ref[0, 0, r], r, 0).start(priority=r % 2)

    for r in range(n_copies):
        gather(src_next_ref[0, 0, r], r, 1 - slot).start(priority=r % 2)
    for r in range(n_copies):
        gather(0, r, slot).wait()

    midx = jnp.minimum(i // (SEQ // ROW_BLK), BATCH)
    g2 = mod_ref[pl.ds(midx, 1), 5 * D_MODEL:6 * D_MODEL]
    tg = tg_ref[...]
    f = jnp.zeros((ROW_BLK, D_MODEL), F32)
    for k in range(TOP_K):
        cols = [ybuf[slot, pl.ds(k * ROW_BLK * n + c, ROW_BLK, stride=n), :] for c in range(n)]
        f = f + jnp.concatenate(cols, axis=1) * tg[:, k:k + 1]
    x = x_ref[...] + g2 * f
    if final:
        x = x * lax.rsqrt(jnp.mean(x * x, axis=-1, keepdims=True) + EPS) * nf_ref[...]
    o_ref[...] = x

    @pl.when(i == last)
    def _():
        for r in range(n_copies):
            gather(0, r, 1 - slot).wait()


def _combine(ys, src, tg, x_mid, mods_l, nf, final):
    n_rows = x_mid.shape[0]
    nblk = n_rows // ROW_BLK
    full = lambda a: pl.BlockSpec(a.shape, lambda i: (0,) * a.ndim)
    idx = lambda f: pl.BlockSpec((1, 1, TOP_K * ROW_BLK), f, memory_space=pltpu.SMEM)
    return pl.pallas_call(
        functools.partial(_combine_kernel, final=final),
        grid=(nblk,),
        in_specs=[idx(lambda i: (i, 0, 0)), idx(lambda i: (jnp.minimum(i + 1, nblk - 1), 0, 0)),
                  pl.BlockSpec(memory_space=pl.ANY),
                  pl.BlockSpec((ROW_BLK, LANES), lambda i: (i, 0)),
                  pl.BlockSpec((ROW_BLK, D_MODEL), lambda i: (i, 0)), full(mods_l), full(nf)],
        out_specs=pl.BlockSpec((ROW_BLK, D_MODEL), lambda i: (i, 0)),
        out_shape=jax.ShapeDtypeStruct((n_rows, D_MODEL), F32),
        scratch_shapes=[pltpu.VMEM((2, TOP_K * ROW_BLK * TILE_ROWS, LANES), F32), pltpu.SemaphoreType.DMA((2,))],
        compiler_params=_cparams(("arbitrary",)),
        name="moe_combine_final" if final else "moe_combine",
    )(src, src, ys, tg, x_mid, mods_l, nf)


def _moe(layer, h2t, top_i, top_g, x_mid, mods_l, w1, b1p, w2, b2, nf, tri, perm, final):
    n_rows = x_mid.shape[0]
    tk = n_rows * TOP_K
    n_blocks = -(-(tk + N_EXPERTS * (MOE_BLOCK - 1)) // MOE_BLOCK)
    p_rows = n_blocks * MOE_BLOCK

    rank, cnt = _expert_ranks(top_i, tri)
    counts = cnt[0, :N_EXPERTS].astype(jnp.int32)
    padded = (counts + MOE_BLOCK - 1) // MOE_BLOCK * MOE_BLOCK
    pend = jnp.cumsum(padded)
    pstart = pend - padded
    ti4 = top_i[:, :TOP_K]
    dest = pstart[ti4] + rank[:, :TOP_K]
    blk_start = jnp.arange(n_blocks, dtype=jnp.int32) * MOE_BLOCK
    blk_e = jnp.minimum(jnp.sum((pend[None, :] <= blk_start[:, None]).astype(jnp.int32), axis=1), N_EXPERTS - 1)
    change = jnp.concatenate([jnp.ones((1,), jnp.int32), (blk_e[1:] != blk_e[:-1]).astype(jnp.int32)])
    grp_slot = (jnp.cumsum(change) - 1) & 1
    experts = jnp.arange(N_EXPERTS, dtype=jnp.int32)
    present = jnp.any(blk_e[:, None] == experts[None, :], axis=0)
    later = present[None, :] & (experts[None, :] > blk_e[:, None])
    next_e = jnp.min(jnp.where(later, experts[None, :], N_EXPERTS), axis=1)
    next_e = jnp.where(next_e == N_EXPERTS, -1, next_e).astype(jnp.int32)

    n_tok_blk = n_rows // SCATTER_TOKENS
    dst_x = (dest * PACKED_TILE_ROWS).reshape(n_tok_blk, 1, SCATTER_TOKENS * TOP_K)
    xs = _dispatch(h2t, dst_x, p_rows)
    ys = _expert_mlp(layer, xs, blk_e, grp_slot.astype(jnp.int32), next_e, w1, b1p, w2, b2, perm)
    src_y = (dest * TILE_ROWS).reshape(n_rows // ROW_BLK, ROW_BLK, TOP_K).transpose(0, 2, 1)
    src_y = src_y.reshape(n_rows // ROW_BLK, 1, TOP_K * ROW_BLK)
    return _combine(ys, src_y, top_g, x_mid, mods_l, nf, final)


def _rope_tables():
    rows = SEQ // GRID_W
    row = jnp.repeat(jnp.arange(rows), GRID_W).astype(F32)
    col = jnp.tile(jnp.arange(GRID_W), rows).astype(F32)
    half = HEAD_DIM // 2
    inv = 1.0 / (ROPE_THETA ** (jnp.arange(0, half, 2, dtype=F32) / half))
    ang_r = row[:, None] * inv
    ang_c = col[:, None] * inv
    ang = jnp.concatenate([ang_r, ang_r, ang_c, ang_c], axis=-1)
    cos, sin = jnp.cos(ang), jnp.sin(ang)
    first = (jnp.arange(HEAD_DIM) % 32) < 16
    sa = jnp.where(first, -sin, 0.0)
    sb = jnp.where(first, 0.0, sin)
    ext = lambda t, fill: jnp.concatenate(
        [jnp.tile(t, (1, 2)), jnp.full((ROW_BLK, LANES), fill, F32)], axis=0)
    return ext(cos, 1.0), ext(sa, 0.0), ext(sb, 0.0)


def kernel(x, c, ctx, c_ctx, w_mod, b_mod, norm1_w, norm2_w, w_in, w_out, attn_sink, q_norm_w, k_norm_w,
           hgrn_lb, gate_norm_w, router_w, router_b, w1, b1, w2, b2, final_norm_w):
    cos_t, sa_t, sb_t = _rope_tables()
    lbs = jax.nn.softmax(hgrn_lb.astype(F32), axis=0)
    lbs = jnp.cumsum(lbs, axis=0) - lbs[0]
    mall_np, masks_np = _scan_constants()
    mall = jnp.asarray(mall_np, BF16)
    masks = jnp.asarray(masks_np, F32)
    gm = jnp.asarray(np.kron(np.eye(2), np.ones((HEAD_DIM, HEAD_DIM))), BF16)
    tri = jnp.asarray(np.tril(np.ones((ROW_BLK, ROW_BLK)), -1), BF16)

    c_all = jnp.concatenate([c, c_ctx[None, :], jnp.zeros((3, D_MODEL), F32)], axis=0)
    mods = _modulation(c_all, w_mod, b_mod)
    x_all = jnp.concatenate([x.reshape(N_LAT, D_MODEL), ctx.reshape(N_CTX, D_MODEL)], axis=0)
    nf = final_norm_w.reshape(1, D_MODEL)
    b1p = jnp.concatenate([b1[..., 0::2], b1[..., 1::2]], axis=-1).reshape(DEPTH, N_EXPERTS, 1, 2 * D_FF)
    b2r = b2.reshape(DEPTH, N_EXPERTS, 1, D_MODEL)
    src = np.concatenate([2 * np.arange(DEINT // 2), 2 * np.arange(DEINT // 2) + 1])
    perm = jnp.asarray(np.arange(DEINT)[:, None] == src[None, :], BF16)

    out = None
    for l in range(DEPTH):
        last = l == DEPTH - 1
        mods_l = mods[l]
        qn = jnp.tile(q_norm_w[l], 2).reshape(1, LANES)
        kn = jnp.tile(k_norm_w[l], 2).reshape(1, LANES)
        sink = jnp.zeros((1, LANES), F32).at[0, :A_HEADS].set(attn_sink[l])
        (qa, ka, va, qb, kb, vb, hq, hk, hg, hv, hgate) = _in_proj(
            x_all, mods_l, norm1_w[l].reshape(1, D_MODEL), w_in[l].astype(BF16), cos_t, sa_t, sb_t, qn, kn, lbs[l], gm)

        ya = _attn_a_latent(qa, ka, va, sink)
        yb = _attn_b_latent(qb, kb, vb)
        o_f, o_b = _hgrn_scan(hq, hk, hg, hv, mall, masks)
        if not last:
            ya = jnp.concatenate([ya, _attn_ctx(qa, ka, va, sink)], axis=0)
            yb = jnp.concatenate([yb, _attn_ctx(qb, kb, vb, None)], axis=0)
        n_rows = N_LAT if last else T_ALL

        rw = jnp.zeros((D_MODEL, LANES), F32).at[:, :N_EXPERTS].set(router_w[l])
        rwh, rwl = _split_bf16(rw)
        rb = jnp.full((1, LANES), NEG_INF, F32).at[0, :N_EXPERTS].set(router_b[l])
        x_mid, h2, top_i, top_g = _out_proj(
            n_rows, ya, yb, o_f, o_b, hgate, x_all, mods_l, gate_norm_w[l].reshape(1, C_VDIM),
            w_out[l].astype(BF16), norm2_w[l].reshape(1, D_MODEL), rwh, rwl, rb)

        x_all = _moe(l, h2, top_i, top_g, x_mid, mods_l, w1, b1p, w2, b2r, nf, tri, perm, last)
        out = x_all
    return out.reshape(BATCH, SEQ, D_MODEL)
```

```python
import functools

import numpy as np
import jax
import jax.numpy as jnp
from jax import lax
from jax.experimental import pallas as pl
from jax.experimental.pallas import tpu as pltpu

F32 = jnp.float32
BF16 = jnp.bfloat16

D_MODEL = 1024
BATCH = 4
SEQ = 4096
DEPTH = 2
GRID_W = 64
CTX_LEN = 256
HEAD_DIM = 64
ATTN_SCALE = HEAD_DIM ** -0.5
ROPE_THETA = 10000.0
A_HEADS = 4
B_HEADS = 4
WINDOW = 128
C_HEADS = 4
C_KDIM = 128
C_VDIM = 128
IN_WIDTH = 3584
MIX_WIDTH = 1024
N_EXPERTS = 32
TOP_K = 4
D_FF = 1024
SWIGLU_ALPHA = 1.702
SWIGLU_LIMIT = 7.0
MOE_BLOCK = 256
EPS = 1e-6
NEG_INF = -1e30
TINY = 1e-30

N_LAT = BATCH * SEQ
N_CTX = BATCH * CTX_LEN
T_ALL = N_LAT + N_CTX
ROW_BLK = 256
LANES = 128
SCAN_CHUNK = 128
SCAN_LEVELS = 7
SCAN_MATS = SCAN_LEVELS + 1
VMEM_LIMIT = 56 * 1024 * 1024


def _cparams(sem):
    return pltpu.CompilerParams(dimension_semantics=sem, vmem_limit_bytes=VMEM_LIMIT)


def _split_bf16(a):
    hi = a.astype(BF16)
    lo = (a - hi.astype(F32)).astype(BF16)
    return hi, lo


def _dot(a, b):
    return jnp.dot(a, b, preferred_element_type=F32)


def _dot_nt(a, b):
    return lax.dot_general(a, b, (((1,), (1,)), ((), ())), preferred_element_type=F32)


def _silu(a):
    return a * jax.nn.sigmoid(a)


TILE_ROWS = D_MODEL // LANES
PACKED_TILE_ROWS = TILE_ROWS // 2


def _store_token_tiles(ref, val, rows_per_token=TILE_ROWS):
    n = val.shape[0]
    for c in range(rows_per_token):
        ref[pl.ds(c, n, stride=rows_per_token), :] = val[:, c * LANES:(c + 1) * LANES]


def _load_token_tiles(ref, n, lead=None, rows_per_token=TILE_ROWS):
    cols = []
    for c in range(rows_per_token):
        if lead is None:
            cols.append(ref[pl.ds(c, n, stride=rows_per_token), :])
        else:
            cols.append(ref[lead, pl.ds(c, n, stride=rows_per_token), :])
    return cols


def _pack_bf16_pairs(v):
    half = v.shape[1] // 2
    hi = pltpu.bitcast(v[:, :half].astype(BF16).astype(F32), jnp.uint32)
    lo = pltpu.bitcast(v[:, half:].astype(BF16).astype(F32), jnp.uint32)
    return hi | (lo >> 16)


def _unpack_bf16_pairs(words):
    his = [pltpu.bitcast(w & jnp.uint32(0xFFFF0000), F32) for w in words]
    los = [pltpu.bitcast(w << 16, F32) for w in words]
    return jnp.concatenate(his + los, axis=1).astype(BF16)


def _mod_kernel(c_ref, w_ref, b_ref, o_ref):
    a = _silu(c_ref[...]).astype(BF16)
    o_ref[0] = _dot(a, w_ref[0].astype(BF16)) + b_ref[0]


def _modulation(c_all, w_mod, b_mod):
    tn = 1536
    return pl.pallas_call(
        _mod_kernel,
        grid=(DEPTH, 6 * D_MODEL // tn),
        in_specs=[pl.BlockSpec((8, D_MODEL), lambda l, j: (0, 0)),
                  pl.BlockSpec((1, D_MODEL, tn), lambda l, j: (l, 0, j)),
                  pl.BlockSpec((1, 1, tn), lambda l, j: (l, 0, j))],
        out_specs=pl.BlockSpec((1, 8, tn), lambda l, j: (l, 0, j)),
        out_shape=jax.ShapeDtypeStruct((DEPTH, 8, 6 * D_MODEL), F32),
        compiler_params=_cparams(("arbitrary", "arbitrary")),
        name="modulation",
    )(c_all, w_mod, b_mod.reshape(DEPTH, 1, 6 * D_MODEL))


def _in_kernel(x_ref, mod_ref, n1_ref, w_ref, cos_ref, sa_ref, sb_ref, qn_ref, kn_ref, lb_ref, gm_ref,
               qa_ref, ka_ref, va_ref, qb_ref, kb_ref, vb_ref, hq_ref, hk_ref, hg_ref, hv_ref, hgate_ref):
    i = pl.program_id(0)
    midx = jnp.minimum(i // (SEQ // ROW_BLK), BATCH)
    mod = mod_ref[pl.ds(midx, 1), :]
    sh1 = mod[:, 0:D_MODEL]
    sc1 = mod[:, D_MODEL:2 * D_MODEL]
    x = x_ref[...]
    h = x * lax.rsqrt(jnp.mean(x * x, axis=-1, keepdims=True) + EPS) * n1_ref[...]
    hb = (h * (1.0 + sc1) + sh1).astype(BF16)

    cos = cos_ref[...]
    sa = sa_ref[...]
    sb = sb_ref[...]
    gm = gm_ref[...]

    def rope(t):
        return t * cos + pltpu.roll(t, LANES - 16, 1) * sa + pltpu.roll(t, 16, 1) * sb

    def headnorm(t, w):
        hi, lo = _split_bf16(t * t)
        ss = _dot(hi, gm) + _dot(lo, gm)
        return t * lax.rsqrt(ss * (1.0 / HEAD_DIM) + EPS) * w

    pa = _dot(hb, w_ref[:, 0:1024])
    for c in range(2):
        qa_ref[:, c * LANES:(c + 1) * LANES] = (rope(pa[:, c * LANES:(c + 1) * LANES]) * ATTN_SCALE).astype(BF16)
    ka_ref[...] = rope(pa[:, 256:384]).astype(BF16)
    va_ref[...] = pa[:, 384:512].astype(BF16)
    qn = qn_ref[...]
    kn = kn_ref[...]
    for c in range(2):
        t = headnorm(pa[:, 512 + c * LANES:512 + (c + 1) * LANES], qn)
        qb_ref[:, c * LANES:(c + 1) * LANES] = (rope(t) * ATTN_SCALE).astype(BF16)
    kb_ref[...] = rope(headnorm(pa[:, 768:896], kn)).astype(BF16)
    vb_ref[...] = pa[:, 896:1024].astype(BF16)

    hq_ref[...] = _silu(_dot(hb, w_ref[:, 1024:1536]))
    for d in range(2):
        z = _dot(hb, w_ref[:, 1536 + 512 * d:2048 + 512 * d])
        lb = lb_ref[d:d + 1, :]
        e = jnp.exp(-jnp.abs(z))
        r = 1.0 / (1.0 + e)
        er = e * r
        pos = z >= 0.0
        sig = jnp.where(pos, r, er)
        sig_neg = jnp.where(pos, er, r)
        f = lb + (1.0 - lb) * sig
        hg_ref[d] = jnp.log(jnp.maximum(f, TINY))
        hk_ref[d] = (1.0 - lb) * sig_neg
    hv_ref[...] = _dot(hb, w_ref[:, 2560:3072])
    hgate_ref[...] = _silu(_dot(hb, w_ref[:, 3072:3584]))


def _in_proj(x_all, mods_l, n1, w_in_b, cos_t, sa_t, sb_t, qn, kn, lb, gm):
    nblk = T_ALL // ROW_BLK
    row = lambda w: pl.BlockSpec((ROW_BLK, w), lambda i: (i, 0))
    full = lambda a: pl.BlockSpec(a.shape, lambda i: (0,) * a.ndim)
    tab = pl.BlockSpec((ROW_BLK, LANES), lambda i: (jnp.where(i < N_LAT // ROW_BLK, i % (SEQ // ROW_BLK), SEQ // ROW_BLK), 0))
    two = pl.BlockSpec((2, ROW_BLK, 512), lambda i: (0, i, 0))
    sd = lambda w, dt: jax.ShapeDtypeStruct((T_ALL, w), dt)
    return pl.pallas_call(
        _in_kernel,
        grid=(nblk,),
        in_specs=[row(D_MODEL), full(mods_l), full(n1), full(w_in_b), tab, tab, tab, full(qn), full(kn), full(lb), full(gm)],
        out_specs=[row(256), row(128), row(128), row(256), row(128), row(128), row(512), two, two, row(512), row(512)],
        out_shape=[sd(256, BF16), sd(128, BF16), sd(128, BF16), sd(256, BF16), sd(128, BF16), sd(128, BF16),
                   sd(512, F32), jax.ShapeDtypeStruct((2, T_ALL, 512), F32), jax.ShapeDtypeStruct((2, T_ALL, 512), F32),
                   sd(512, F32), sd(512, F32)],
        compiler_params=_cparams(("arbitrary",)),
        name="in_proj",
    )(x_all, mods_l, n1, w_in_b, cos_t, sa_t, sb_t, qn, kn, lb, gm)


def _attn_core(q_ref, parts, sink_ref, o_ref, tq):
    lane = lax.broadcasted_iota(jnp.int32, (tq, LANES), 1)
    k = jnp.concatenate([p[0] for p in parts], axis=0)
    v = jnp.concatenate([p[1] for p in parts], axis=0)
    n_masked = max([j + 1 for j, p in enumerate(parts) if p[2] is not None], default=0)
    if n_masked:
        bias = jnp.concatenate(
            [jnp.where(p[2], 0.0, NEG_INF) if p[2] is not None else jnp.zeros((tq, p[0].shape[0]), F32)
             for p in parts[:n_masked]], axis=1)
        l_masked = bias.shape[1]
    outs = []
    for h in range(4):
        kvh = h // 2
        q128 = q_ref[:, kvh * LANES:(kvh + 1) * LANES].astype(F32)
        if (h % 2) != kvh:
            q128 = pltpu.roll(q128, HEAD_DIM, 1)
        qm = jnp.where((lane >= kvh * HEAD_DIM) & (lane < (kvh + 1) * HEAD_DIM), q128, 0.0).astype(BF16)
        s = _dot_nt(qm, k)
        if n_masked:
            s = jnp.concatenate([jnp.where(bias < 0.0, NEG_INF, s[:, :l_masked]), s[:, l_masked:]], axis=1)
        m = s.max(axis=-1, keepdims=True)
        if sink_ref is not None:
            sk = sink_ref[:, h:h + 1]
            m = jnp.maximum(m, sk)
            den = jnp.exp(sk - m)
        else:
            den = jnp.zeros_like(m)
        p = jnp.exp(s - m)
        den = den + p.sum(axis=-1, keepdims=True)
        outs.append(_dot(p.astype(BF16), v) / den)
    for c in range(2):
        a, b = outs[2 * c], outs[2 * c + 1]
        if c == 0:
            b = pltpu.roll(b, HEAD_DIM, 1)
        else:
            a = pltpu.roll(a, HEAD_DIM, 1)
        o_ref[:, c * LANES:(c + 1) * LANES] = jnp.where(lane < HEAD_DIM, a, b).astype(BF16)


A_TQ = 512


def _attn_a_kernel(q_ref, kp_ref, kc_ref, kn_ref, vp_ref, vc_ref, vn_ref, kx_ref, vx_ref, sink_ref, o_ref):
    n = pl.program_id(1)
    qi = lax.broadcasted_iota(jnp.int32, (A_TQ, WINDOW), 0)
    kj = lax.broadcasted_iota(jnp.int32, (A_TQ, WINDOW), 1)
    mask_prev = (kj >= qi) & (n > 0)
    mask_next = (kj <= qi - (A_TQ - WINDOW)) & (n < SEQ // A_TQ - 1)
    qc = lax.broadcasted_iota(jnp.int32, (A_TQ, A_TQ), 0)
    kc = lax.broadcasted_iota(jnp.int32, (A_TQ, A_TQ), 1)
    mask_cur = jnp.abs(qc - kc) <= WINDOW
    parts = [(kp_ref[...], vp_ref[...], mask_prev), (kc_ref[...], vc_ref[...], mask_cur),
             (kn_ref[...], vn_ref[...], mask_next), (kx_ref[...], vx_ref[...], None)]
    _attn_core(q_ref, parts, sink_ref, o_ref, A_TQ)


def _attn_a_latent(qa, ka, va, sink):
    nq = SEQ // A_TQ
    nb = SEQ // WINDOW
    per = A_TQ // WINDOW
    q_spec = pl.BlockSpec((A_TQ, 256), lambda b, n: (b * nq + n, 0))
    prev = pl.BlockSpec((WINDOW, LANES), lambda b, n: (b * nb + jnp.maximum(n * per - 1, 0), 0))
    cur = pl.BlockSpec((A_TQ, LANES), lambda b, n: (b * nq + n, 0))
    nxt = pl.BlockSpec((WINDOW, LANES), lambda b, n: (b * nb + jnp.minimum((n + 1) * per, nb - 1), 0))
    ctx = pl.BlockSpec((CTX_LEN, LANES), lambda b, n: (N_LAT // CTX_LEN + b, 0))
    return pl.pallas_call(
        _attn_a_kernel,
        grid=(BATCH, nq),
        in_specs=[q_spec, prev, cur, nxt, prev, cur, nxt, ctx, ctx, pl.BlockSpec((1, LANES), lambda b, n: (0, 0))],
        out_specs=pl.BlockSpec((A_TQ, 256), lambda b, n: (b * nq + n, 0)),
        out_shape=jax.ShapeDtypeStruct((N_LAT, 256), BF16),
        compiler_params=_cparams(("arbitrary", "arbitrary")),
        name="attn_window",
    )(qa, ka, ka, ka, va, va, va, ka, va, sink)


B_TQ = 512


def _attn_b_kernel(q_ref, kl_ref, vl_ref, kx_ref, vx_ref, o_ref):
    parts = [(kl_ref[...], vl_ref[...], None), (kx_ref[...], vx_ref[...], None)]
    _attn_core(q_ref, parts, None, o_ref, B_TQ)


def _attn_b_latent(qb, kb, vb):
    nq = SEQ // B_TQ
    lat = pl.BlockSpec((SEQ, LANES), lambda b, j: (b, 0))
    ctx = pl.BlockSpec((CTX_LEN, LANES), lambda b, j: (N_LAT // CTX_LEN + b, 0))
    return pl.pallas_call(
        _attn_b_kernel,
        grid=(BATCH, nq),
        in_specs=[pl.BlockSpec((B_TQ, 256), lambda b, j: (b * nq + j, 0)), lat, lat, ctx, ctx],
        out_specs=pl.BlockSpec((B_TQ, 256), lambda b, j: (b * nq + j, 0)),
        out_shape=jax.ShapeDtypeStruct((N_LAT, 256), BF16),
        compiler_params=_cparams(("arbitrary", "arbitrary")),
        name="attn_full",
    )(qb, kb, vb, kb, vb)


def _attn_ctx_sink_kernel(q_ref, kx_ref, vx_ref, sink_ref, o_ref):
    _attn_core(q_ref, [(kx_ref[...], vx_ref[...], None)], sink_ref, o_ref, CTX_LEN)


def _attn_ctx_kernel(q_ref, kx_ref, vx_ref, o_ref):
    _attn_core(q_ref, [(kx_ref[...], vx_ref[...], None)], None, o_ref, CTX_LEN)


def _attn_ctx(q, k, v, sink):
    blk = lambda w: pl.BlockSpec((CTX_LEN, w), lambda b: (N_LAT // CTX_LEN + b, 0))
    in_specs = [blk(256), blk(LANES), blk(LANES)]
    args = [q, k, v]
    if sink is not None:
        in_specs.append(pl.BlockSpec((1, LANES), lambda b: (0, 0)))
        args.append(sink)
    return pl.pallas_call(
        _attn_ctx_sink_kernel if sink is not None else _attn_ctx_kernel,
        grid=(BATCH,),
        in_specs=in_specs,
        out_specs=pl.BlockSpec((CTX_LEN, 256), lambda b: (b, 0)),
        out_shape=jax.ShapeDtypeStruct((N_CTX, 256), BF16),
        compiler_params=_cparams(("arbitrary",)),
        name="attn_ctx_sink" if sink is not None else "attn_ctx",
    )(*args)


def _scan_constants():
    c = SCAN_CHUNK
    t = np.arange(c)[:, None]
    u = np.arange(c)[None, :]
    mats = [(u <= t)]
    masks = []
    for lvl in range(SCAN_LEVELS):
        m = 2 ** lvl
        second = ((t // m) % 2 == 1)
        end_first = (t // (2 * m)) * 2 * m + m - 1
        mats.append((second & (u > end_first) & (u <= t)) | ((~second) & (u > t) & (u <= end_first)))
        s = u
        masks.append(second & ((s // m) % 2 == 0) & ((s // (2 * m)) == (t // (2 * m))))
    fwd = np.concatenate(mats, axis=0).astype(np.float32)
    fmask = np.stack(masks).astype(np.float32)
    bwd = fwd.reshape(SCAN_MATS, c, c)[:, ::-1, ::-1].reshape(SCAN_MATS * c, c)
    bmask = fmask[:, ::-1, ::-1]
    both = np.stack([fwd, bwd])
    return np.concatenate([both, both], axis=2), np.stack([fmask, bmask])


def _scan_chunk(d, q_ref, k_ref, g_ref, v_ref, mall_ref, mask_ref, o_ref, st_ref):
    c = SCAN_CHUNK
    lf = g_ref[0]
    hi, lo = _split_bf16(lf)
    sums = _dot(mall_ref[d], jnp.concatenate([hi, lo], axis=0))
    ex = jnp.exp(sums)
    total = jnp.sum(lf, axis=0, keepdims=True)
    ex_after = jnp.exp(total - sums[0:c])
    tot = jnp.exp(total)
    ones = jnp.ones((c, c), BF16)
    for h in range(C_HEADS):
        sl = slice(h * C_KDIM, (h + 1) * C_KDIM)
        q = q_ref[:, sl]
        k = k_ref[0, :, sl]
        v = v_ref[:, sl]
        vb = v.astype(BF16)
        exh = lambda idx: ex[idx * c:(idx + 1) * c, sl]
        st = st_ref[d, h]
        inter = _dot_nt((q * exh(0)).astype(BF16), st.astype(BF16))
        att = jnp.zeros((c, c), F32)
        for lvl in range(SCAN_LEVELS):
            e = exh(1 + lvl)
            pr = _dot_nt((q * e).astype(BF16), (k * e).astype(BF16))
            att = att + mask_ref[d, lvl] * pr
        diag = _dot((q * k).astype(BF16), ones)
        o_ref[:, sl] = inter + _dot(att.astype(BF16), vb) + diag * v
        khat = (k * ex_after[:, sl]).astype(BF16)
        st_ref[d, h] = st * tot[:, sl] + _dot(v.T.astype(BF16), khat)


def _scan_kernel(qf_ref, kf_ref, gf_ref, vf_ref, qb_ref, kb_ref, gb_ref, vb_ref, mall_ref, mask_ref,
                 of_ref, ob_ref, st_ref):
    @pl.when(pl.program_id(1) == 0)
    def _():
        st_ref[...] = jnp.zeros_like(st_ref)

    _scan_chunk(0, qf_ref, kf_ref, gf_ref, vf_ref, mall_ref, mask_ref, of_ref, st_ref)
    _scan_chunk(1, qb_ref, kb_ref, gb_ref, vb_ref, mall_ref, mask_ref, ob_ref, st_ref)


def _hgrn_scan(hq, hk, hg, hv, mall, masks):
    c = SCAN_CHUNK
    n_ctx_chunks = CTX_LEN // c
    n_lat_chunks = SEQ // c
    steps = n_ctx_chunks + n_lat_chunks

    def rb(d):
        def index(b, j):
            ctx_blk = N_LAT // c + n_ctx_chunks * b + (j if d == 0 else n_ctx_chunks - 1 - j)
            jl = j - n_ctx_chunks
            lat_blk = b * n_lat_chunks + (jl if d == 0 else n_lat_chunks - 1 - jl)
            return jnp.where(j < n_ctx_chunks, ctx_blk, lat_blk)
        return index

    row = lambda d: pl.BlockSpec((c, 512), lambda b, j: (rb(d)(b, j), 0))
    drow = lambda d: pl.BlockSpec((1, c, 512), lambda b, j: (d, rb(d)(b, j), 0))
    full = lambda a: pl.BlockSpec(a.shape, lambda b, j: (0,) * a.ndim)
    o_sd = jax.ShapeDtypeStruct((T_ALL, 512), F32)
    return pl.pallas_call(
        _scan_kernel,
        grid=(BATCH, steps),
        in_specs=[row(0), drow(0), drow(0), row(0), row(1), drow(1), drow(1), row(1), full(mall), full(masks)],
        out_specs=[row(0), row(1)],
        out_shape=[o_sd, o_sd],
        scratch_shapes=[pltpu.VMEM((2, C_HEADS, C_VDIM, C_KDIM), F32)],
        compiler_params=_cparams(("arbitrary", "arbitrary")),
        name="hgrn_scan",
    )(hq, hk, hg, hv, hq, hk, hg, hv, mall, masks)


def _out_kernel(ya_ref, yb_ref, of_ref, ob_ref, gate_ref, x_ref, mod_ref, gw_ref, w_ref, n2_ref,
                rwh_ref, rwl_ref, rb_ref, tri_ref, xo_ref, h2_ref, ti_ref, tg_ref, rank_ref, cnt_ref, carry_ref):
    i = pl.program_id(0)
    midx = jnp.minimum(i // (SEQ // ROW_BLK), BATCH)
    mod = mod_ref[pl.ds(midx, 1), :]
    g1 = mod[:, 2 * D_MODEL:3 * D_MODEL]
    sh2 = mod[:, 3 * D_MODEL:4 * D_MODEL]
    sc2 = mod[:, 4 * D_MODEL:5 * D_MODEL]

    o = of_ref[...] + ob_ref[...]
    y = _dot(ya_ref[...], w_ref[0:256, :]) + _dot(yb_ref[...], w_ref[256:512, :])
    for h in range(C_HEADS):
        sl = slice(h * C_VDIM, (h + 1) * C_VDIM)
        oh = o[:, sl]
        yn = oh * lax.rsqrt(jnp.mean(oh * oh, axis=-1, keepdims=True) + EPS) * gw_ref[...]
        yc = (yn * gate_ref[:, sl]).astype(BF16)
        y = y + _dot(yc, w_ref[512 + h * C_VDIM:512 + (h + 1) * C_VDIM, :])
    x = x_ref[...] + g1 * y
    xo_ref[...] = x
    h2 = x * lax.rsqrt(jnp.mean(x * x, axis=-1, keepdims=True) + EPS) * n2_ref[...]
    h2 = h2 * (1.0 + sc2) + sh2
    _store_token_tiles(h2_ref, _pack_bf16_pairs(h2), PACKED_TILE_ROWS)

    hi, lo = _split_bf16(h2)
    rwh = rwh_ref[...]
    logits = _dot(hi, rwh) + _dot(hi, rwl_ref[...]) + _dot(lo, rwh) + rb_ref[...]
    lane = lax.broadcasted_iota(jnp.int32, (ROW_BLK, LANES), 1).astype(F32)
    ti = jnp.zeros((ROW_BLK, LANES), F32)
    ex = jnp.zeros((ROW_BLK, LANES), F32)
    den = jnp.zeros((ROW_BLK, 1), F32)
    top = None
    onehots = []
    for k in range(TOP_K):
        m = logits.max(axis=-1, keepdims=True)
        idx = jnp.min(jnp.where(logits == m, lane, float(LANES)), axis=-1, keepdims=True)
        if top is None:
            top = m
        e = jnp.exp(m - top)
        den = den + e
        ti = jnp.where(lane == float(k), idx, ti)
        ex = jnp.where(lane == float(k), e, ex)
        chosen = lane == idx
        onehots.append(chosen.astype(F32))
        logits = jnp.where(chosen, -3.0e38, logits)
    ti_ref[...] = ti.astype(jnp.int32)
    tg_ref[...] = ex / den

    @pl.when(i == 0)
    def _():
        carry_ref[...] = jnp.zeros_like(carry_ref)

    tot = onehots[0] + onehots[1] + onehots[2] + onehots[3]
    before = _dot(tri_ref[...], tot.astype(BF16)) + carry_ref[0:1, :]
    r = jnp.zeros((ROW_BLK, LANES), F32)
    for k in range(TOP_K):
        rk = jnp.sum(onehots[k] * before, axis=-1, keepdims=True)
        r = jnp.where(lane == float(k), rk, r)
    rank_ref[...] = r.astype(jnp.int32)
    carry = carry_ref[0:1, :] + jnp.sum(tot, axis=0, keepdims=True)
    carry_ref[0:1, :] = carry
    cnt_ref[...] = jnp.broadcast_to(carry, (8, LANES))


def _out_proj(n_rows, ya, yb, o_f, o_b, gate, x_all, mods_l, gw, w_out_b, n2, rwh, rwl, rb, tri):
    nblk = n_rows // ROW_BLK
    row = lambda w: pl.BlockSpec((ROW_BLK, w), lambda i: (i, 0))
    full = lambda a: pl.BlockSpec(a.shape, lambda i: (0,) * a.ndim)
    sd = lambda w, dt: jax.ShapeDtypeStruct((n_rows, w), dt)
    return pl.pallas_call(
        _out_kernel,
        grid=(nblk,),
        in_specs=[row(256), row(256), row(512), row(512),
                  row(512), row(D_MODEL), full(mods_l), full(gw), full(w_out_b), full(n2),
                  full(rwh), full(rwl), full(rb), full(tri)],
        out_specs=[row(D_MODEL), pl.BlockSpec((ROW_BLK * PACKED_TILE_ROWS, LANES), lambda i: (i, 0)),
                   row(LANES), row(LANES), row(LANES), pl.BlockSpec((8, LANES), lambda i: (0, 0))],
        out_shape=[sd(D_MODEL, F32), jax.ShapeDtypeStruct((n_rows * PACKED_TILE_ROWS, LANES), jnp.uint32),
                   sd(LANES, jnp.int32), sd(LANES, F32), sd(LANES, jnp.int32),
                   jax.ShapeDtypeStruct((8, LANES), F32)],
        scratch_shapes=[pltpu.VMEM((8, LANES), F32)],
        compiler_params=_cparams(("arbitrary",)),
        name="out_proj_router",
    )(ya, yb, o_f, o_b, gate, x_all, mods_l, gw, w_out_b, n2, rwh, rwl, rb, tri)


DEINT = 256


SCATTER_TOKENS = 256


def _dispatch_kernel(pad_blk_ref, pad_new_ref, dst_ref, h_ref, xs_ref, zbuf, sem):
    n = PACKED_TILE_ROWS
    blk_rows = MOE_BLOCK * n

    def zero_fill(j):
        start = pl.multiple_of(pad_blk_ref[j] * blk_rows, blk_rows)
        return pltpu.make_async_copy(zbuf, xs_ref.at[pl.ds(start, blk_rows), :], sem)

    @pl.when(pl.program_id(0) == 0)
    def _():
        zbuf[...] = jnp.zeros_like(zbuf)
        for j in range(2 * N_EXPERTS):
            @pl.when(pad_new_ref[j] == 1)
            def _():
                zero_fill(j).start()
        for j in range(2 * N_EXPERTS):
            @pl.when(pad_new_ref[j] == 1)
            def _():
                zero_fill(j).wait()

    def copy(t, k, dst_row):
        return pltpu.make_async_copy(h_ref.at[pl.ds(t * n, n), :],
                                     xs_ref.at[pl.ds(pl.multiple_of(dst_row, n), n), :], sem)

    for t in range(SCATTER_TOKENS):
        for k in range(TOP_K):
            copy(t, k, dst_ref[0, 0, t * TOP_K + k]).start(priority=k % 2)
    for t in range(SCATTER_TOKENS):
        for k in range(TOP_K):
            copy(t, k, 0).wait()


def _dispatch(h2t, dst, pad_blk, pad_new, p_rows):
    nblk = dst.shape[0]
    shape = (p_rows * PACKED_TILE_ROWS, LANES)
    grid_spec = pltpu.PrefetchScalarGridSpec(
        num_scalar_prefetch=2,
        grid=(nblk,),
        in_specs=[pl.BlockSpec((1, 1, SCATTER_TOKENS * TOP_K), lambda i, pb, pn: (i, 0, 0), memory_space=pltpu.SMEM),
                  pl.BlockSpec((SCATTER_TOKENS * PACKED_TILE_ROWS, LANES), lambda i, pb, pn: (i, 0))],
        out_specs=pl.BlockSpec(memory_space=pl.ANY),
        scratch_shapes=[pltpu.VMEM((MOE_BLOCK * PACKED_TILE_ROWS, LANES), jnp.uint32), pltpu.SemaphoreType.DMA(())],
    )
    return pl.pallas_call(
        _dispatch_kernel,
        grid_spec=grid_spec,
        out_shape=jax.ShapeDtypeStruct(shape, jnp.uint32),
        compiler_params=_cparams(("arbitrary",)),
        name="moe_dispatch",
    )(pad_blk, pad_new, dst, h2t)


def _moe_kernel(blk_e_ref, grp_slot_ref, next_e_ref, x_ref, w1_ref, b1_ref, w2_ref, b2_ref, perm_ref, y_ref,
                wf1, wf2, w1p_ref, w2p_ref, wsem, *, layer):
    i = pl.program_id(0)

    def fetch(e, s):
        return (pltpu.make_async_copy(w1_ref.at[layer, e], wf1.at[s], wsem.at[s]),
                pltpu.make_async_copy(w2_ref.at[layer, e], wf2.at[s], wsem.at[s]))

    @pl.when(i == 0)
    def _():
        for c in fetch(blk_e_ref[0], 0):
            c.start()

    new_expert = (i == 0) | (blk_e_ref[i] != blk_e_ref[jnp.maximum(i - 1, 0)])

    @pl.when(new_expert)
    def _():
        s = grp_slot_ref[i]
        for c in fetch(0, s):
            c.wait()

        @pl.when(next_e_ref[i] >= 0)
        def _():
            for c in fetch(next_e_ref[i], 1 - s):
                c.start()

        perm = perm_ref[...]
        half = DEINT // 2
        for j in range(2 * D_FF // DEINT):
            r = _dot(wf1[s, :, j * DEINT:(j + 1) * DEINT].astype(BF16), perm)
            w1p_ref[:, j * half:(j + 1) * half] = r[:, 0:half].astype(BF16)
            w1p_ref[:, D_FF + j * half:D_FF + (j + 1) * half] = r[:, half:DEINT].astype(BF16)
        w2p_ref[...] = wf2[s].astype(BF16)

    x = _unpack_bf16_pairs(_load_token_tiles(x_ref, MOE_BLOCK, rows_per_token=PACKED_TILE_ROWS))
    u = _dot(x, w1p_ref[...]) + b1_ref[0, 0]
    u_glu = jnp.minimum(u[:, 0:D_FF], SWIGLU_LIMIT)
    u_lin = jnp.clip(u[:, D_FF:2 * D_FF], -SWIGLU_LIMIT, SWIGLU_LIMIT)
    act = u_glu * jax.nn.sigmoid(SWIGLU_ALPHA * u_glu) * (u_lin + 1.0)
    y = _dot(act.astype(BF16), w2p_ref[...]) + b2_ref[0, 0]
    _store_token_tiles(y_ref, y)


def _expert_mlp(layer, xs, blk_e, grp_slot, next_e, w1, b1p, w2, b2, perm):
    n_blocks = blk_e.shape[0]
    grid_spec = pltpu.PrefetchScalarGridSpec(
        num_scalar_prefetch=3,
        grid=(n_blocks,),
        in_specs=[pl.BlockSpec((MOE_BLOCK * PACKED_TILE_ROWS, LANES), lambda i, be, gs, ne: (i, 0)),
                  pl.BlockSpec(memory_space=pl.ANY),
                  pl.BlockSpec((1, 1, 1, 2 * D_FF), lambda i, be, gs, ne: (layer, be[i], 0, 0)),
                  pl.BlockSpec(memory_space=pl.ANY),
                  pl.BlockSpec((1, 1, 1, D_MODEL), lambda i, be, gs, ne: (layer, be[i], 0, 0)),
                  pl.BlockSpec((DEINT, DEINT), lambda i, be, gs, ne: (0, 0))],
        out_specs=pl.BlockSpec((MOE_BLOCK * TILE_ROWS, LANES), lambda i, be, gs, ne: (i, 0)),
        scratch_shapes=[pltpu.VMEM((2, D_MODEL, 2 * D_FF), F32), pltpu.VMEM((2, D_FF, D_MODEL), F32),
                        pltpu.VMEM((D_MODEL, 2 * D_FF), BF16), pltpu.VMEM((D_FF, D_MODEL), BF16),
                        pltpu.SemaphoreType.DMA((2,))],
    )
    return pl.pallas_call(
        functools.partial(_moe_kernel, layer=layer),
        grid_spec=grid_spec,
        out_shape=jax.ShapeDtypeStruct((n_blocks * MOE_BLOCK * TILE_ROWS, LANES), F32),
        compiler_params=_cparams(("arbitrary",)),
        name="expert_mlp",
    )(blk_e, grp_slot, next_e, xs, w1, b1p, w2, b2, perm)


def _combine_kernel(src_ref, src_next_ref, ys_ref, tg_ref, x_ref, mod_ref, nf_ref, o_ref, ybuf, sem, *, final):
    i = pl.program_id(0)
    last = pl.num_programs(0) - 1
    slot = i % 2
    n = TILE_ROWS
    n_copies = TOP_K * ROW_BLK
    rows = lambda start: pl.ds(pl.multiple_of(start, n), n)

    def gather(src_row, r, s):
        return pltpu.make_async_copy(ys_ref.at[rows(src_row), :], ybuf.at[s, rows(r * n), :], sem.at[s])

    @pl.when(i == 0)
    def _():
        for r in range(n_copies):
            gather(src_ref[0, 0, r], r, 0).start(priority=r % 2)

    for r in range(n_copies):
        gather(src_next_ref[0, 0, r], r, 1 - slot).start(priority=r % 2)
    for r in range(n_copies):
        gather(0, r, slot).wait()

    midx = jnp.minimum(i // (SEQ // ROW_BLK), BATCH)
    g2 = mod_ref[pl.ds(midx, 1), 5 * D_MODEL:6 * D_MODEL]
    tg = tg_ref[...]
    f = jnp.zeros((ROW_BLK, D_MODEL), F32)
    for k in range(TOP_K):
        cols = [ybuf[slot, pl.ds(k * ROW_BLK * n + c, ROW_BLK, stride=n), :] for c in range(n)]
        f = f + jnp.concatenate(cols, axis=1) * tg[:, k:k + 1]
    x = x_ref[...] + g2 * f
    if final:
        x = x * lax.rsqrt(jnp.mean(x * x, axis=-1, keepdims=True) + EPS) * nf_ref[...]
    o_ref[...] = x

    @pl.when(i == last)
    def _():
        for r in range(n_copies):
            gather(0, r, 1 - slot).wait()


def _combine(ys, src, tg, x_mid, mods_l, nf, final):
    n_rows = x_mid.shape[0]
    nblk = n_rows // ROW_BLK
    full = lambda a: pl.BlockSpec(a.shape, lambda i: (0,) * a.ndim)
    idx = lambda f: pl.BlockSpec((1, 1, TOP_K * ROW_BLK), f, memory_space=pltpu.SMEM)
    return pl.pallas_call(
        functools.partial(_combine_kernel, final=final),
        grid=(nblk,),
        in_specs=[idx(lambda i: (i, 0, 0)), idx(lambda i: (jnp.minimum(i + 1, nblk - 1), 0, 0)),
                  pl.BlockSpec(memory_space=pl.ANY),
                  pl.BlockSpec((ROW_BLK, LANES), lambda i: (i, 0)),
                  pl.BlockSpec((ROW_BLK, D_MODEL), lambda i: (i, 0)), full(mods_l), full(nf)],
        out_specs=pl.BlockSpec((ROW_BLK, D_MODEL), lambda i: (i, 0)),
        out_shape=jax.ShapeDtypeStruct((n_rows, D_MODEL), F32),
        scratch_shapes=[pltpu.VMEM((2, TOP_K * ROW_BLK * TILE_ROWS, LANES), F32), pltpu.SemaphoreType.DMA((2,))],
        compiler_params=_cparams(("arbitrary",)),
        name="moe_combine_final" if final else "moe_combine",
    )(src, src, ys, tg, x_mid, mods_l, nf)


def _moe(layer, h2t, top_i, top_g, rank, cnt, x_mid, mods_l, w1, b1p, w2, b2, nf, perm, final):
    n_rows = x_mid.shape[0]
    tk = n_rows * TOP_K
    n_blocks = -(-(tk + N_EXPERTS * (MOE_BLOCK - 1)) // MOE_BLOCK)
    p_rows = n_blocks * MOE_BLOCK

    counts = cnt[0, :N_EXPERTS].astype(jnp.int32)
    padded = (counts + MOE_BLOCK - 1) // MOE_BLOCK * MOE_BLOCK
    pend = jnp.cumsum(padded)
    pstart = pend - padded
    ti4 = top_i[:, :TOP_K]
    dest = pstart[ti4] + rank[:, :TOP_K]
    blk_start = jnp.arange(n_blocks, dtype=jnp.int32) * MOE_BLOCK
    blk_e = jnp.minimum(jnp.sum((pend[None, :] <= blk_start[:, None]).astype(jnp.int32), axis=1), N_EXPERTS - 1)
    change = jnp.concatenate([jnp.ones((1,), jnp.int32), (blk_e[1:] != blk_e[:-1]).astype(jnp.int32)])
    grp_slot = (jnp.cumsum(change) - 1) & 1
    experts = jnp.arange(N_EXPERTS, dtype=jnp.int32)
    present = jnp.any(blk_e[:, None] == experts[None, :], axis=0)
    later = present[None, :] & (experts[None, :] > blk_e[:, None])
    next_e = jnp.min(jnp.where(later, experts[None, :], N_EXPERTS), axis=1)
    next_e = jnp.where(next_e == N_EXPERTS, -1, next_e).astype(jnp.int32)

    n_tok_blk = n_rows // SCATTER_TOKENS
    dst_x = (dest * PACKED_TILE_ROWS).reshape(n_tok_blk, 1, SCATTER_TOKENS * TOP_K)
    last_blk = jnp.maximum(pend // MOE_BLOCK - 1, 0)
    tail_blk = jnp.minimum(pend[-1] // MOE_BLOCK + experts, n_blocks - 1)
    pad_blk = jnp.concatenate([last_blk, tail_blk]).astype(jnp.int32)
    pad_new = jnp.concatenate([jnp.ones((1,), jnp.int32), (pad_blk[1:] != pad_blk[:-1]).astype(jnp.int32)])
    xs = _dispatch(h2t, dst_x, pad_blk, pad_new, p_rows)
    ys = _expert_mlp(layer, xs, blk_e, grp_slot.astype(jnp.int32), next_e, w1, b1p, w2, b2, perm)
    src_y = (dest * TILE_ROWS).reshape(n_rows // ROW_BLK, ROW_BLK, TOP_K).transpose(0, 2, 1)
    src_y = src_y.reshape(n_rows // ROW_BLK, 1, TOP_K * ROW_BLK)
    return _combine(ys, src_y, top_g, x_mid, mods_l, nf, final)


def _rope_tables():
    rows = SEQ // GRID_W
    row = jnp.repeat(jnp.arange(rows), GRID_W).astype(F32)
    col = jnp.tile(jnp.arange(GRID_W), rows).astype(F32)
    half = HEAD_DIM // 2
    inv = 1.0 / (ROPE_THETA ** (jnp.arange(0, half, 2, dtype=F32) / half))
    ang_r = row[:, None] * inv
    ang_c = col[:, None] * inv
    ang = jnp.concatenate([ang_r, ang_r, ang_c, ang_c], axis=-1)
    cos, sin = jnp.cos(ang), jnp.sin(ang)
    first = (jnp.arange(HEAD_DIM) % 32) < 16
    sa = jnp.where(first, -sin, 0.0)
    sb = jnp.where(first, 0.0, sin)
    ext = lambda t, fill: jnp.concatenate(
        [jnp.tile(t, (1, 2)), jnp.full((ROW_BLK, LANES), fill, F32)], axis=0)
    return ext(cos, 1.0), ext(sa, 0.0), ext(sb, 0.0)


def kernel(x, c, ctx, c_ctx, w_mod, b_mod, norm1_w, norm2_w, w_in, w_out, attn_sink, q_norm_w, k_norm_w,
           hgrn_lb, gate_norm_w, router_w, router_b, w1, b1, w2, b2, final_norm_w):
    cos_t, sa_t, sb_t = _rope_tables()
    lbs = jax.nn.softmax(hgrn_lb.astype(F32), axis=0)
    lbs = jnp.cumsum(lbs, axis=0) - lbs[0]
    mall_np, masks_np = _scan_constants()
    mall = jnp.asarray(mall_np, BF16)
    masks = jnp.asarray(masks_np, F32)
    gm = jnp.asarray(np.kron(np.eye(2), np.ones((HEAD_DIM, HEAD_DIM))), BF16)
    tri = jnp.asarray(np.tril(np.ones((ROW_BLK, ROW_BLK)), -1), BF16)

    c_all = jnp.concatenate([c, c_ctx[None, :], jnp.zeros((3, D_MODEL), F32)], axis=0)
    mods = _modulation(c_all, w_mod, b_mod)
    x_all = jnp.concatenate([x.reshape(N_LAT, D_MODEL), ctx.reshape(N_CTX, D_MODEL)], axis=0)
    nf = final_norm_w.reshape(1, D_MODEL)
    b1p = jnp.concatenate([b1[..., 0::2], b1[..., 1::2]], axis=-1).reshape(DEPTH, N_EXPERTS, 1, 2 * D_FF)
    b2r = b2.reshape(DEPTH, N_EXPERTS, 1, D_MODEL)
    src = np.concatenate([2 * np.arange(DEINT // 2), 2 * np.arange(DEINT // 2) + 1])
    perm = jnp.asarray(np.arange(DEINT)[:, None] == src[None, :], BF16)

    out = None
    for l in range(DEPTH):
        last = l == DEPTH - 1
        mods_l = mods[l]
        qn = jnp.tile(q_norm_w[l], 2).reshape(1, LANES)
        kn = jnp.tile(k_norm_w[l], 2).reshape(1, LANES)
        sink = jnp.zeros((1, LANES), F32).at[0, :A_HEADS].set(attn_sink[l])
        (qa, ka, va, qb, kb, vb, hq, hk, hg, hv, hgate) = _in_proj(
            x_all, mods_l, norm1_w[l].reshape(1, D_MODEL), w_in[l].astype(BF16), cos_t, sa_t, sb_t, qn, kn, lbs[l], gm)

        ya = _attn_a_latent(qa, ka, va, sink)
        yb = _attn_b_latent(qb, kb, vb)
        o_f, o_b = _hgrn_scan(hq, hk, hg, hv, mall, masks)
        if not last:
            ya = jnp.concatenate([ya, _attn_ctx(qa, ka, va, sink)], axis=0)
            yb = jnp.concatenate([yb, _attn_ctx(qb, kb, vb, None)], axis=0)
        n_rows = N_LAT if last else T_ALL

        rw = jnp.zeros((D_MODEL, LANES), F32).at[:, :N_EXPERTS].set(router_w[l])
        rwh, rwl = _split_bf16(rw)
        rb = jnp.full((1, LANES), NEG_INF, F32).at[0, :N_EXPERTS].set(router_b[l])
        x_mid, h2, top_i, top_g, rank, cnt = _out_proj(
            n_rows, ya, yb, o_f, o_b, hgate, x_all, mods_l, gate_norm_w[l].reshape(1, C_VDIM),
            w_out[l].astype(BF16), norm2_w[l].reshape(1, D_MODEL), rwh, rwl, rb, tri)

        x_all = _moe(l, h2, top_i, top_g, rank, cnt, x_mid, mods_l, w1, b1p, w2, b2r, nf, perm, last)
        out = x_all
    return out.reshape(BATCH, SEQ, D_MODEL)
```

```python
import functools

import numpy as np
import jax
import jax.numpy as jnp
from jax import lax
from jax.experimental import pallas as pl
from jax.experimental.pallas import tpu as pltpu

F32 = jnp.float32
BF16 = jnp.bfloat16

D_MODEL = 1024
BATCH = 4
SEQ = 4096
DEPTH = 2
GRID_W = 64
CTX_LEN = 256
HEAD_DIM = 64
ATTN_SCALE = HEAD_DIM ** -0.5
ROPE_THETA = 10000.0
A_HEADS = 4
B_HEADS = 4
WINDOW = 128
C_HEADS = 4
C_KDIM = 128
C_VDIM = 128
IN_WIDTH = 3584
MIX_WIDTH = 1024
N_EXPERTS = 32
TOP_K = 4
D_FF = 1024
SWIGLU_ALPHA = 1.702
SWIGLU_LIMIT = 7.0
MOE_BLOCK = 256
EPS = 1e-6
NEG_INF = -1e30
TINY = 1e-30

N_LAT = BATCH * SEQ
N_CTX = BATCH * CTX_LEN
T_ALL = N_LAT + N_CTX
ROW_BLK = 256
LANES = 128
SCAN_CHUNK = 128
SCAN_LEVELS = 7
SCAN_MATS = SCAN_LEVELS + 1
VMEM_LIMIT = 56 * 1024 * 1024


def _cparams(sem):
    return pltpu.CompilerParams(dimension_semantics=sem, vmem_limit_bytes=VMEM_LIMIT)


def _split_bf16(a):
    hi = a.astype(BF16)
    lo = (a - hi.astype(F32)).astype(BF16)
    return hi, lo


def _dot(a, b):
    return jnp.dot(a, b, preferred_element_type=F32)


def _dot_nt(a, b):
    return lax.dot_general(a, b, (((1,), (1,)), ((), ())), preferred_element_type=F32)


def _silu(a):
    return a * jax.nn.sigmoid(a)


TILE_ROWS = D_MODEL // LANES
PACKED_TILE_ROWS = TILE_ROWS // 2


def _store_token_tiles(ref, val, rows_per_token=TILE_ROWS):
    n = val.shape[0]
    for c in range(rows_per_token):
        ref[pl.ds(c, n, stride=rows_per_token), :] = val[:, c * LANES:(c + 1) * LANES]


def _load_token_tiles(ref, n, lead=None, rows_per_token=TILE_ROWS):
    cols = []
    for c in range(rows_per_token):
        if lead is None:
            cols.append(ref[pl.ds(c, n, stride=rows_per_token), :])
        else:
            cols.append(ref[lead, pl.ds(c, n, stride=rows_per_token), :])
    return cols


def _pack_bf16_pairs(v):
    half = v.shape[1] // 2
    hi = pltpu.bitcast(v[:, :half].astype(BF16).astype(F32), jnp.uint32)
    lo = pltpu.bitcast(v[:, half:].astype(BF16).astype(F32), jnp.uint32)
    return hi | (lo >> 16)


def _unpack_bf16_pairs(words):
    his = [pltpu.bitcast(w & jnp.uint32(0xFFFF0000), F32) for w in words]
    los = [pltpu.bitcast(w << 16, F32) for w in words]
    return jnp.concatenate(his + los, axis=1).astype(BF16)


def _mod_kernel(c_ref, w_ref, b_ref, o_ref):
    a = _silu(c_ref[...]).astype(BF16)
    o_ref[0] = _dot(a, w_ref[0].astype(BF16)) + b_ref[0]


def _modulation(c_all, w_mod, b_mod):
    tn = 1536
    return pl.pallas_call(
        _mod_kernel,
        grid=(DEPTH, 6 * D_MODEL // tn),
        in_specs=[pl.BlockSpec((8, D_MODEL), lambda l, j: (0, 0)),
                  pl.BlockSpec((1, D_MODEL, tn), lambda l, j: (l, 0, j)),
                  pl.BlockSpec((1, 1, tn), lambda l, j: (l, 0, j))],
        out_specs=pl.BlockSpec((1, 8, tn), lambda l, j: (l, 0, j)),
        out_shape=jax.ShapeDtypeStruct((DEPTH, 8, 6 * D_MODEL), F32),
        compiler_params=_cparams(("arbitrary", "arbitrary")),
        name="modulation",
    )(c_all, w_mod, b_mod.reshape(DEPTH, 1, 6 * D_MODEL))


def _in_kernel(x_ref, mod_ref, n1_ref, w_ref, cos_ref, sa_ref, sb_ref, qn_ref, kn_ref, lb_ref, gm_ref,
               qa_ref, ka_ref, va_ref, qb_ref, kb_ref, vb_ref, hq_ref, hk_ref, hg_ref, hv_ref, hgate_ref):
    i = pl.program_id(0)
    midx = jnp.minimum(i // (SEQ // ROW_BLK), BATCH)
    mod = mod_ref[pl.ds(midx, 1), :]
    sh1 = mod[:, 0:D_MODEL]
    sc1 = mod[:, D_MODEL:2 * D_MODEL]
    x = x_ref[...]
    h = x * lax.rsqrt(jnp.mean(x * x, axis=-1, keepdims=True) + EPS) * n1_ref[...]
    hb = (h * (1.0 + sc1) + sh1).astype(BF16)

    cos = cos_ref[...]
    sa = sa_ref[...]
    sb = sb_ref[...]
    gm = gm_ref[...]

    def rope(t):
        return t * cos + pltpu.roll(t, LANES - 16, 1) * sa + pltpu.roll(t, 16, 1) * sb

    def headnorm(t, w):
        hi, lo = _split_bf16(t * t)
        ss = _dot(hi, gm) + _dot(lo, gm)
        return t * lax.rsqrt(ss * (1.0 / HEAD_DIM) + EPS) * w

    pa = _dot(hb, w_ref[:, 0:1024])
    for c in range(2):
        qa_ref[:, c * LANES:(c + 1) * LANES] = (rope(pa[:, c * LANES:(c + 1) * LANES]) * ATTN_SCALE).astype(BF16)
    ka_ref[...] = rope(pa[:, 256:384]).astype(BF16)
    va_ref[...] = pa[:, 384:512].astype(BF16)
    qn = qn_ref[...]
    kn = kn_ref[...]
    for c in range(2):
        t = headnorm(pa[:, 512 + c * LANES:512 + (c + 1) * LANES], qn)
        qb_ref[:, c * LANES:(c + 1) * LANES] = (rope(t) * ATTN_SCALE).astype(BF16)
    kb_ref[...] = rope(headnorm(pa[:, 768:896], kn)).astype(BF16)
    vb_ref[...] = pa[:, 896:1024].astype(BF16)

    hq_ref[...] = _silu(_dot(hb, w_ref[:, 1024:1536]))
    for d in range(2):
        z = _dot(hb, w_ref[:, 1536 + 512 * d:2048 + 512 * d])
        lb = lb_ref[d:d + 1, :]
        e = jnp.exp(-jnp.abs(z))
        r = 1.0 / (1.0 + e)
        er = e * r
        pos = z >= 0.0
        sig = jnp.where(pos, r, er)
        sig_neg = jnp.where(pos, er, r)
        f = lb + (1.0 - lb) * sig
        hg_ref[d] = jnp.log(jnp.maximum(f, TINY))
        hk_ref[d] = (1.0 - lb) * sig_neg
    hv_ref[...] = _dot(hb, w_ref[:, 2560:3072])
    hgate_ref[...] = _silu(_dot(hb, w_ref[:, 3072:3584]))


def _in_proj(x_all, mods_l, n1, w_in_b, cos_t, sa_t, sb_t, qn, kn, lb, gm):
    nblk = T_ALL // ROW_BLK
    row = lambda w: pl.BlockSpec((ROW_BLK, w), lambda i: (i, 0))
    full = lambda a: pl.BlockSpec(a.shape, lambda i: (0,) * a.ndim)
    tab = pl.BlockSpec((ROW_BLK, LANES), lambda i: (jnp.where(i < N_LAT // ROW_BLK, i % (SEQ // ROW_BLK), SEQ // ROW_BLK), 0))
    two = pl.BlockSpec((2, ROW_BLK, 512), lambda i: (0, i, 0))
    sd = lambda w, dt: jax.ShapeDtypeStruct((T_ALL, w), dt)
    return pl.pallas_call(
        _in_kernel,
        grid=(nblk,),
        in_specs=[row(D_MODEL), full(mods_l), full(n1), full(w_in_b), tab, tab, tab, full(qn), full(kn), full(lb), full(gm)],
        out_specs=[row(256), row(128), row(128), row(256), row(128), row(128), row(512), two, two, row(512), row(512)],
        out_shape=[sd(256, BF16), sd(128, BF16), sd(128, BF16), sd(256, BF16), sd(128, BF16), sd(128, BF16),
                   sd(512, F32), jax.ShapeDtypeStruct((2, T_ALL, 512), F32), jax.ShapeDtypeStruct((2, T_ALL, 512), F32),
                   sd(512, F32), sd(512, F32)],
        compiler_params=_cparams(("arbitrary",)),
        name="in_proj",
    )(x_all, mods_l, n1, w_in_b, cos_t, sa_t, sb_t, qn, kn, lb, gm)


def _attn_core(q_ref, parts, sink_ref, o_ref, tq):
    lane = lax.broadcasted_iota(jnp.int32, (tq, LANES), 1)
    k = jnp.concatenate([p[0] for p in parts], axis=0)
    v = jnp.concatenate([p[1] for p in parts], axis=0)
    n_masked = max([j + 1 for j, p in enumerate(parts) if p[2] is not None], default=0)
    if n_masked:
        bias = jnp.concatenate(
            [jnp.where(p[2], 0.0, NEG_INF) if p[2] is not None else jnp.zeros((tq, p[0].shape[0]), F32)
             for p in parts[:n_masked]], axis=1)
        l_masked = bias.shape[1]
    outs = []
    for h in range(4):
        kvh = h // 2
        q128 = q_ref[:, kvh * LANES:(kvh + 1) * LANES].astype(F32)
        if (h % 2) != kvh:
            q128 = pltpu.roll(q128, HEAD_DIM, 1)
        qm = jnp.where((lane >= kvh * HEAD_DIM) & (lane < (kvh + 1) * HEAD_DIM), q128, 0.0).astype(BF16)
        s = _dot_nt(qm, k)
        if n_masked:
            s = jnp.concatenate([jnp.where(bias < 0.0, NEG_INF, s[:, :l_masked]), s[:, l_masked:]], axis=1)
        m = s.max(axis=-1, keepdims=True)
        if sink_ref is not None:
            sk = sink_ref[:, h:h + 1]
            m = jnp.maximum(m, sk)
            den = jnp.exp(sk - m)
        else:
            den = jnp.zeros_like(m)
        p = jnp.exp(s - m)
        den = den + p.sum(axis=-1, keepdims=True)
        outs.append(_dot(p.astype(BF16), v) / den)
    for c in range(2):
        a, b = outs[2 * c], outs[2 * c + 1]
        if c == 0:
            b = pltpu.roll(b, HEAD_DIM, 1)
        else:
            a = pltpu.roll(a, HEAD_DIM, 1)
        o_ref[:, c * LANES:(c + 1) * LANES] = jnp.where(lane < HEAD_DIM, a, b).astype(BF16)


A_TQ = 512


def _attn_a_kernel(q_ref, kp_ref, kc_ref, kn_ref, vp_ref, vc_ref, vn_ref, kx_ref, vx_ref, sink_ref, o_ref):
    n = pl.program_id(1)
    qi = lax.broadcasted_iota(jnp.int32, (A_TQ, WINDOW), 0)
    kj = lax.broadcasted_iota(jnp.int32, (A_TQ, WINDOW), 1)
    mask_prev = (kj >= qi) & (n > 0)
    mask_next = (kj <= qi - (A_TQ - WINDOW)) & (n < SEQ // A_TQ - 1)
    qc = lax.broadcasted_iota(jnp.int32, (A_TQ, A_TQ), 0)
    kc = lax.broadcasted_iota(jnp.int32, (A_TQ, A_TQ), 1)
    mask_cur = jnp.abs(qc - kc) <= WINDOW
    parts = [(kp_ref[...], vp_ref[...], mask_prev), (kc_ref[...], vc_ref[...], mask_cur),
             (kn_ref[...], vn_ref[...], mask_next), (kx_ref[...], vx_ref[...], None)]
    _attn_core(q_ref, parts, sink_ref, o_ref, A_TQ)


def _attn_a_latent(qa, ka, va, sink):
    nq = SEQ // A_TQ
    nb = SEQ // WINDOW
    per = A_TQ // WINDOW
    q_spec = pl.BlockSpec((A_TQ, 256), lambda b, n: (b * nq + n, 0))
    prev = pl.BlockSpec((WINDOW, LANES), lambda b, n: (b * nb + jnp.maximum(n * per - 1, 0), 0))
    cur = pl.BlockSpec((A_TQ, LANES), lambda b, n: (b * nq + n, 0))
    nxt = pl.BlockSpec((WINDOW, LANES), lambda b, n: (b * nb + jnp.minimum((n + 1) * per, nb - 1), 0))
    ctx = pl.BlockSpec((CTX_LEN, LANES), lambda b, n: (N_LAT // CTX_LEN + b, 0))
    return pl.pallas_call(
        _attn_a_kernel,
        grid=(BATCH, nq),
        in_specs=[q_spec, prev, cur, nxt, prev, cur, nxt, ctx, ctx, pl.BlockSpec((1, LANES), lambda b, n: (0, 0))],
        out_specs=pl.BlockSpec((A_TQ, 256), lambda b, n: (b * nq + n, 0)),
        out_shape=jax.ShapeDtypeStruct((N_LAT, 256), BF16),
        compiler_params=_cparams(("arbitrary", "arbitrary")),
        name="attn_window",
    )(qa, ka, ka, ka, va, va, va, ka, va, sink)


B_TQ = 512


def _attn_b_kernel(q_ref, kl_ref, vl_ref, kx_ref, vx_ref, o_ref):
    parts = [(kl_ref[...], vl_ref[...], None), (kx_ref[...], vx_ref[...], None)]
    _attn_core(q_ref, parts, None, o_ref, B_TQ)


def _attn_b_latent(qb, kb, vb):
    nq = SEQ // B_TQ
    lat = pl.BlockSpec((SEQ, LANES), lambda b, j: (b, 0))
    ctx = pl.BlockSpec((CTX_LEN, LANES), lambda b, j: (N_LAT // CTX_LEN + b, 0))
    return pl.pallas_call(
        _attn_b_kernel,
        grid=(BATCH, nq),
        in_specs=[pl.BlockSpec((B_TQ, 256), lambda b, j: (b * nq + j, 0)), lat, lat, ctx, ctx],
        out_specs=pl.BlockSpec((B_TQ, 256), lambda b, j: (b * nq + j, 0)),
        out_shape=jax.ShapeDtypeStruct((N_LAT, 256), BF16),
        compiler_params=_cparams(("arbitrary", "arbitrary")),
        name="attn_full",
    )(qb, kb, vb, kb, vb)


def _attn_ctx_sink_kernel(q_ref, kx_ref, vx_ref, sink_ref, o_ref):
    _attn_core(q_ref, [(kx_ref[...], vx_ref[...], None)], sink_ref, o_ref, CTX_LEN)


def _attn_ctx_kernel(q_ref, kx_ref, vx_ref, o_ref):
    _attn_core(q_ref, [(kx_ref[...], vx_ref[...], None)], None, o_ref, CTX_LEN)


def _attn_ctx(q, k, v, sink):
    blk = lambda w: pl.BlockSpec((CTX_LEN, w), lambda b: (N_LAT // CTX_LEN + b, 0))
    in_specs = [blk(256), blk(LANES), blk(LANES)]
    args = [q, k, v]
    if sink is not None:
        in_specs.append(pl.BlockSpec((1, LANES), lambda b: (0, 0)))
        args.append(sink)
    return pl.pallas_call(
        _attn_ctx_sink_kernel if sink is not None else _attn_ctx_kernel,
        grid=(BATCH,),
        in_specs=in_specs,
        out_specs=pl.BlockSpec((CTX_LEN, 256), lambda b: (b, 0)),
        out_shape=jax.ShapeDtypeStruct((N_CTX, 256), BF16),
        compiler_params=_cparams(("arbitrary",)),
        name="attn_ctx_sink" if sink is not None else "attn_ctx",
    )(*args)


def _scan_constants():
    c = SCAN_CHUNK
    t = np.arange(c)[:, None]
    u = np.arange(c)[None, :]
    mats = [(u <= t)]
    masks = []
    for lvl in range(SCAN_LEVELS):
        m = 2 ** lvl
        second = ((t // m) % 2 == 1)
        end_first = (t // (2 * m)) * 2 * m + m - 1
        mats.append((second & (u > end_first) & (u <= t)) | ((~second) & (u > t) & (u <= end_first)))
        s = u
        masks.append(second & ((s // m) % 2 == 0) & ((s // (2 * m)) == (t // (2 * m))))
    fwd = np.concatenate(mats, axis=0).astype(np.float32)
    fmask = np.stack(masks).astype(np.float32)
    bwd = fwd.reshape(SCAN_MATS, c, c)[:, ::-1, ::-1].reshape(SCAN_MATS * c, c)
    bmask = fmask[:, ::-1, ::-1]
    both = np.stack([fwd, bwd])
    pair_masks = np.stack([fmask, bmask])
    return np.concatenate([both, both], axis=2), np.concatenate([pair_masks, pair_masks], axis=3)


def _scan_chunk(d, q_ref, k_ref, g_ref, v_ref, mall_ref, mask_ref, o_ref, st_ref):
    c = SCAN_CHUNK
    w = 2 * C_KDIM
    lf = g_ref[0]
    hi, lo = _split_bf16(lf)
    sums = _dot(mall_ref[d], jnp.concatenate([hi, lo], axis=0))
    ex = jnp.exp(sums)
    total = jnp.sum(lf, axis=0, keepdims=True)
    ex_after = jnp.exp(total - sums[0:c])
    tot = jnp.exp(total)
    first = lax.broadcasted_iota(jnp.int32, (c, w), 1) < C_KDIM
    row_first = lax.broadcasted_iota(jnp.int32, (w, w), 0) < C_KDIM
    col_first = lax.broadcasted_iota(jnp.int32, (w, w), 1) < C_KDIM
    same_head = (row_first == col_first).astype(F32)

    def block_diag(a):
        return jnp.concatenate([jnp.where(first, a, 0.0), jnp.where(first, 0.0, a)], axis=0).astype(BF16)

    for pair in range(C_HEADS // 2):
        sl = slice(pair * w, (pair + 1) * w)
        q = q_ref[:, sl]
        k = k_ref[0, :, sl]
        v = v_ref[:, sl]
        exh = lambda idx: ex[idx * c:(idx + 1) * c, sl]
        st = st_ref[d, pair]
        inter = _dot_nt((q * exh(0)).astype(BF16), st.astype(BF16))
        att = jnp.zeros((c, w), F32)
        for lvl in range(SCAN_LEVELS):
            e = exh(1 + lvl)
            pr = _dot_nt((q * e).astype(BF16), block_diag(k * e))
            att = att + mask_ref[d, lvl] * pr
        diag = _dot((q * k).astype(BF16), same_head.astype(BF16))
        o_ref[:, sl] = inter + _dot(att.astype(BF16), block_diag(v)) + diag * v
        khat = (k * ex_after[:, sl]).astype(BF16)
        st_ref[d, pair] = st * tot[:, sl] + same_head * _dot(v.T.astype(BF16), khat)


def _scan_kernel(qf_ref, kf_ref, gf_ref, vf_ref, qb_ref, kb_ref, gb_ref, vb_ref, mall_ref, mask_ref,
                 of_ref, ob_ref, st_ref):
    @pl.when(pl.program_id(1) == 0)
    def _():
        st_ref[...] = jnp.zeros_like(st_ref)

    _scan_chunk(0, qf_ref, kf_ref, gf_ref, vf_ref, mall_ref, mask_ref, of_ref, st_ref)
    _scan_chunk(1, qb_ref, kb_ref, gb_ref, vb_ref, mall_ref, mask_ref, ob_ref, st_ref)


def _hgrn_scan(hq, hk, hg, hv, mall, masks):
    c = SCAN_CHUNK
    n_ctx_chunks = CTX_LEN // c
    n_lat_chunks = SEQ // c
    steps = n_ctx_chunks + n_lat_chunks

    def rb(d):
        def index(b, j):
            ctx_blk = N_LAT // c + n_ctx_chunks * b + (j if d == 0 else n_ctx_chunks - 1 - j)
            jl = j - n_ctx_chunks
            lat_blk = b * n_lat_chunks + (jl if d == 0 else n_lat_chunks - 1 - jl)
            return jnp.where(j < n_ctx_chunks, ctx_blk, lat_blk)
        return index

    row = lambda d: pl.BlockSpec((c, 512), lambda b, j: (rb(d)(b, j), 0))
    drow = lambda d: pl.BlockSpec((1, c, 512), lambda b, j: (d, rb(d)(b, j), 0))
    full = lambda a: pl.BlockSpec(a.shape, lambda b, j: (0,) * a.ndim)
    o_sd = jax.ShapeDtypeStruct((T_ALL, 512), F32)
    return pl.pallas_call(
        _scan_kernel,
        grid=(BATCH, steps),
        in_specs=[row(0), drow(0), drow(0), row(0), row(1), drow(1), drow(1), row(1), full(mall), full(masks)],
        out_specs=[row(0), row(1)],
        out_shape=[o_sd, o_sd],
        scratch_shapes=[pltpu.VMEM((2, C_HEADS // 2, 2 * C_VDIM, 2 * C_KDIM), F32)],
        compiler_params=_cparams(("arbitrary", "arbitrary")),
        name="hgrn_scan",
    )(hq, hk, hg, hv, hq, hk, hg, hv, mall, masks)


def _out_kernel(ya_ref, yb_ref, of_ref, ob_ref, gate_ref, x_ref, mod_ref, gw_ref, w_ref, n2_ref,
                rwh_ref, rwl_ref, rb_ref, tri_ref, xo_ref, h2_ref, ti_ref, tg_ref, rank_ref, cnt_ref, carry_ref):
    i = pl.program_id(0)
    midx = jnp.minimum(i // (SEQ // ROW_BLK), BATCH)
    mod = mod_ref[pl.ds(midx, 1), :]
    g1 = mod[:, 2 * D_MODEL:3 * D_MODEL]
    sh2 = mod[:, 3 * D_MODEL:4 * D_MODEL]
    sc2 = mod[:, 4 * D_MODEL:5 * D_MODEL]

    o = of_ref[...] + ob_ref[...]
    y = _dot(ya_ref[...], w_ref[0:256, :]) + _dot(yb_ref[...], w_ref[256:512, :])
    for h in range(C_HEADS):
        sl = slice(h * C_VDIM, (h + 1) * C_VDIM)
        oh = o[:, sl]
        yn = oh * lax.rsqrt(jnp.mean(oh * oh, axis=-1, keepdims=True) + EPS) * gw_ref[...]
        yc = (yn * gate_ref[:, sl]).astype(BF16)
        y = y + _dot(yc, w_ref[512 + h * C_VDIM:512 + (h + 1) * C_VDIM, :])
    x = x_ref[...] + g1 * y
    xo_ref[...] = x
    h2 = x * lax.rsqrt(jnp.mean(x * x, axis=-1, keepdims=True) + EPS) * n2_ref[...]
    h2 = h2 * (1.0 + sc2) + sh2
    _store_token_tiles(h2_ref, _pack_bf16_pairs(h2), PACKED_TILE_ROWS)

    hi, lo = _split_bf16(h2)
    rwh = rwh_ref[...]
    logits = _dot(hi, rwh) + _dot(hi, rwl_ref[...]) + _dot(lo, rwh) + rb_ref[...]
    lane = lax.broadcasted_iota(jnp.int32, (ROW_BLK, LANES), 1).astype(F32)
    ti = jnp.zeros((ROW_BLK, LANES), F32)
    ex = jnp.zeros((ROW_BLK, LANES), F32)
    den = jnp.zeros((ROW_BLK, 1), F32)
    top = None
    onehots = []
    for k in range(TOP_K):
        m = logits.max(axis=-1, keepdims=True)
        idx = jnp.min(jnp.where(logits == m, lane, float(LANES)), axis=-1, keepdims=True)
        if top is None:
            top = m
        e = jnp.exp(m - top)
        den = den + e
        ti = jnp.where(lane == float(k), idx, ti)
        ex = jnp.where(lane == float(k), e, ex)
        chosen = lane == idx
        onehots.append(chosen.astype(F32))
        logits = jnp.where(chosen, -3.0e38, logits)
    ti_ref[...] = ti.astype(jnp.int32)
    tg_ref[...] = ex / den

    @pl.when(i == 0)
    def _():
        carry_ref[...] = jnp.zeros_like(carry_ref)

    tot = onehots[0] + onehots[1] + onehots[2] + onehots[3]
    before = _dot(tri_ref[...], tot.astype(BF16)) + carry_ref[0:1, :]
    r = jnp.zeros((ROW_BLK, LANES), F32)
    for k in range(TOP_K):
        rk = jnp.sum(onehots[k] * before, axis=-1, keepdims=True)
        r = jnp.where(lane == float(k), rk, r)
    rank_ref[...] = r.astype(jnp.int32)
    carry = carry_ref[0:1, :] + jnp.sum(tot, axis=0, keepdims=True)
    carry_ref[0:1, :] = carry
    cnt_ref[...] = jnp.broadcast_to(carry, (8, LANES))


def _out_proj(n_rows, ya, yb, o_f, o_b, gate, x_all, mods_l, gw, w_out_b, n2, rwh, rwl, rb, tri):
    nblk = n_rows // ROW_BLK
    row = lambda w: pl.BlockSpec((ROW_BLK, w), lambda i: (i, 0))
    full = lambda a: pl.BlockSpec(a.shape, lambda i: (0,) * a.ndim)
    sd = lambda w, dt: jax.ShapeDtypeStruct((n_rows, w), dt)
    return pl.pallas_call(
        _out_kernel,
        grid=(nblk,),
        in_specs=[row(256), row(256), row(512), row(512),
                  row(512), row(D_MODEL), full(mods_l), full(gw), full(w_out_b), full(n2),
                  full(rwh), full(rwl), full(rb), full(tri)],
        out_specs=[row(D_MODEL), pl.BlockSpec((ROW_BLK * PACKED_TILE_ROWS, LANES), lambda i: (i, 0)),
                   row(LANES), row(LANES), row(LANES), pl.BlockSpec((8, LANES), lambda i: (0, 0))],
        out_shape=[sd(D_MODEL, F32), jax.ShapeDtypeStruct((n_rows * PACKED_TILE_ROWS, LANES), jnp.uint32),
                   sd(LANES, jnp.int32), sd(LANES, F32), sd(LANES, jnp.int32),
                   jax.ShapeDtypeStruct((8, LANES), F32)],
        scratch_shapes=[pltpu.VMEM((8, LANES), F32)],
        compiler_params=_cparams(("arbitrary",)),
        name="out_proj_router",
    )(ya, yb, o_f, o_b, gate, x_all, mods_l, gw, w_out_b, n2, rwh, rwl, rb, tri)


DEINT = 256


SCATTER_TOKENS = 256


def _dispatch_kernel(pad_blk_ref, pad_new_ref, dst_ref, h_ref, xs_ref, zbuf, sem):
    n = PACKED_TILE_ROWS
    blk_rows = MOE_BLOCK * n

    def zero_fill(j):
        start = pl.multiple_of(pad_blk_ref[j] * blk_rows, blk_rows)
        return pltpu.make_async_copy(zbuf, xs_ref.at[pl.ds(start, blk_rows), :], sem)

    @pl.when(pl.program_id(0) == 0)
    def _():
        zbuf[...] = jnp.zeros_like(zbuf)
        for j in range(2 * N_EXPERTS):
            @pl.when(pad_new_ref[j] == 1)
            def _():
                zero_fill(j).start()
        for j in range(2 * N_EXPERTS):
            @pl.when(pad_new_ref[j] == 1)
            def _():
                zero_fill(j).wait()

    def copy(t, k, dst_row):
        return pltpu.make_async_copy(h_ref.at[pl.ds(t * n, n), :],
                                     xs_ref.at[pl.ds(pl.multiple_of(dst_row, n), n), :], sem)

    for t in range(SCATTER_TOKENS):
        for k in range(TOP_K):
            copy(t, k, dst_ref[0, 0, t * TOP_K + k]).start(priority=k % 2)
    for t in range(SCATTER_TOKENS):
        for k in range(TOP_K):
            copy(t, k, 0).wait()


def _dispatch(h2t, dst, pad_blk, pad_new, p_rows):
    nblk = dst.shape[0]
    shape = (p_rows * PACKED_TILE_ROWS, LANES)
    grid_spec = pltpu.PrefetchScalarGridSpec(
        num_scalar_prefetch=2,
        grid=(nblk,),
        in_specs=[pl.BlockSpec((1, 1, SCATTER_TOKENS * TOP_K), lambda i, pb, pn: (i, 0, 0), memory_space=pltpu.SMEM),
                  pl.BlockSpec((SCATTER_TOKENS * PACKED_TILE_ROWS, LANES), lambda i, pb, pn: (i, 0))],
        out_specs=pl.BlockSpec(memory_space=pl.ANY),
        scratch_shapes=[pltpu.VMEM((MOE_BLOCK * PACKED_TILE_ROWS, LANES), jnp.uint32), pltpu.SemaphoreType.DMA(())],
    )
    return pl.pallas_call(
        _dispatch_kernel,
        grid_spec=grid_spec,
        out_shape=jax.ShapeDtypeStruct(shape, jnp.uint32),
        compiler_params=_cparams(("arbitrary",)),
        name="moe_dispatch",
    )(pad_blk, pad_new, dst, h2t)


def _moe_kernel(blk_e_ref, grp_slot_ref, next_e_ref, x_ref, w1_ref, b1_ref, w2_ref, b2_ref, perm_ref, y_ref,
                wf1, wf2, w1p_ref, w2p_ref, wsem, *, layer):
    i = pl.program_id(0)

    def fetch(e, s):
        return (pltpu.make_async_copy(w1_ref.at[layer, e], wf1.at[s], wsem.at[s]),
                pltpu.make_async_copy(w2_ref.at[layer, e], wf2.at[s], wsem.at[s]))

    @pl.when(i == 0)
    def _():
        for c in fetch(blk_e_ref[0], 0):
            c.start()

    new_expert = (i == 0) | (blk_e_ref[i] != blk_e_ref[jnp.maximum(i - 1, 0)])

    @pl.when(new_expert)
    def _():
        s = grp_slot_ref[i]
        for c in fetch(0, s):
            c.wait()

        @pl.when(next_e_ref[i] >= 0)
        def _():
            for c in fetch(next_e_ref[i], 1 - s):
                c.start()

        perm = perm_ref[...]
        half = DEINT // 2
        for j in range(2 * D_FF // DEINT):
            r = _dot(wf1[s, :, j * DEINT:(j + 1) * DEINT].astype(BF16), perm)
            w1p_ref[:, j * half:(j + 1) * half] = r[:, 0:half].astype(BF16)
            w1p_ref[:, D_FF + j * half:D_FF + (j + 1) * half] = r[:, half:DEINT].astype(BF16)
        w2p_ref[...] = wf2[s].astype(BF16)

    x = _unpack_bf16_pairs(_load_token_tiles(x_ref, MOE_BLOCK, rows_per_token=PACKED_TILE_ROWS))
    u = _dot(x, w1p_ref[...]) + b1_ref[0, 0]
    u_glu = jnp.minimum(u[:, 0:D_FF], SWIGLU_LIMIT)
    u_lin = jnp.clip(u[:, D_FF:2 * D_FF], -SWIGLU_LIMIT, SWIGLU_LIMIT)
    act = u_glu * jax.nn.sigmoid(SWIGLU_ALPHA * u_glu) * (u_lin + 1.0)
    y = _dot(act.astype(BF16), w2p_ref[...]) + b2_ref[0, 0]
    _store_token_tiles(y_ref, y)


def _expert_mlp(layer, xs, blk_e, grp_slot, next_e, w1, b1p, w2, b2, perm):
    n_blocks = blk_e.shape[0]
    grid_spec = pltpu.PrefetchScalarGridSpec(
        num_scalar_prefetch=3,
        grid=(n_blocks,),
        in_specs=[pl.BlockSpec((MOE_BLOCK * PACKED_TILE_ROWS, LANES), lambda i, be, gs, ne: (i, 0)),
                  pl.BlockSpec(memory_space=pl.ANY),
                  pl.BlockSpec((1, 1, 1, 2 * D_FF), lambda i, be, gs, ne: (layer, be[i], 0, 0)),
                  pl.BlockSpec(memory_space=pl.ANY),
                  pl.BlockSpec((1, 1, 1, D_MODEL), lambda i, be, gs, ne: (layer, be[i], 0, 0)),
                  pl.BlockSpec((DEINT, DEINT), lambda i, be, gs, ne: (0, 0))],
        out_specs=pl.BlockSpec((MOE_BLOCK * TILE_ROWS, LANES), lambda i, be, gs, ne: (i, 0)),
        scratch_shapes=[pltpu.VMEM((2, D_MODEL, 2 * D_FF), F32), pltpu.VMEM((2, D_FF, D_MODEL), F32),
                        pltpu.VMEM((D_MODEL, 2 * D_FF), BF16), pltpu.VMEM((D_FF, D_MODEL), BF16),
                        pltpu.SemaphoreType.DMA((2,))],
    )
    return pl.pallas_call(
        functools.partial(_moe_kernel, layer=layer),
        grid_spec=grid_spec,
        out_shape=jax.ShapeDtypeStruct((n_blocks * MOE_BLOCK * TILE_ROWS, LANES), F32),
        compiler_params=_cparams(("arbitrary",)),
        name="expert_mlp",
    )(blk_e, grp_slot, next_e, xs, w1, b1p, w2, b2, perm)


def _combine_kernel(src_ref, src_next_ref, ys_ref, tg_ref, x_ref, mod_ref, nf_ref, o_ref, ybuf, sem, *, final):
    i = pl.program_id(0)
    last = pl.num_programs(0) - 1
    slot = i % 2
    n = TILE_ROWS
    n_copies = TOP_K * ROW_BLK
    rows = lambda start: pl.ds(pl.multiple_of(start, n), n)

    def gather(src_row, r, s):
        return pltpu.make_async_copy(ys_ref.at[rows(src_row), :], ybuf.at[s, rows(r * n), :], sem.at[s])

    @pl.when(i == 0)
    def _():
        for r in range(n_copies):
            gather(src_ref[0, 0, r], r, 0).start(priority=r % 2)

    for r in range(n_copies):
        gather(src_next_ref[0, 0, r], r, 1 - slot).start(priority=r % 2)
    for r in range(n_copies):
        gather(0, r, slot).wait()

    midx = jnp.minimum(i // (SEQ // ROW_BLK), BATCH)
    g2 = mod_ref[pl.ds(midx, 1), 5 * D_MODEL:6 * D_MODEL]
    tg = tg_ref[...]
    f = jnp.zeros((ROW_BLK, D_MODEL), F32)
    for k in range(TOP_K):
        cols = [ybuf[slot, pl.ds(k * ROW_BLK * n + c, ROW_BLK, stride=n), :] for c in range(n)]
        f = f + jnp.concatenate(cols, axis=1) * tg[:, k:k + 1]
    x = x_ref[...] + g2 * f
    if final:
        x = x * lax.rsqrt(jnp.mean(x * x, axis=-1, keepdims=True) + EPS) * nf_ref[...]
    o_ref[...] = x

    @pl.when(i == last)
    def _():
        for r in range(n_copies):
            gather(0, r, 1 - slot).wait()


def _combine(ys, src, tg, x_mid, mods_l, nf, final):
    n_rows = x_mid.shape[0]
    nblk = n_rows // ROW_BLK
    full = lambda a: pl.BlockSpec(a.shape, lambda i: (0,) * a.ndim)
    idx = lambda f: pl.BlockSpec((1, 1, TOP_K * ROW_BLK), f, memory_space=pltpu.SMEM)
    return pl.pallas_call(
        functools.partial(_combine_kernel, final=final),
        grid=(nblk,),
        in_specs=[idx(lambda i: (i, 0, 0)), idx(lambda i: (jnp.minimum(i + 1, nblk - 1), 0, 0)),
                  pl.BlockSpec(memory_space=pl.ANY),
                  pl.BlockSpec((ROW_BLK, LANES), lambda i: (i, 0)),
                  pl.BlockSpec((ROW_BLK, D_MODEL), lambda i: (i, 0)), full(mods_l), full(nf)],
        out_specs=pl.BlockSpec((ROW_BLK, D_MODEL), lambda i: (i, 0)),
        out_shape=jax.ShapeDtypeStruct((n_rows, D_MODEL), F32),
        scratch_shapes=[pltpu.VMEM((2, TOP_K * ROW_BLK * TILE_ROWS, LANES), F32), pltpu.SemaphoreType.DMA((2,))],
        compiler_params=_cparams(("arbitrary",)),
        name="moe_combine_final" if final else "moe_combine",
    )(src, src, ys, tg, x_mid, mods_l, nf)


def _moe(layer, h2t, top_i, top_g, rank, cnt, x_mid, mods_l, w1, b1p, w2, b2, nf, perm, final):
    n_rows = x_mid.shape[0]
    tk = n_rows * TOP_K
    n_blocks = -(-(tk + N_EXPERTS * (MOE_BLOCK - 1)) // MOE_BLOCK)
    p_rows = n_blocks * MOE_BLOCK

    counts = cnt[0, :N_EXPERTS].astype(jnp.int32)
    padded = (counts + MOE_BLOCK - 1) // MOE_BLOCK * MOE_BLOCK
    pend = jnp.cumsum(padded)
    pstart = pend - padded
    ti4 = top_i[:, :TOP_K]
    dest = pstart[ti4] + rank[:, :TOP_K]
    blk_start = jnp.arange(n_blocks, dtype=jnp.int32) * MOE_BLOCK
    blk_e = jnp.minimum(jnp.sum((pend[None, :] <= blk_start[:, None]).astype(jnp.int32), axis=1), N_EXPERTS - 1)
    change = jnp.concatenate([jnp.ones((1,), jnp.int32), (blk_e[1:] != blk_e[:-1]).astype(jnp.int32)])
    grp_slot = (jnp.cumsum(change) - 1) & 1
    experts = jnp.arange(N_EXPERTS, dtype=jnp.int32)
    present = jnp.any(blk_e[:, None] == experts[None, :], axis=0)
    later = present[None, :] & (experts[None, :] > blk_e[:, None])
    next_e = jnp.min(jnp.where(later, experts[None, :], N_EXPERTS), axis=1)
    next_e = jnp.where(next_e == N_EXPERTS, -1, next_e).astype(jnp.int32)

    n_tok_blk = n_rows // SCATTER_TOKENS
    dst_x = (dest * PACKED_TILE_ROWS).reshape(n_tok_blk, 1, SCATTER_TOKENS * TOP_K)
    last_blk = jnp.maximum(pend // MOE_BLOCK - 1, 0)
    tail_blk = jnp.minimum(pend[-1] // MOE_BLOCK + experts, n_blocks - 1)
    pad_blk = jnp.concatenate([last_blk, tail_blk]).astype(jnp.int32)
    pad_new = jnp.concatenate([jnp.ones((1,), jnp.int32), (pad_blk[1:] != pad_blk[:-1]).astype(jnp.int32)])
    xs = _dispatch(h2t, dst_x, pad_blk, pad_new, p_rows)
    ys = _expert_mlp(layer, xs, blk_e, grp_slot.astype(jnp.int32), next_e, w1, b1p, w2, b2, perm)
    src_y = (dest * TILE_ROWS).reshape(n_rows // ROW_BLK, ROW_BLK, TOP_K).transpose(0, 2, 1)
    src_y = src_y.reshape(n_rows // ROW_BLK, 1, TOP_K * ROW_BLK)
    return _combine(ys, src_y, top_g, x_mid, mods_l, nf, final)


def _rope_tables():
    rows = SEQ // GRID_W
    row = jnp.repeat(jnp.arange(rows), GRID_W).astype(F32)
    col = jnp.tile(jnp.arange(GRID_W), rows).astype(F32)
    half = HEAD_DIM // 2
    inv = 1.0 / (ROPE_THETA ** (jnp.arange(0, half, 2, dtype=F32) / half))
    ang_r = row[:, None] * inv
    ang_c = col[:, None] * inv
    ang = jnp.concatenate([ang_r, ang_r, ang_c, ang_c], axis=-1)
    cos, sin = jnp.cos(ang), jnp.sin(ang)
    first = (jnp.arange(HEAD_DIM) % 32) < 16
    sa = jnp.where(first, -sin, 0.0)
    sb = jnp.where(first, 0.0, sin)
    ext = lambda t, fill: jnp.concatenate(
        [jnp.tile(t, (1, 2)), jnp.full((ROW_BLK, LANES), fill, F32)], axis=0)
    return ext(cos, 1.0), ext(sa, 0.0), ext(sb, 0.0)


def kernel(x, c, ctx, c_ctx, w_mod, b_mod, norm1_w, norm2_w, w_in, w_out, attn_sink, q_norm_w, k_norm_w,
           hgrn_lb, gate_norm_w, router_w, router_b, w1, b1, w2, b2, final_norm_w):
    cos_t, sa_t, sb_t = _rope_tables()
    lbs = jax.nn.softmax(hgrn_lb.astype(F32), axis=0)
    lbs = jnp.cumsum(lbs, axis=0) - lbs[0]
    mall_np, masks_np = _scan_constants()
    mall = jnp.asarray(mall_np, BF16)
    masks = jnp.asarray(masks_np, F32)
    gm = jnp.asarray(np.kron(np.eye(2), np.ones((HEAD_DIM, HEAD_DIM))), BF16)
    tri = jnp.asarray(np.tril(np.ones((ROW_BLK, ROW_BLK)), -1), BF16)

    c_all = jnp.concatenate([c, c_ctx[None, :], jnp.zeros((3, D_MODEL), F32)], axis=0)
    mods = _modulation(c_all, w_mod, b_mod)
    x_all = jnp.concatenate([x.reshape(N_LAT, D_MODEL), ctx.reshape(N_CTX, D_MODEL)], axis=0)
    nf = final_norm_w.reshape(1, D_MODEL)
    b1p = jnp.concatenate([b1[..., 0::2], b1[..., 1::2]], axis=-1).reshape(DEPTH, N_EXPERTS, 1, 2 * D_FF)
    b2r = b2.reshape(DEPTH, N_EXPERTS, 1, D_MODEL)
    src = np.concatenate([2 * np.arange(DEINT // 2), 2 * np.arange(DEINT // 2) + 1])
    perm = jnp.asarray(np.arange(DEINT)[:, None] == src[None, :], BF16)

    out = None
    for l in range(DEPTH):
        last = l == DEPTH - 1
        mods_l = mods[l]
        qn = jnp.tile(q_norm_w[l], 2).reshape(1, LANES)
        kn = jnp.tile(k_norm_w[l], 2).reshape(1, LANES)
        sink = jnp.zeros((1, LANES), F32).at[0, :A_HEADS].set(attn_sink[l])
        (qa, ka, va, qb, kb, vb, hq, hk, hg, hv, hgate) = _in_proj(
            x_all, mods_l, norm1_w[l].reshape(1, D_MODEL), w_in[l].astype(BF16), cos_t, sa_t, sb_t, qn, kn, lbs[l], gm)

        ya = _attn_a_latent(qa, ka, va, sink)
        yb = _attn_b_latent(qb, kb, vb)
        o_f, o_b = _hgrn_scan(hq, hk, hg, hv, mall, masks)
        if not last:
            ya = jnp.concatenate([ya, _attn_ctx(qa, ka, va, sink)], axis=0)
            yb = jnp.concatenate([yb, _attn_ctx(qb, kb, vb, None)], axis=0)
        n_rows = N_LAT if last else T_ALL

        rw = jnp.zeros((D_MODEL, LANES), F32).at[:, :N_EXPERTS].set(router_w[l])
        rwh, rwl = _split_bf16(rw)
        rb = jnp.full((1, LANES), NEG_INF, F32).at[0, :N_EXPERTS].set(router_b[l])
        x_mid, h2, top_i, top_g, rank, cnt = _out_proj(
            n_rows, ya, yb, o_f, o_b, hgate, x_all, mods_l, gate_norm_w[l].reshape(1, C_VDIM),
            w_out[l].astype(BF16), norm2_w[l].reshape(1, D_MODEL), rwh, rwl, rb, tri)

        x_all = _moe(l, h2, top_i, top_g, rank, cnt, x_mid, mods_l, w1, b1p, w2, b2r, nf, perm, last)
        out = x_all
    return out.reshape(BATCH, SEQ, D_MODEL)
```

```python
import functools

import numpy as np
import jax
import jax.numpy as jnp
from jax import lax
from jax.experimental import pallas as pl
from jax.experimental.pallas import tpu as pltpu

F32 = jnp.float32
BF16 = jnp.bfloat16

D_MODEL = 1024
BATCH = 4
SEQ = 4096
DEPTH = 2
GRID_W = 64
CTX_LEN = 256
HEAD_DIM = 64
ATTN_SCALE = HEAD_DIM ** -0.5
ROPE_THETA = 10000.0
A_HEADS = 4
B_HEADS = 4
WINDOW = 128
C_HEADS = 4
C_KDIM = 128
C_VDIM = 128
IN_WIDTH = 3584
MIX_WIDTH = 1024
N_EXPERTS = 32
TOP_K = 4
D_FF = 1024
SWIGLU_ALPHA = 1.702
SWIGLU_LIMIT = 7.0
MOE_BLOCK = 256
EPS = 1e-6
NEG_INF = -1e30
TINY = 1e-30

N_LAT = BATCH * SEQ
N_CTX = BATCH * CTX_LEN
T_ALL = N_LAT + N_CTX
ROW_BLK = 256
LANES = 128
SCAN_CHUNK = 128
SCAN_LEVELS = 7
SCAN_MATS = SCAN_LEVELS + 1
VMEM_LIMIT = 56 * 1024 * 1024


def _cparams(sem):
    return pltpu.CompilerParams(dimension_semantics=sem, vmem_limit_bytes=VMEM_LIMIT)


def _split_bf16(a):
    hi = a.astype(BF16)
    lo = (a - hi.astype(F32)).astype(BF16)
    return hi, lo


def _dot(a, b):
    return jnp.dot(a, b, preferred_element_type=F32)


def _dot_nt(a, b):
    return lax.dot_general(a, b, (((1,), (1,)), ((), ())), preferred_element_type=F32)


def _silu(a):
    return a * jax.nn.sigmoid(a)


TILE_ROWS = D_MODEL // LANES
PACKED_TILE_ROWS = TILE_ROWS // 2


def _store_token_tiles(ref, val, rows_per_token=TILE_ROWS):
    n = val.shape[0]
    for c in range(rows_per_token):
        ref[pl.ds(c, n, stride=rows_per_token), :] = val[:, c * LANES:(c + 1) * LANES]


def _load_token_tiles(ref, n, lead=None, rows_per_token=TILE_ROWS):
    cols = []
    for c in range(rows_per_token):
        if lead is None:
            cols.append(ref[pl.ds(c, n, stride=rows_per_token), :])
        else:
            cols.append(ref[lead, pl.ds(c, n, stride=rows_per_token), :])
    return cols


def _pack_bf16_pairs(v):
    half = v.shape[1] // 2
    hi = pltpu.bitcast(v[:, :half].astype(BF16).astype(F32), jnp.uint32)
    lo = pltpu.bitcast(v[:, half:].astype(BF16).astype(F32), jnp.uint32)
    return hi | (lo >> 16)


def _unpack_bf16_pairs(words):
    his = [pltpu.bitcast(w & jnp.uint32(0xFFFF0000), F32) for w in words]
    los = [pltpu.bitcast(w << 16, F32) for w in words]
    return jnp.concatenate(his + los, axis=1).astype(BF16)


def _mod_kernel(c_ref, w_ref, b_ref, o_ref):
    a = _silu(c_ref[...]).astype(BF16)
    o_ref[0] = _dot(a, w_ref[0].astype(BF16)) + b_ref[0]


def _modulation(c_all, w_mod, b_mod):
    tn = 1536
    return pl.pallas_call(
        _mod_kernel,
        grid=(DEPTH, 6 * D_MODEL // tn),
        in_specs=[pl.BlockSpec((8, D_MODEL), lambda l, j: (0, 0)),
                  pl.BlockSpec((1, D_MODEL, tn), lambda l, j: (l, 0, j)),
                  pl.BlockSpec((1, 1, tn), lambda l, j: (l, 0, j))],
        out_specs=pl.BlockSpec((1, 8, tn), lambda l, j: (l, 0, j)),
        out_shape=jax.ShapeDtypeStruct((DEPTH, 8, 6 * D_MODEL), F32),
        compiler_params=_cparams(("arbitrary", "arbitrary")),
        name="modulation",
    )(c_all, w_mod, b_mod.reshape(DEPTH, 1, 6 * D_MODEL))


def _in_kernel(x_ref, mod_ref, n1_ref, w_ref, cos_ref, sa_ref, sb_ref, qn_ref, kn_ref, lb_ref, gm_ref,
               qa_ref, ka_ref, va_ref, qb_ref, kb_ref, vb_ref, hq_ref, hk_ref, hg_ref, hv_ref, hgate_ref):
    i = pl.program_id(0)
    midx = jnp.minimum(i // (SEQ // ROW_BLK), BATCH)
    mod = mod_ref[pl.ds(midx, 1), :]
    sh1 = mod[:, 0:D_MODEL]
    sc1 = mod[:, D_MODEL:2 * D_MODEL]
    x = x_ref[...]
    h = x * lax.rsqrt(jnp.mean(x * x, axis=-1, keepdims=True) + EPS) * n1_ref[...]
    hb = (h * (1.0 + sc1) + sh1).astype(BF16)

    cos = cos_ref[...]
    sa = sa_ref[...]
    sb = sb_ref[...]
    gm = gm_ref[...]

    def rope(t):
        return t * cos + pltpu.roll(t, LANES - 16, 1) * sa + pltpu.roll(t, 16, 1) * sb

    def headnorm(t, w):
        hi, lo = _split_bf16(t * t)
        ss = _dot(hi, gm) + _dot(lo, gm)
        return t * lax.rsqrt(ss * (1.0 / HEAD_DIM) + EPS) * w

    pa = _dot(hb, w_ref[:, 0:1024])
    for c in range(2):
        qa_ref[:, c * LANES:(c + 1) * LANES] = (rope(pa[:, c * LANES:(c + 1) * LANES]) * ATTN_SCALE).astype(BF16)
    ka_ref[...] = rope(pa[:, 256:384]).astype(BF16)
    va_ref[...] = pa[:, 384:512].astype(BF16)
    qn = qn_ref[...]
    kn = kn_ref[...]
    for c in range(2):
        t = headnorm(pa[:, 512 + c * LANES:512 + (c + 1) * LANES], qn)
        qb_ref[:, c * LANES:(c + 1) * LANES] = (rope(t) * ATTN_SCALE).astype(BF16)
    kb_ref[...] = rope(headnorm(pa[:, 768:896], kn)).astype(BF16)
    vb_ref[...] = pa[:, 896:1024].astype(BF16)

    hq_ref[...] = _silu(_dot(hb, w_ref[:, 1024:1536]))
    for d in range(2):
        z = _dot(hb, w_ref[:, 1536 + 512 * d:2048 + 512 * d])
        lb = lb_ref[d:d + 1, :]
        e = jnp.exp(-jnp.abs(z))
        r = 1.0 / (1.0 + e)
        er = e * r
        pos = z >= 0.0
        sig = jnp.where(pos, r, er)
        sig_neg = jnp.where(pos, er, r)
        f = lb + (1.0 - lb) * sig
        hg_ref[d] = jnp.log(jnp.maximum(f, TINY))
        hk_ref[d] = (1.0 - lb) * sig_neg
    hv_ref[...] = _dot(hb, w_ref[:, 2560:3072])
    hgate_ref[...] = _silu(_dot(hb, w_ref[:, 3072:3584]))


def _in_proj(x_all, mods_l, n1, w_in_b, cos_t, sa_t, sb_t, qn, kn, lb, gm):
    nblk = T_ALL // ROW_BLK
    row = lambda w: pl.BlockSpec((ROW_BLK, w), lambda i: (i, 0))
    full = lambda a: pl.BlockSpec(a.shape, lambda i: (0,) * a.ndim)
    tab = pl.BlockSpec((ROW_BLK, LANES), lambda i: (jnp.where(i < N_LAT // ROW_BLK, i % (SEQ // ROW_BLK), SEQ // ROW_BLK), 0))
    two = pl.BlockSpec((2, ROW_BLK, 512), lambda i: (0, i, 0))
    sd = lambda w, dt: jax.ShapeDtypeStruct((T_ALL, w), dt)
    return pl.pallas_call(
        _in_kernel,
        grid=(nblk,),
        in_specs=[row(D_MODEL), full(mods_l), full(n1), full(w_in_b), tab, tab, tab, full(qn), full(kn), full(lb), full(gm)],
        out_specs=[row(256), row(128), row(128), row(256), row(128), row(128), row(512), two, two, row(512), row(512)],
        out_shape=[sd(256, BF16), sd(128, BF16), sd(128, BF16), sd(256, BF16), sd(128, BF16), sd(128, BF16),
                   sd(512, F32), jax.ShapeDtypeStruct((2, T_ALL, 512), F32), jax.ShapeDtypeStruct((2, T_ALL, 512), F32),
                   sd(512, F32), sd(512, F32)],
        compiler_params=_cparams(("arbitrary",)),
        name="in_proj",
    )(x_all, mods_l, n1, w_in_b, cos_t, sa_t, sb_t, qn, kn, lb, gm)


def _attn_core(q_ref, parts, sink_ref, o_ref, tq):
    lane = lax.broadcasted_iota(jnp.int32, (tq, LANES), 1)
    k = jnp.concatenate([p[0] for p in parts], axis=0)
    v = jnp.concatenate([p[1] for p in parts], axis=0)
    n_masked = max([j + 1 for j, p in enumerate(parts) if p[2] is not None], default=0)
    if n_masked:
        bias = jnp.concatenate(
            [jnp.where(p[2], 0.0, NEG_INF) if p[2] is not None else jnp.zeros((tq, p[0].shape[0]), F32)
             for p in parts[:n_masked]], axis=1)
        l_masked = bias.shape[1]
    outs = []
    for h in range(4):
        kvh = h // 2
        q128 = q_ref[:, kvh * LANES:(kvh + 1) * LANES].astype(F32)
        if (h % 2) != kvh:
            q128 = pltpu.roll(q128, HEAD_DIM, 1)
        qm = jnp.where((lane >= kvh * HEAD_DIM) & (lane < (kvh + 1) * HEAD_DIM), q128, 0.0).astype(BF16)
        s = _dot_nt(qm, k)
        if n_masked:
            s = jnp.concatenate([jnp.where(bias < 0.0, NEG_INF, s[:, :l_masked]), s[:, l_masked:]], axis=1)
        m = s.max(axis=-1, keepdims=True)
        if sink_ref is not None:
            sk = sink_ref[:, h:h + 1]
            m = jnp.maximum(m, sk)
            den = jnp.exp(sk - m)
        else:
            den = jnp.zeros_like(m)
        p = jnp.exp(s - m)
        den = den + p.sum(axis=-1, keepdims=True)
        outs.append(_dot(p.astype(BF16), v) / den)
    for c in range(2):
        a, b = outs[2 * c], outs[2 * c + 1]
        if c == 0:
            b = pltpu.roll(b, HEAD_DIM, 1)
        else:
            a = pltpu.roll(a, HEAD_DIM, 1)
        o_ref[:, c * LANES:(c + 1) * LANES] = jnp.where(lane < HEAD_DIM, a, b).astype(BF16)


A_TQ = 512


def _attn_a_kernel(q_ref, kp_ref, kc_ref, kn_ref, vp_ref, vc_ref, vn_ref, kx_ref, vx_ref, sink_ref, o_ref):
    n = pl.program_id(1)
    qi = lax.broadcasted_iota(jnp.int32, (A_TQ, WINDOW), 0)
    kj = lax.broadcasted_iota(jnp.int32, (A_TQ, WINDOW), 1)
    mask_prev = (kj >= qi) & (n > 0)
    mask_next = (kj <= qi - (A_TQ - WINDOW)) & (n < SEQ // A_TQ - 1)
    qc = lax.broadcasted_iota(jnp.int32, (A_TQ, A_TQ), 0)
    kc = lax.broadcasted_iota(jnp.int32, (A_TQ, A_TQ), 1)
    mask_cur = jnp.abs(qc - kc) <= WINDOW
    parts = [(kp_ref[...], vp_ref[...], mask_prev), (kc_ref[...], vc_ref[...], mask_cur),
             (kn_ref[...], vn_ref[...], mask_next), (kx_ref[...], vx_ref[...], None)]
    _attn_core(q_ref, parts, sink_ref, o_ref, A_TQ)


def _attn_a_latent(qa, ka, va, sink):
    nq = SEQ // A_TQ
    nb = SEQ // WINDOW
    per = A_TQ // WINDOW
    q_spec = pl.BlockSpec((A_TQ, 256), lambda b, n: (b * nq + n, 0))
    prev = pl.BlockSpec((WINDOW, LANES), lambda b, n: (b * nb + jnp.maximum(n * per - 1, 0), 0))
    cur = pl.BlockSpec((A_TQ, LANES), lambda b, n: (b * nq + n, 0))
    nxt = pl.BlockSpec((WINDOW, LANES), lambda b, n: (b * nb + jnp.minimum((n + 1) * per, nb - 1), 0))
    ctx = pl.BlockSpec((CTX_LEN, LANES), lambda b, n: (N_LAT // CTX_LEN + b, 0))
    return pl.pallas_call(
        _attn_a_kernel,
        grid=(BATCH, nq),
        in_specs=[q_spec, prev, cur, nxt, prev, cur, nxt, ctx, ctx, pl.BlockSpec((1, LANES), lambda b, n: (0, 0))],
        out_specs=pl.BlockSpec((A_TQ, 256), lambda b, n: (b * nq + n, 0)),
        out_shape=jax.ShapeDtypeStruct((N_LAT, 256), BF16),
        compiler_params=_cparams(("arbitrary", "arbitrary")),
        name="attn_window",
    )(qa, ka, ka, ka, va, va, va, ka, va, sink)


B_TQ = 512


def _attn_b_kernel(q_ref, kl_ref, vl_ref, kx_ref, vx_ref, o_ref):
    parts = [(kl_ref[...], vl_ref[...], None), (kx_ref[...], vx_ref[...], None)]
    _attn_core(q_ref, parts, None, o_ref, B_TQ)


def _attn_b_latent(qb, kb, vb):
    nq = SEQ // B_TQ
    lat = pl.BlockSpec((SEQ, LANES), lambda b, j: (b, 0))
    ctx = pl.BlockSpec((CTX_LEN, LANES), lambda b, j: (N_LAT // CTX_LEN + b, 0))
    return pl.pallas_call(
        _attn_b_kernel,
        grid=(BATCH, nq),
        in_specs=[pl.BlockSpec((B_TQ, 256), lambda b, j: (b * nq + j, 0)), lat, lat, ctx, ctx],
        out_specs=pl.BlockSpec((B_TQ, 256), lambda b, j: (b * nq + j, 0)),
        out_shape=jax.ShapeDtypeStruct((N_LAT, 256), BF16),
        compiler_params=_cparams(("arbitrary", "arbitrary")),
        name="attn_full",
    )(qb, kb, vb, kb, vb)


def _attn_ctx_sink_kernel(q_ref, kx_ref, vx_ref, sink_ref, o_ref):
    _attn_core(q_ref, [(kx_ref[...], vx_ref[...], None)], sink_ref, o_ref, CTX_LEN)


def _attn_ctx_kernel(q_ref, kx_ref, vx_ref, o_ref):
    _attn_core(q_ref, [(kx_ref[...], vx_ref[...], None)], None, o_ref, CTX_LEN)


def _attn_ctx(q, k, v, sink):
    blk = lambda w: pl.BlockSpec((CTX_LEN, w), lambda b: (N_LAT // CTX_LEN + b, 0))
    in_specs = [blk(256), blk(LANES), blk(LANES)]
    args = [q, k, v]
    if sink is not None:
        in_specs.append(pl.BlockSpec((1, LANES), lambda b: (0, 0)))
        args.append(sink)
    return pl.pallas_call(
        _attn_ctx_sink_kernel if sink is not None else _attn_ctx_kernel,
        grid=(BATCH,),
        in_specs=in_specs,
        out_specs=pl.BlockSpec((CTX_LEN, 256), lambda b: (b, 0)),
        out_shape=jax.ShapeDtypeStruct((N_CTX, 256), BF16),
        compiler_params=_cparams(("arbitrary",)),
        name="attn_ctx_sink" if sink is not None else "attn_ctx",
    )(*args)


def _scan_constants():
    c = SCAN_CHUNK
    t = np.arange(c)[:, None]
    u = np.arange(c)[None, :]
    mats = [(u <= t)]
    masks = []
    for lvl in range(SCAN_LEVELS):
        m = 2 ** lvl
        second = ((t // m) % 2 == 1)
        end_first = (t // (2 * m)) * 2 * m + m - 1
        mats.append((second & (u > end_first) & (u <= t)) | ((~second) & (u > t) & (u <= end_first)))
        s = u
        masks.append(second & ((s // m) % 2 == 0) & ((s // (2 * m)) == (t // (2 * m))))
    fwd = np.concatenate(mats, axis=0).astype(np.float32)
    fmask = np.stack(masks).astype(np.float32)
    bwd = fwd.reshape(SCAN_MATS, c, c)[:, ::-1, ::-1].reshape(SCAN_MATS * c, c)
    bmask = fmask[:, ::-1, ::-1]
    both = np.stack([fwd, bwd])
    pair_masks = np.stack([fmask, bmask])
    return np.concatenate([both, both], axis=2), np.concatenate([pair_masks, pair_masks], axis=3)


def _scan_chunk(d, q_ref, k_ref, g_ref, v_ref, mall_ref, mask_ref, o_ref, st_ref):
    c = SCAN_CHUNK
    w = 2 * C_KDIM
    lf = g_ref[0]
    hi, lo = _split_bf16(lf)
    sums = _dot(mall_ref[d], jnp.concatenate([hi, lo], axis=0))
    ex = jnp.exp(sums)
    total = jnp.sum(lf, axis=0, keepdims=True)
    ex_after = jnp.exp(total - sums[0:c])
    tot = jnp.exp(total)
    first = lax.broadcasted_iota(jnp.int32, (c, w), 1) < C_KDIM
    row_first = lax.broadcasted_iota(jnp.int32, (w, w), 0) < C_KDIM
    col_first = lax.broadcasted_iota(jnp.int32, (w, w), 1) < C_KDIM
    same_head = (row_first == col_first).astype(F32)

    def block_diag(a):
        return jnp.concatenate([jnp.where(first, a, 0.0), jnp.where(first, 0.0, a)], axis=0).astype(BF16)

    for pair in range(C_HEADS // 2):
        sl = slice(pair * w, (pair + 1) * w)
        q = q_ref[:, sl]
        k = k_ref[0, :, sl]
        v = v_ref[:, sl]
        exh = lambda idx: ex[idx * c:(idx + 1) * c, sl]
        st = st_ref[d, pair]
        inter = _dot_nt((q * exh(0)).astype(BF16), st.astype(BF16))
        att = jnp.zeros((c, w), F32)
        for lvl in range(SCAN_LEVELS):
            e = exh(1 + lvl)
            pr = _dot_nt((q * e).astype(BF16), block_diag(k * e))
            att = att + mask_ref[d, lvl] * pr
        diag = _dot((q * k).astype(BF16), same_head.astype(BF16))
        o_ref[:, sl] = inter + _dot(att.astype(BF16), block_diag(v)) + diag * v
        khat = (k * ex_after[:, sl]).astype(BF16)
        st_ref[d, pair] = st * tot[:, sl] + same_head * _dot(v.T.astype(BF16), khat)


def _scan_kernel(qf_ref, kf_ref, gf_ref, vf_ref, qb_ref, kb_ref, gb_ref, vb_ref, mall_ref, mask_ref,
                 of_ref, ob_ref, st_ref):
    @pl.when(pl.program_id(1) == 0)
    def _():
        st_ref[...] = jnp.zeros_like(st_ref)

    _scan_chunk(0, qf_ref, kf_ref, gf_ref, vf_ref, mall_ref, mask_ref, of_ref, st_ref)
    _scan_chunk(1, qb_ref, kb_ref, gb_ref, vb_ref, mall_ref, mask_ref, ob_ref, st_ref)


def _hgrn_scan(hq, hk, hg, hv, mall, masks):
    c = SCAN_CHUNK
    n_ctx_chunks = CTX_LEN // c
    n_lat_chunks = SEQ // c
    steps = n_ctx_chunks + n_lat_chunks

    def rb(d):
        def index(b, j):
            ctx_blk = N_LAT // c + n_ctx_chunks * b + (j if d == 0 else n_ctx_chunks - 1 - j)
            jl = j - n_ctx_chunks
            lat_blk = b * n_lat_chunks + (jl if d == 0 else n_lat_chunks - 1 - jl)
            return jnp.where(j < n_ctx_chunks, ctx_blk, lat_blk)
        return index

    row = lambda d: pl.BlockSpec((c, 512), lambda b, j: (rb(d)(b, j), 0))
    drow = lambda d: pl.BlockSpec((1, c, 512), lambda b, j: (d, rb(d)(b, j), 0))
    full = lambda a: pl.BlockSpec(a.shape, lambda b, j: (0,) * a.ndim)
    o_sd = jax.ShapeDtypeStruct((T_ALL, 512), F32)
    return pl.pallas_call(
        _scan_kernel,
        grid=(BATCH, steps),
        in_specs=[row(0), drow(0), drow(0), row(0), row(1), drow(1), drow(1), row(1), full(mall), full(masks)],
        out_specs=[row(0), row(1)],
        out_shape=[o_sd, o_sd],
        scratch_shapes=[pltpu.VMEM((2, C_HEADS // 2, 2 * C_VDIM, 2 * C_KDIM), F32)],
        compiler_params=_cparams(("arbitrary", "arbitrary")),
        name="hgrn_scan",
    )(hq, hk, hg, hv, hq, hk, hg, hv, mall, masks)


def _out_kernel(ya_ref, yb_ref, of_ref, ob_ref, gate_ref, x_ref, mod_ref, gw_ref, w_ref, n2_ref,
                rwh_ref, rwl_ref, rb_ref, tri_ref, xo_ref, h2_ref, ti_ref, tg_ref, rank_ref, cnt_ref, carry_ref):
    i = pl.program_id(0)
    midx = jnp.minimum(i // (SEQ // ROW_BLK), BATCH)
    mod = mod_ref[pl.ds(midx, 1), :]
    g1 = mod[:, 2 * D_MODEL:3 * D_MODEL]
    sh2 = mod[:, 3 * D_MODEL:4 * D_MODEL]
    sc2 = mod[:, 4 * D_MODEL:5 * D_MODEL]

    o = of_ref[...] + ob_ref[...]
    y = _dot(ya_ref[...], w_ref[0:256, :]) + _dot(yb_ref[...], w_ref[256:512, :])
    for h in range(C_HEADS):
        sl = slice(h * C_VDIM, (h + 1) * C_VDIM)
        oh = o[:, sl]
        yn = oh * lax.rsqrt(jnp.mean(oh * oh, axis=-1, keepdims=True) + EPS) * gw_ref[...]
        yc = (yn * gate_ref[:, sl]).astype(BF16)
        y = y + _dot(yc, w_ref[512 + h * C_VDIM:512 + (h + 1) * C_VDIM, :])
    x = x_ref[...] + g1 * y
    xo_ref[...] = x
    h2 = x * lax.rsqrt(jnp.mean(x * x, axis=-1, keepdims=True) + EPS) * n2_ref[...]
    h2 = h2 * (1.0 + sc2) + sh2
    _store_token_tiles(h2_ref, _pack_bf16_pairs(h2), PACKED_TILE_ROWS)

    hi, lo = _split_bf16(h2)
    rwh = rwh_ref[...]
    logits = _dot(hi, rwh) + _dot(hi, rwl_ref[...]) + _dot(lo, rwh) + rb_ref[...]
    lane = lax.broadcasted_iota(jnp.int32, (ROW_BLK, LANES), 1).astype(F32)
    ti = jnp.zeros((ROW_BLK, LANES), F32)
    ex = jnp.zeros((ROW_BLK, LANES), F32)
    den = jnp.zeros((ROW_BLK, 1), F32)
    top = None
    onehots = []
    for k in range(TOP_K):
        m = logits.max(axis=-1, keepdims=True)
        idx = jnp.min(jnp.where(logits == m, lane, float(LANES)), axis=-1, keepdims=True)
        if top is None:
            top = m
        e = jnp.exp(m - top)
        den = den + e
        ti = jnp.where(lane == float(k), idx, ti)
        ex = jnp.where(lane == float(k), e, ex)
        chosen = lane == idx
        onehots.append(chosen.astype(F32))
        logits = jnp.where(chosen, -3.0e38, logits)
    ti_ref[...] = ti.T[0:8, :].astype(jnp.int32)
    tg_ref[...] = ex / den

    @pl.when(i == 0)
    def _():
        carry_ref[...] = jnp.zeros_like(carry_ref)

    tot = onehots[0] + onehots[1] + onehots[2] + onehots[3]
    before = _dot(tri_ref[...], tot.astype(BF16)) + carry_ref[0:1, :]
    r = jnp.zeros((ROW_BLK, LANES), F32)
    for k in range(TOP_K):
        rk = jnp.sum(onehots[k] * before, axis=-1, keepdims=True)
        r = jnp.where(lane == float(k), rk, r)
    rank_ref[...] = r.T[0:8, :].astype(jnp.int32)
    carry = carry_ref[0:1, :] + jnp.sum(tot, axis=0, keepdims=True)
    carry_ref[0:1, :] = carry
    cnt_ref[...] = jnp.broadcast_to(carry, (8, LANES))


def _out_proj(n_rows, ya, yb, o_f, o_b, gate, x_all, mods_l, gw, w_out_b, n2, rwh, rwl, rb, tri):
    nblk = n_rows // ROW_BLK
    row = lambda w: pl.BlockSpec((ROW_BLK, w), lambda i: (i, 0))
    full = lambda a: pl.BlockSpec(a.shape, lambda i: (0,) * a.ndim)
    sd = lambda w, dt: jax.ShapeDtypeStruct((n_rows, w), dt)
    return pl.pallas_call(
        _out_kernel,
        grid=(nblk,),
        in_specs=[row(256), row(256), row(512), row(512),
                  row(512), row(D_MODEL), full(mods_l), full(gw), full(w_out_b), full(n2),
                  full(rwh), full(rwl), full(rb), full(tri)],
        out_specs=[row(D_MODEL), pl.BlockSpec((ROW_BLK * PACKED_TILE_ROWS, LANES), lambda i: (i, 0)),
                   pl.BlockSpec((8, ROW_BLK), lambda i: (0, i)), row(LANES),
                   pl.BlockSpec((8, ROW_BLK), lambda i: (0, i)), pl.BlockSpec((8, LANES), lambda i: (0, 0))],
        out_shape=[sd(D_MODEL, F32), jax.ShapeDtypeStruct((n_rows * PACKED_TILE_ROWS, LANES), jnp.uint32),
                   jax.ShapeDtypeStruct((8, n_rows), jnp.int32), sd(LANES, F32),
                   jax.ShapeDtypeStruct((8, n_rows), jnp.int32), jax.ShapeDtypeStruct((8, LANES), F32)],
        scratch_shapes=[pltpu.VMEM((8, LANES), F32)],
        compiler_params=_cparams(("arbitrary",)),
        name="out_proj_router",
    )(ya, yb, o_f, o_b, gate, x_all, mods_l, gw, w_out_b, n2, rwh, rwl, rb, tri)


DEINT = 256


SCATTER_TOKENS = 256


def _dispatch_kernel(pad_blk_ref, pad_new_ref, dst_ref, h_ref, xs_ref, zbuf, sem):
    n = PACKED_TILE_ROWS
    blk_rows = MOE_BLOCK * n

    def zero_fill(j):
        start = pl.multiple_of(pad_blk_ref[j] * blk_rows, blk_rows)
        return pltpu.make_async_copy(zbuf, xs_ref.at[pl.ds(start, blk_rows), :], sem)

    @pl.when(pl.program_id(0) == 0)
    def _():
        zbuf[...] = jnp.zeros_like(zbuf)
        for j in range(2 * N_EXPERTS):
            @pl.when(pad_new_ref[j] == 1)
            def _():
                zero_fill(j).start()
        for j in range(2 * N_EXPERTS):
            @pl.when(pad_new_ref[j] == 1)
            def _():
                zero_fill(j).wait()

    def copy(t, k, dst_row):
        return pltpu.make_async_copy(h_ref.at[pl.ds(t * n, n), :],
                                     xs_ref.at[pl.ds(pl.multiple_of(dst_row, n), n), :], sem)

    for t in range(SCATTER_TOKENS):
        for k in range(TOP_K):
            copy(t, k, dst_ref[0, 0, k * SCATTER_TOKENS + t]).start(priority=k % 2)
    for t in range(SCATTER_TOKENS):
        for k in range(TOP_K):
            copy(t, k, 0).wait()


def _dispatch(h2t, dst, pad_blk, pad_new, p_rows):
    nblk = dst.shape[0]
    shape = (p_rows * PACKED_TILE_ROWS, LANES)
    grid_spec = pltpu.PrefetchScalarGridSpec(
        num_scalar_prefetch=2,
        grid=(nblk,),
        in_specs=[pl.BlockSpec((1, 1, SCATTER_TOKENS * TOP_K), lambda i, pb, pn: (i, 0, 0), memory_space=pltpu.SMEM),
                  pl.BlockSpec((SCATTER_TOKENS * PACKED_TILE_ROWS, LANES), lambda i, pb, pn: (i, 0))],
        out_specs=pl.BlockSpec(memory_space=pl.ANY),
        scratch_shapes=[pltpu.VMEM((MOE_BLOCK * PACKED_TILE_ROWS, LANES), jnp.uint32), pltpu.SemaphoreType.DMA(())],
    )
    return pl.pallas_call(
        _dispatch_kernel,
        grid_spec=grid_spec,
        out_shape=jax.ShapeDtypeStruct(shape, jnp.uint32),
        compiler_params=_cparams(("arbitrary",)),
        name="moe_dispatch",
    )(pad_blk, pad_new, dst, h2t)


def _moe_kernel(blk_e_ref, grp_slot_ref, next_e_ref, x_ref, w1_ref, b1_ref, w2_ref, b2_ref, perm_ref, y_ref,
                wf1, wf2, w1p_ref, w2p_ref, wsem, *, layer):
    i = pl.program_id(0)

    def fetch(e, s):
        return (pltpu.make_async_copy(w1_ref.at[layer, e], wf1.at[s], wsem.at[s]),
                pltpu.make_async_copy(w2_ref.at[layer, e], wf2.at[s], wsem.at[s]))

    @pl.when(i == 0)
    def _():
        for c in fetch(blk_e_ref[0], 0):
            c.start()

    new_expert = (i == 0) | (blk_e_ref[i] != blk_e_ref[jnp.maximum(i - 1, 0)])

    @pl.when(new_expert)
    def _():
        s = grp_slot_ref[i]
        for c in fetch(0, s):
            c.wait()

        @pl.when(next_e_ref[i] >= 0)
        def _():
            for c in fetch(next_e_ref[i], 1 - s):
                c.start()

        perm = perm_ref[...]
        half = DEINT // 2
        for j in range(2 * D_FF // DEINT):
            r = _dot(wf1[s, :, j * DEINT:(j + 1) * DEINT].astype(BF16), perm)
            w1p_ref[:, j * half:(j + 1) * half] = r[:, 0:half].astype(BF16)
            w1p_ref[:, D_FF + j * half:D_FF + (j + 1) * half] = r[:, half:DEINT].astype(BF16)
        w2p_ref[...] = wf2[s].astype(BF16)

    x = _unpack_bf16_pairs(_load_token_tiles(x_ref, MOE_BLOCK, rows_per_token=PACKED_TILE_ROWS))
    u = _dot(x, w1p_ref[...]) + b1_ref[0, 0]
    u_glu = jnp.minimum(u[:, 0:D_FF], SWIGLU_LIMIT)
    u_lin = jnp.clip(u[:, D_FF:2 * D_FF], -SWIGLU_LIMIT, SWIGLU_LIMIT)
    act = u_glu * jax.nn.sigmoid(SWIGLU_ALPHA * u_glu) * (u_lin + 1.0)
    y = _dot(act.astype(BF16), w2p_ref[...]) + b2_ref[0, 0]
    _store_token_tiles(y_ref, y)


def _expert_mlp(layer, xs, blk_e, grp_slot, next_e, w1, b1p, w2, b2, perm):
    n_blocks = blk_e.shape[0]
    grid_spec = pltpu.PrefetchScalarGridSpec(
        num_scalar_prefetch=3,
        grid=(n_blocks,),
        in_specs=[pl.BlockSpec((MOE_BLOCK * PACKED_TILE_ROWS, LANES), lambda i, be, gs, ne: (i, 0)),
                  pl.BlockSpec(memory_space=pl.ANY),
                  pl.BlockSpec((1, 1, 1, 2 * D_FF), lambda i, be, gs, ne: (layer, be[i], 0, 0)),
                  pl.BlockSpec(memory_space=pl.ANY),
                  pl.BlockSpec((1, 1, 1, D_MODEL), lambda i, be, gs, ne: (layer, be[i], 0, 0)),
                  pl.BlockSpec((DEINT, DEINT), lambda i, be, gs, ne: (0, 0))],
        out_specs=pl.BlockSpec((MOE_BLOCK * TILE_ROWS, LANES), lambda i, be, gs, ne: (i, 0)),
        scratch_shapes=[pltpu.VMEM((2, D_MODEL, 2 * D_FF), F32), pltpu.VMEM((2, D_FF, D_MODEL), F32),
                        pltpu.VMEM((D_MODEL, 2 * D_FF), BF16), pltpu.VMEM((D_FF, D_MODEL), BF16),
                        pltpu.SemaphoreType.DMA((2,))],
    )
    return pl.pallas_call(
        functools.partial(_moe_kernel, layer=layer),
        grid_spec=grid_spec,
        out_shape=jax.ShapeDtypeStruct((n_blocks * MOE_BLOCK * TILE_ROWS, LANES), F32),
        compiler_params=_cparams(("arbitrary",)),
        name="expert_mlp",
    )(blk_e, grp_slot, next_e, xs, w1, b1p, w2, b2, perm)


def _combine_kernel(src_ref, src_next_ref, ys_ref, tg_ref, x_ref, mod_ref, nf_ref, o_ref, ybuf, sem, *, final):
    i = pl.program_id(0)
    last = pl.num_programs(0) - 1
    slot = i % 2
    n = TILE_ROWS
    n_copies = TOP_K * ROW_BLK
    rows = lambda start: pl.ds(pl.multiple_of(start, n), n)

    def gather(src_row, r, s):
        return pltpu.make_async_copy(ys_ref.at[rows(src_row), :], ybuf.at[s, rows(r * n), :], sem.at[s])

    @pl.when(i == 0)
    def _():
        for r in range(n_copies):
            gather(src_ref[0, 0, r], r, 0).start(priority=r % 2)

    for r in range(n_copies):
        gather(src_next_ref[0, 0, r], r, 1 - slot).start(priority=r % 2)
    for r in range(n_copies):
        gather(0, r, slot).wait()

    midx = jnp.minimum(i // (SEQ // ROW_BLK), BATCH)
    g2 = mod_ref[pl.ds(midx, 1), 5 * D_MODEL:6 * D_MODEL]
    tg = tg_ref[...]
    f = jnp.zeros((ROW_BLK, D_MODEL), F32)
    for k in range(TOP_K):
        cols = [ybuf[slot, pl.ds(k * ROW_BLK * n + c, ROW_BLK, stride=n), :] for c in range(n)]
        f = f + jnp.concatenate(cols, axis=1) * tg[:, k:k + 1]
    x = x_ref[...] + g2 * f
    if final:
        x = x * lax.rsqrt(jnp.mean(x * x, axis=-1, keepdims=True) + EPS) * nf_ref[...]
    o_ref[...] = x

    @pl.when(i == last)
    def _():
        for r in range(n_copies):
            gather(0, r, 1 - slot).wait()


def _combine(ys, src, tg, x_mid, mods_l, nf, final):
    n_rows = x_mid.shape[0]
    nblk = n_rows // ROW_BLK
    full = lambda a: pl.BlockSpec(a.shape, lambda i: (0,) * a.ndim)
    idx = lambda f: pl.BlockSpec((1, 1, TOP_K * ROW_BLK), f, memory_space=pltpu.SMEM)
    return pl.pallas_call(
        functools.partial(_combine_kernel, final=final),
        grid=(nblk,),
        in_specs=[idx(lambda i: (i, 0, 0)), idx(lambda i: (jnp.minimum(i + 1, nblk - 1), 0, 0)),
                  pl.BlockSpec(memory_space=pl.ANY),
                  pl.BlockSpec((ROW_BLK, LANES), lambda i: (i, 0)),
                  pl.BlockSpec((ROW_BLK, D_MODEL), lambda i: (i, 0)), full(mods_l), full(nf)],
        out_specs=pl.BlockSpec((ROW_BLK, D_MODEL), lambda i: (i, 0)),
        out_shape=jax.ShapeDtypeStruct((n_rows, D_MODEL), F32),
        scratch_shapes=[pltpu.VMEM((2, TOP_K * ROW_BLK * TILE_ROWS, LANES), F32), pltpu.SemaphoreType.DMA((2,))],
        compiler_params=_cparams(("arbitrary",)),
        name="moe_combine_final" if final else "moe_combine",
    )(src, src, ys, tg, x_mid, mods_l, nf)


def _moe(layer, h2t, top_i, top_g, rank, cnt, x_mid, mods_l, w1, b1p, w2, b2, nf, perm, final):
    n_rows = x_mid.shape[0]
    tk = n_rows * TOP_K
    n_blocks = -(-(tk + N_EXPERTS * (MOE_BLOCK - 1)) // MOE_BLOCK)
    p_rows = n_blocks * MOE_BLOCK

    counts = cnt[0, :N_EXPERTS].astype(jnp.int32)
    padded = (counts + MOE_BLOCK - 1) // MOE_BLOCK * MOE_BLOCK
    pend = jnp.cumsum(padded)
    pstart = pend - padded
    eids = jnp.arange(N_EXPERTS, dtype=jnp.int32)[:, None, None]
    start_of = jnp.sum(jnp.where(top_i[None, :TOP_K] == eids, pstart[:, None, None], 0), axis=0)
    dest = start_of + rank[:TOP_K]
    dest_blk = dest.reshape(TOP_K, n_rows // ROW_BLK, ROW_BLK).transpose(1, 0, 2)
    dest_blk = dest_blk.reshape(n_rows // ROW_BLK, 1, TOP_K * ROW_BLK)
    blk_start = jnp.arange(n_blocks, dtype=jnp.int32) * MOE_BLOCK
    blk_e = jnp.minimum(jnp.sum((pend[None, :] <= blk_start[:, None]).astype(jnp.int32), axis=1), N_EXPERTS - 1)
    change = jnp.concatenate([jnp.ones((1,), jnp.int32), (blk_e[1:] != blk_e[:-1]).astype(jnp.int32)])
    grp_slot = (jnp.cumsum(change) - 1) & 1
    experts = jnp.arange(N_EXPERTS, dtype=jnp.int32)
    present = jnp.any(blk_e[:, None] == experts[None, :], axis=0)
    later = present[None, :] & (experts[None, :] > blk_e[:, None])
    next_e = jnp.min(jnp.where(later, experts[None, :], N_EXPERTS), axis=1)
    next_e = jnp.where(next_e == N_EXPERTS, -1, next_e).astype(jnp.int32)

    dst_x = dest_blk * PACKED_TILE_ROWS
    last_blk = jnp.maximum(pend // MOE_BLOCK - 1, 0)
    tail_blk = jnp.minimum(pend[-1] // MOE_BLOCK + experts, n_blocks - 1)
    pad_blk = jnp.concatenate([last_blk, tail_blk]).astype(jnp.int32)
    pad_new = jnp.concatenate([jnp.ones((1,), jnp.int32), (pad_blk[1:] != pad_blk[:-1]).astype(jnp.int32)])
    xs = _dispatch(h2t, dst_x, pad_blk, pad_new, p_rows)
    ys = _expert_mlp(layer, xs, blk_e, grp_slot.astype(jnp.int32), next_e, w1, b1p, w2, b2, perm)
    return _combine(ys, dest_blk * TILE_ROWS, top_g, x_mid, mods_l, nf, final)


def _rope_tables():
    rows = SEQ // GRID_W
    row = jnp.repeat(jnp.arange(rows), GRID_W).astype(F32)
    col = jnp.tile(jnp.arange(GRID_W), rows).astype(F32)
    half = HEAD_DIM // 2
    inv = 1.0 / (ROPE_THETA ** (jnp.arange(0, half, 2, dtype=F32) / half))
    ang_r = row[:, None] * inv
    ang_c = col[:, None] * inv
    ang = jnp.concatenate([ang_r, ang_r, ang_c, ang_c], axis=-1)
    cos, sin = jnp.cos(ang), jnp.sin(ang)
    first = (jnp.arange(HEAD_DIM) % 32) < 16
    sa = jnp.where(first, -sin, 0.0)
    sb = jnp.where(first, 0.0, sin)
    ext = lambda t, fill: jnp.concatenate(
        [jnp.tile(t, (1, 2)), jnp.full((ROW_BLK, LANES), fill, F32)], axis=0)
    return ext(cos, 1.0), ext(sa, 0.0), ext(sb, 0.0)


def kernel(x, c, ctx, c_ctx, w_mod, b_mod, norm1_w, norm2_w, w_in, w_out, attn_sink, q_norm_w, k_norm_w,
           hgrn_lb, gate_norm_w, router_w, router_b, w1, b1, w2, b2, final_norm_w):
    cos_t, sa_t, sb_t = _rope_tables()
    lbs = jax.nn.softmax(hgrn_lb.astype(F32), axis=0)
    lbs = jnp.cumsum(lbs, axis=0) - lbs[0]
    mall_np, masks_np = _scan_constants()
    mall = jnp.asarray(mall_np, BF16)
    masks = jnp.asarray(masks_np, F32)
    gm = jnp.asarray(np.kron(np.eye(2), np.ones((HEAD_DIM, HEAD_DIM))), BF16)
    tri = jnp.asarray(np.tril(np.ones((ROW_BLK, ROW_BLK)), -1), BF16)

    c_all = jnp.concatenate([c, c_ctx[None, :], jnp.zeros((3, D_MODEL), F32)], axis=0)
    mods = _modulation(c_all, w_mod, b_mod)
    x_all = jnp.concatenate([x.reshape(N_LAT, D_MODEL), ctx.reshape(N_CTX, D_MODEL)], axis=0)
    nf = final_norm_w.reshape(1, D_MODEL)
    b1p = jnp.concatenate([b1[..., 0::2], b1[..., 1::2]], axis=-1).reshape(DEPTH, N_EXPERTS, 1, 2 * D_FF)
    b2r = b2.reshape(DEPTH, N_EXPERTS, 1, D_MODEL)
    src = np.concatenate([2 * np.arange(DEINT // 2), 2 * np.arange(DEINT // 2) + 1])
    perm = jnp.asarray(np.arange(DEINT)[:, None] == src[None, :], BF16)

    out = None
    for l in range(DEPTH):
        last = l == DEPTH - 1
        mods_l = mods[l]
        qn = jnp.tile(q_norm_w[l], 2).reshape(1, LANES)
        kn = jnp.tile(k_norm_w[l], 2).reshape(1, LANES)
        sink = jnp.zeros((1, LANES), F32).at[0, :A_HEADS].set(attn_sink[l])
        (qa, ka, va, qb, kb, vb, hq, hk, hg, hv, hgate) = _in_proj(
            x_all, mods_l, norm1_w[l].reshape(1, D_MODEL), w_in[l].astype(BF16), cos_t, sa_t, sb_t, qn, kn, lbs[l], gm)

        ya = _attn_a_latent(qa, ka, va, sink)
        yb = _attn_b_latent(qb, kb, vb)
        o_f, o_b = _hgrn_scan(hq, hk, hg, hv, mall, masks)
        if not last:
            ya = jnp.concatenate([ya, _attn_ctx(qa, ka, va, sink)], axis=0)
            yb = jnp.concatenate([yb, _attn_ctx(qb, kb, vb, None)], axis=0)
        n_rows = N_LAT if last else T_ALL

        rw = jnp.zeros((D_MODEL, LANES), F32).at[:, :N_EXPERTS].set(router_w[l])
        rwh, rwl = _split_bf16(rw)
        rb = jnp.full((1, LANES), NEG_INF, F32).at[0, :N_EXPERTS].set(router_b[l])
        x_mid, h2, top_i, top_g, rank, cnt = _out_proj(
            n_rows, ya, yb, o_f, o_b, hgate, x_all, mods_l, gate_norm_w[l].reshape(1, C_VDIM),
            w_out[l].astype(BF16), norm2_w[l].reshape(1, D_MODEL), rwh, rwl, rb, tri)

        x_all = _moe(l, h2, top_i, top_g, rank, cnt, x_mid, mods_l, w1, b1p, w2, b2r, nf, perm, last)
        out = x_all
    return out.reshape(BATCH, SEQ, D_MODEL)
```
